```python
import math
import jax, jax.numpy as jnp
from jax import lax
import numpy as np

D_MODEL = 2048
BATCH = 2
SEQ = 8192
DEPTH = 2
DEC_BATCH = 32
DEC_SEQ = 16
PAST_LEN = 1024

CHUNK = 64
EPS = 1e-6
R_HEADS = 8
R_DK = 128
R_DV = 256
R_QK = R_HEADS * R_DK
R_VAL = R_HEADS * R_DV
ROPE_BASE = 10000.0
S5_GROUP = 16
S5_WIDTH = D_MODEL
S5_GROUPS = S5_WIDTH // S5_GROUP
S5_STATE = 64
SSD_INNER = 2 * D_MODEL
SSD_HEADDIM = 64
SSD_HEADS = SSD_INNER // SSD_HEADDIM
SSD_GROUPS = 8
SSD_HPG = SSD_HEADS // SSD_GROUPS
SSD_STATE = 128
SSD_CONV_W = 4
SSD_CONV_DIM = SSD_INNER + 2 * SSD_GROUPS * SSD_STATE
MEM_LEN = 256
X_HEADS = 4
X_HEAD_DIM = D_MODEL // X_HEADS
D_FF = 4 * D_MODEL
N_BRANCH = 3
N_NORMS = 7
IN_SIZES = (R_QK, R_QK, R_VAL, R_VAL, S5_WIDTH, SSD_INNER, SSD_CONV_DIM, SSD_HEADS, N_BRANCH * D_MODEL)
IN_COLS = sum(IN_SIZES)
DT_MIN = 0.001
DT_MAX = 0.1

kernel_name = "hybrid_streaming_encoder_step"


def rmsnorm(x, g):
    xf = x.astype(jnp.float32)
    y = xf * lax.rsqrt(jnp.mean(xf * xf, axis=-1, keepdims=True) + EPS)
    return (y * g.astype(jnp.float32)).astype(x.dtype)


def to_chunks(t, n):
    b, l = t.shape[:2]
    return jnp.swapaxes(t.reshape(b, l // n, n, *t.shape[2:]), 0, 1)


def from_chunks(t):
    t = jnp.swapaxes(t, 0, 1)
    return t.reshape(t.shape[0], t.shape[1] * t.shape[2], *t.shape[3:])


def split_columns(proj):
    offsets = np.cumsum(IN_SIZES)[:-1].tolist()
    return jnp.split(proj, offsets, axis=-1)


def rope(x, pos):
    half = x.shape[-1] // 2
    inv_freq = jnp.exp(-math.log(ROPE_BASE) * jnp.arange(half, dtype=jnp.float32) / half)
    ang = pos.astype(jnp.float32)[:, None] * inv_freq[None]
    cos = jnp.cos(ang)[None, :, None]
    sin = jnp.sin(ang)[None, :, None]
    x = x.astype(jnp.float32)
    x1, x2 = x[..., :half], x[..., half:]
    return jnp.concatenate([x1 * cos - x2 * sin, x2 * cos + x1 * sin], axis=-1)


def retention_chunk(s, q, k, v, log_gamma):
    L = q.shape[1]
    idx = jnp.arange(L, dtype=jnp.float32)
    lg = log_gamma[:, None]
    intra = jnp.exp(jnp.abs(idx[:, None] - idx[None, :])[None] * lg[:, :, None])
    decay_in = jnp.exp((idx + 1.0)[None] * lg).T
    decay_up = jnp.exp((L - 1.0 - idx)[None] * lg).T
    decay_all = jnp.exp(L * log_gamma)
    q = q.astype(jnp.float32)
    k = k.astype(jnp.float32)
    v = v.astype(jnp.float32)
    scores = jnp.einsum('bihd,bjhd->bhij', q, k) * intra[None]
    out = (jnp.einsum('bhij,bjhe->bihe', scores, v)
           + jnp.einsum('bihd,bhde->bihe', q, s) * decay_in[None, :, :, None])
    s = decay_all[:, None, None] * s + jnp.einsum('bjhd,bjhe->bhde', k * decay_up[None, :, :, None], v)
    return s, out


def retention_branch(q, k, v, g, s0, pos, gn_gain, w_o):
    b, l = q.shape[:2]
    q = rope(q.reshape(b, l, R_HEADS, R_DK), pos)
    k = rope(k.reshape(b, l, R_HEADS, R_DK), pos) * (R_DK ** -0.5)
    v = v.reshape(b, l, R_HEADS, R_DV)
    n = min(l, CHUNK)
    log_gamma = jnp.log1p(-jnp.exp2(-5.0 - jnp.arange(R_HEADS, dtype=jnp.float32)))

    def step(s, qkv):
        qc, kc, vc = qkv
        return retention_chunk(s, qc, kc, vc, log_gamma)

    s_new, o = lax.scan(step, s0.astype(jnp.float32), (to_chunks(q, n), to_chunks(k, n), to_chunks(v, n)))
    o = from_chunks(o)
    mu = jnp.mean(o, axis=-1, keepdims=True)
    var = jnp.mean(jnp.square(o - mu), axis=-1, keepdims=True)
    o = ((o - mu) * lax.rsqrt(var + EPS)).reshape(b, l, R_VAL) * gn_gain.astype(jnp.float32)
    o = jax.nn.silu(g.astype(jnp.float32)) * o
    return o.astype(g.dtype) @ w_o, s_new


def cplx_affine(e1, e2):
    a1r, a1i, b1r, b1i = e1
    a2r, a2i, b2r, b2i = e2
    return (a1r * a2r - a1i * a2i, a1r * a2i + a1i * a2r,
            a2r * b1r - a2i * b1i + b2r, a2r * b1i + a2i * b1r + b2i)


def s5_discretize(a_re, a_im, b_re, b_im, log_dt):
    a_re = a_re.astype(jnp.float32)
    a_im = a_im.astype(jnp.float32)
    b_re = b_re.astype(jnp.float32)
    b_im = b_im.astype(jnp.float32)
    dt = jnp.exp(log_dt.astype(jnp.float32))[:, None]
    mag = jnp.exp(dt * a_re)
    ar = mag * jnp.cos(dt * a_im)
    ai = mag * jnp.sin(dt * a_im)
    nr, ni = ar - 1.0, ai
    den = a_re * a_re + a_im * a_im
    cr = (nr * a_re + ni * a_im) / den
    ci = (ni * a_re - nr * a_im) / den
    bbr = cr[..., None] * b_re - ci[..., None] * b_im
    bbi = cr[..., None] * b_im + ci[..., None] * b_re
    return ar, ai, bbr, bbi


def s5_branch(u, h0r, h0i, a_re, a_im, b_re, b_im, c_re, c_im, d, log_dt, w_glu):
    b, l = u.shape[:2]
    ar, ai, bbr, bbi = s5_discretize(a_re, a_im, b_re, b_im, log_dt)
    c_re = c_re.astype(jnp.float32)
    c_im = c_im.astype(jnp.float32)
    uf = u.astype(jnp.float32)
    n = min(l, CHUNK)

    def step(carry, uc):
        hr, hi = carry
        br = jnp.einsum('blgi,gpi->blgp', uc, bbr)
        bi = jnp.einsum('blgi,gpi->blgp', uc, bbi)
        br = br.at[:, 0].add(ar * hr - ai * hi)
        bi = bi.at[:, 0].add(ar * hi + ai * hr)
        elems = (jnp.broadcast_to(ar, br.shape), jnp.broadcast_to(ai, br.shape), br, bi)
        _, _, xr, xi = lax.associative_scan(cplx_affine, elems, axis=1)
        y = jnp.einsum('blgp,gip->blgi', xr, c_re) - jnp.einsum('blgp,gip->blgi', xi, c_im)
        return (xr[:, -1], xi[:, -1]), y

    (hr, hi), y = lax.scan(step, (h0r.astype(jnp.float32), h0i.astype(jnp.float32)),
                           to_chunks(uf.reshape(b, l, S5_GROUPS, S5_GROUP), n))
    y = from_chunks(y).reshape(b, l, S5_WIDTH) + d.astype(jnp.float32) * uf
    y = jax.nn.gelu(y).astype(u.dtype)
    val, gate = jnp.split(y @ w_glu, 2, axis=-1)
    return val * jax.nn.sigmoid(gate), hr, hi


def ssd_chunk(h, x, bm, cm, dt, a):
    L = x.shape[1]
    cs = jnp.cumsum(dt * a, axis=1)
    causal = jnp.tril(jnp.ones((L, L), dtype=bool))
    seg = cs[:, :, None] - cs[:, None, :]
    decay = jnp.exp(jnp.where(causal[None, :, :, None, None], seg, -jnp.inf))
    xdt = x * dt[..., None]
    cb = jnp.einsum('btgn,bsgn->btsg', cm, bm)
    y = (jnp.einsum('btsg,btsgr,bsgrp->btgrp', cb, decay, xdt)
         + jnp.einsum('btgn,bgrpn->btgrp', cm, h) * jnp.exp(cs)[..., None])
    w = jnp.exp(cs[:, -1:] - cs)
    h = jnp.exp(cs[:, -1])[..., None, None] * h + jnp.einsum('bsgn,bsgr,bsgrp->bgrpn', bm, w, xdt)
    return h, y


def ssd_branch(z, xbc, dt_raw, h0, conv0, conv_w, conv_b, dt_bias, a_log, d, norm_g, w_out):
    b, l = z.shape[:2]
    xpad = jnp.concatenate([conv0.astype(xbc.dtype), xbc], axis=1)
    conv_new = xpad[:, -(SSD_CONV_W - 1):]
    xc = lax.conv_general_dilated(xpad, conv_w.astype(xbc.dtype)[:, None, :], window_strides=(1,),
                                  padding='VALID', dimension_numbers=('NWC', 'WIO', 'NWC'),
                                  feature_group_count=SSD_CONV_DIM)
    xc = jax.nn.silu(xc + conv_b.astype(xc.dtype))
    xs, bm, cm = jnp.split(xc, [SSD_INNER, SSD_INNER + SSD_GROUPS * SSD_STATE], axis=-1)
    xs = xs.astype(jnp.float32).reshape(b, l, SSD_GROUPS, SSD_HPG, SSD_HEADDIM)
    bm = bm.astype(jnp.float32).reshape(b, l, SSD_GROUPS, SSD_STATE)
    cm = cm.astype(jnp.float32).reshape(b, l, SSD_GROUPS, SSD_STATE)
    dt = jax.nn.softplus(dt_raw.astype(jnp.float32) + dt_bias.astype(jnp.float32))
    dt = dt.reshape(b, l, SSD_GROUPS, SSD_HPG)
    a = -jnp.exp(a_log.astype(jnp.float32)).reshape(SSD_GROUPS, SSD_HPG)
    n = min(l, CHUNK)

    def step(h, inp):
        xch, bch, cch, dtch = inp
        return ssd_chunk(h, xch, bch, cch, dtch, a)

    h_init = h0.astype(jnp.float32).reshape(b, SSD_GROUPS, SSD_HPG, SSD_HEADDIM, SSD_STATE)
    h_new, y = lax.scan(step, h_init, (to_chunks(xs, n), to_chunks(bm, n), to_chunks(cm, n), to_chunks(dt, n)))
    y = from_chunks(y) + d.astype(jnp.float32).reshape(SSD_GROUPS, SSD_HPG)[..., None] * xs
    y = y.reshape(b, l, SSD_INNER) * jax.nn.silu(z.astype(jnp.float32))
    yg = y.reshape(b, l, SSD_GROUPS, SSD_INNER // SSD_GROUPS)
    yg = yg * lax.rsqrt(jnp.mean(yg * yg, axis=-1, keepdims=True) + EPS)
    y = (yg.reshape(b, l, SSD_INNER) * norm_g.astype(jnp.float32)).astype(z.dtype)
    return y @ w_out, h_new.reshape(b, SSD_HEADS, SSD_HEADDIM, SSD_STATE), conv_new


def memory_kv(mem, g, w_xkv):
    b, m = mem.shape[:2]
    k, v = jnp.split(rmsnorm(mem, g) @ w_xkv, 2, axis=-1)
    return k.reshape(b, m, X_HEADS, X_HEAD_DIM), v.reshape(b, m, X_HEADS, X_HEAD_DIM)


def cross_attn(x, mem_k, mem_v, w_q, w_o):
    b, l, _ = x.shape
    q = (x @ w_q).reshape(b, l, X_HEADS, X_HEAD_DIM)
    s = jnp.einsum('blhd,bmhd->bhlm', q, mem_k.astype(q.dtype)).astype(jnp.float32) * (X_HEAD_DIM ** -0.5)
    p = jax.nn.softmax(s, axis=-1).astype(x.dtype)
    o = jnp.einsum('bhlm,bmhd->blhd', p, mem_v.astype(x.dtype))
    return o.reshape(b, l, D_MODEL) @ w_o


def run_layer(x, pos, mem_k, mem_v, ret_s, s5_r, s5_i, ssd_s, conv_s,
              gains, w_in, ret_gn, w_ret_o, s5_a_re, s5_a_im, s5_b_re, s5_b_im, s5_c_re, s5_c_im,
              s5_d, s5_log_dt, w_s5_glu, ssd_conv_w, ssd_conv_b, ssd_dt_bias, ssd_a_log, ssd_d, ssd_norm,
              w_ssd_out, w_mix_out, w_xq, w_xo, w_up, w_down):
    hn = rmsnorm(x, gains[0])
    q, k, v, g, u, z, xbc, dt_raw, gate_logits = split_columns(hn @ w_in)
    ret_out, ret_s = retention_branch(q, k, v, g, ret_s, pos, ret_gn, w_ret_o)
    s5_out, s5_r, s5_i = s5_branch(u, s5_r, s5_i, s5_a_re, s5_a_im, s5_b_re, s5_b_im,
                                   s5_c_re, s5_c_im, s5_d, s5_log_dt, w_s5_glu)
    ssd_out, ssd_s, conv_s = ssd_branch(z, xbc, dt_raw, ssd_s, conv_s, ssd_conv_w, ssd_conv_b,
                                        ssd_dt_bias, ssd_a_log, ssd_d, ssd_norm, w_ssd_out)
    g_ret, g_s5, g_ssd = jnp.split(jax.nn.sigmoid(gate_logits.astype(jnp.float32)).astype(x.dtype), N_BRANCH, axis=-1)
    merged = g_ret * ret_out + g_s5 * s5_out + g_ssd * ssd_out
    x = x + rmsnorm(merged @ w_mix_out, gains[1])
    x = x + rmsnorm(cross_attn(rmsnorm(x, gains[2]), mem_k, mem_v, w_xq, w_xo), gains[3])
    hn = rmsnorm(x, gains[4])
    x = x + rmsnorm(jnp.square(jax.nn.relu(hn @ w_up)) @ w_down, gains[5])
    return x, (ret_s, s5_r, s5_i, ssd_s, conv_s)


def setup_inputs(seed: int = 0) -> dict:
    key = jax.random.key(seed)
    keys = list(jax.random.split(key, 48))

    def nrm(shape, scale=1.0):
        return scale * jax.random.normal(keys.pop(), shape, jnp.float32)

    def unif(shape, lo, hi):
        return jax.random.uniform(keys.pop(), shape, jnp.float32, lo, hi)

    dt_ssd = jnp.exp(unif((DEPTH, SSD_HEADS), math.log(DT_MIN), math.log(DT_MAX)))
    s5_a_im = (math.pi * jnp.arange(S5_STATE, dtype=jnp.float32))[None, None] + nrm((DEPTH, S5_GROUPS, S5_STATE), 0.01)
    inp = {}
    inp['x_prompt'] = nrm((BATCH, SEQ, D_MODEL))
    inp['x_sample'] = nrm((DEC_BATCH, DEC_SEQ, D_MODEL))
    inp['mem_prompt'] = nrm((BATCH, MEM_LEN, D_MODEL))
    inp['state_ret'] = nrm((DEPTH, DEC_BATCH, R_HEADS, R_DK, R_DV), 0.5)
    inp['state_s5_re'] = nrm((DEPTH, DEC_BATCH, S5_GROUPS, S5_STATE), 0.1)
    inp['state_s5_im'] = nrm((DEPTH, DEC_BATCH, S5_GROUPS, S5_STATE), 0.1)
    inp['state_ssd'] = nrm((DEPTH, DEC_BATCH, SSD_HEADS, SSD_HEADDIM, SSD_STATE), 0.1)
    inp['cache_ssd_conv'] = nrm((DEPTH, DEC_BATCH, SSD_CONV_W - 1, SSD_CONV_DIM))
    inp['cache_mem_k'] = nrm((DEPTH, DEC_BATCH, MEM_LEN, X_HEADS, X_HEAD_DIM))
    inp['cache_mem_v'] = nrm((DEPTH, DEC_BATCH, MEM_LEN, X_HEADS, X_HEAD_DIM))
    inp['norm_gains'] = 1.0 + nrm((DEPTH, N_NORMS, D_MODEL), 0.02)
    inp['w_in'] = nrm((DEPTH, D_MODEL, IN_COLS), D_MODEL ** -0.5)
    inp['ret_gn'] = 1.0 + nrm((DEPTH, R_VAL), 0.02)
    inp['w_ret_o'] = nrm((DEPTH, R_VAL, D_MODEL), R_VAL ** -0.5)
    inp['s5_a_re'] = -0.5 + nrm((DEPTH, S5_GROUPS, S5_STATE), 0.01)
    inp['s5_a_im'] = s5_a_im
    inp['s5_b_re'] = nrm((DEPTH, S5_GROUPS, S5_STATE, S5_GROUP), (2 * S5_GROUP) ** -0.5)
    inp['s5_b_im'] = nrm((DEPTH, S5_GROUPS, S5_STATE, S5_GROUP), (2 * S5_GROUP) ** -0.5)
    inp['s5_c_re'] = nrm((DEPTH, S5_GROUPS, S5_GROUP, S5_STATE), S5_STATE ** -0.5)
    inp['s5_c_im'] = nrm((DEPTH, S5_GROUPS, S5_GROUP, S5_STATE), S5_STATE ** -0.5)
    inp['s5_d'] = nrm((DEPTH, S5_WIDTH))
    inp['s5_log_dt'] = unif((DEPTH, S5_GROUPS), math.log(DT_MIN), math.log(DT_MAX))
    inp['w_s5_glu'] = nrm((DEPTH, S5_WIDTH, 2 * D_MODEL), S5_WIDTH ** -0.5)
    inp['ssd_conv_w'] = nrm((DEPTH, SSD_CONV_W, SSD_CONV_DIM), SSD_CONV_W ** -0.5)
    inp['ssd_conv_b'] = nrm((DEPTH, SSD_CONV_DIM), 0.01)
    inp['ssd_dt_bias'] = dt_ssd + jnp.log(-jnp.expm1(-dt_ssd))
    inp['ssd_a_log'] = jnp.log(unif((DEPTH, SSD_HEADS), 1.0, 16.0))
    inp['ssd_d'] = 1.0 + nrm((DEPTH, SSD_HEADS), 0.02)
    inp['ssd_norm'] = 1.0 + nrm((DEPTH, SSD_INNER), 0.02)
    inp['w_ssd_out'] = nrm((DEPTH, SSD_INNER, D_MODEL), SSD_INNER ** -0.5)
    inp['w_mix_out'] = nrm((DEPTH, D_MODEL, D_MODEL), D_MODEL ** -0.5)
    inp['w_xq'] = nrm((DEPTH, D_MODEL, D_MODEL), D_MODEL ** -0.5)
    inp['w_xkv'] = nrm((DEPTH, D_MODEL, 2 * D_MODEL), D_MODEL ** -0.5)
    inp['w_xo'] = nrm((DEPTH, D_MODEL, D_MODEL), D_MODEL ** -0.5)
    inp['w_up'] = nrm((DEPTH, D_MODEL, D_FF), D_MODEL ** -0.5)
    inp['w_down'] = nrm((DEPTH, D_FF, D_MODEL), D_FF ** -0.5)
    return inp


def reference(x_prompt, x_sample, mem_prompt, state_ret, state_s5_re, state_s5_im, state_ssd,
              cache_ssd_conv, cache_mem_k, cache_mem_v, norm_gains, w_in, ret_gn, w_ret_o,
              s5_a_re, s5_a_im, s5_b_re, s5_b_im, s5_c_re, s5_c_im, s5_d, s5_log_dt, w_s5_glu,
              ssd_conv_w, ssd_conv_b, ssd_dt_bias, ssd_a_log, ssd_d, ssd_norm, w_ssd_out,
              w_mix_out, w_xq, w_xkv, w_xo, w_up, w_down):
    b, l = x_prompt.shape[:2]
    dl = x_sample.shape[1]
    pos_p = jnp.arange(l, dtype=jnp.int32)
    pos_s = PAST_LEN + jnp.arange(dl, dtype=jnp.int32)
    yp, ys = x_prompt, x_sample
    p_states, s_states, p_mk, p_mv = [], [], [], []
    for i in range(DEPTH):
        lw = (norm_gains[i], w_in[i], ret_gn[i], w_ret_o[i], s5_a_re[i], s5_a_im[i], s5_b_re[i], s5_b_im[i],
              s5_c_re[i], s5_c_im[i], s5_d[i], s5_log_dt[i], w_s5_glu[i], ssd_conv_w[i], ssd_conv_b[i],
              ssd_dt_bias[i], ssd_a_log[i], ssd_d[i], ssd_norm[i], w_ssd_out[i], w_mix_out[i],
              w_xq[i], w_xo[i], w_up[i], w_down[i])
        mk, mv = memory_kv(mem_prompt, norm_gains[i, 6], w_xkv[i])
        yp, st_p = run_layer(yp, pos_p, mk, mv,
                             jnp.zeros((b, R_HEADS, R_DK, R_DV), jnp.float32),
                             jnp.zeros((b, S5_GROUPS, S5_STATE), jnp.float32),
                             jnp.zeros((b, S5_GROUPS, S5_STATE), jnp.float32),
                             jnp.zeros((b, SSD_HEADS, SSD_HEADDIM, SSD_STATE), jnp.float32),
                             jnp.zeros((b, SSD_CONV_W - 1, SSD_CONV_DIM), x_prompt.dtype),
                             *lw)
        p_states.append(st_p)
        p_mk.append(mk)
        p_mv.append(mv)
        ys, st_s = run_layer(ys, pos_s, cache_mem_k[i], cache_mem_v[i], state_ret[i], state_s5_re[i],
                             state_s5_im[i], state_ssd[i], cache_ssd_conv[i], *lw)
        s_states.append(st_s)
    p_ret, p_s5_re, p_s5_im, p_ssd, p_conv = [jnp.stack(t) for t in zip(*p_states)]
    s_ret, s_s5_re, s_s5_im, s_ssd, s_conv = [jnp.stack(t) for t in zip(*s_states)]
    p_mem_k = jnp.stack(p_mk)
    p_mem_v = jnp.stack(p_mv)
    return (yp, ys, p_ret, p_s5_re, p_s5_im, p_ssd, p_conv, p_mem_k, p_mem_v,
            s_ret, s_s5_re, s_s5_im, s_ssd, s_conv)
```

```python
import functools
import math

import jax
import jax.numpy as jnp
from jax import lax
from jax.experimental import pallas as pl
from jax.experimental.pallas import tpu as pltpu

f32 = jnp.float32
bf16 = jnp.bfloat16

D_MODEL = 2048
PAST_LEN = 1024
CHUNK = 64
EPS = 1e-6
R_HEADS = 8
R_DK = 128
R_DV = 256
R_QK = R_HEADS * R_DK
R_VAL = R_HEADS * R_DV
ROPE_BASE = 10000.0
S5_GROUP = 16
S5_GROUPS = D_MODEL // S5_GROUP
S5_STATE = 64
S5_LANES = S5_GROUPS * S5_STATE
S5_GT = 16
S5_NGT = S5_GROUPS // S5_GT
S5_TU = S5_GT * S5_GROUP
S5_TL = S5_GT * S5_STATE
SSD_INNER = 2 * D_MODEL
SSD_HEADDIM = 64
SSD_HEADS = SSD_INNER // SSD_HEADDIM
SSD_GROUPS = 8
SSD_HPG = SSD_HEADS // SSD_GROUPS
SSD_STATE = 128
SSD_CONV_W = 4
SSD_CONV_DIM = SSD_INNER + 2 * SSD_GROUPS * SSD_STATE
MEM_LEN = 256
X_HEADS = 4
X_HEAD_DIM = D_MODEL // X_HEADS
D_FF = 4 * D_MODEL
OFF_Q = 0
OFF_K = OFF_Q + R_QK
OFF_V = OFF_K + R_QK
OFF_G = OFF_V + R_VAL
OFF_U = OFF_G + R_VAL
OFF_Z = OFF_U + D_MODEL
OFF_XBC = OFF_Z + SSD_INNER
OFF_DT = OFF_XBC + SSD_CONV_DIM
OFF_GATE = OFF_DT + SSD_HEADS
MAIN_COLS = OFF_DT

VMEM_LIMIT_BYTES = 52 * 1024 * 1024


def _cparams(*sem):
    return pltpu.CompilerParams(dimension_semantics=sem, vmem_limit_bytes=VMEM_LIMIT_BYTES)


def _rms(x, g):
    return x * lax.rsqrt(jnp.mean(x * x, axis=-1, keepdims=True) + EPS) * g


def _dot(a, b):
    return jnp.dot(a, b, preferred_element_type=f32)


def _dot_nt(a, b):
    return lax.dot_general(a, b, (((1,), (1,)), ((), ())), preferred_element_type=f32)


def _dot_tn(a, b):
    return lax.dot_general(a, b, (((0,), (0,)), ((), ())), preferred_element_type=f32)


def _sigmoid(x):
    return 1.0 / (1.0 + jnp.exp(-x))


def _norm_matmul_kernel(x_ref, g_ref, w_ref, o_ref, hn_ref):
    @pl.when(pl.program_id(1) == 0)
    def _():
        hn_ref[...] = _rms(x_ref[...], g_ref[...]).astype(bf16)

    o_ref[...] = _dot(hn_ref[...], w_ref[...]).astype(o_ref.dtype)


def norm_matmul(x, g, w, *, tm, tn, out_dtype, ncols=None):
    n, k = x.shape
    m = ncols if ncols is not None else w.shape[1]
    return pl.pallas_call(
        _norm_matmul_kernel,
        grid=(n // tm, m // tn),
        in_specs=[pl.BlockSpec((tm, k), lambda i, j: (i, 0)),
                  pl.BlockSpec((1, k), lambda i, j: (0, 0)),
                  pl.BlockSpec((k, tn), lambda i, j: (0, j))],
        out_specs=pl.BlockSpec((tm, tn), lambda i, j: (i, j)),
        out_shape=jax.ShapeDtypeStruct((n, m), out_dtype),
        scratch_shapes=[pltpu.VMEM((tm, k), bf16)],
        compiler_params=_cparams("parallel", "arbitrary"),
        name="norm_matmul",
    )(x, g, w)


def _matmul_norm_res_kernel(a_ref, w_ref, g_ref, x_ref, o_ref):
    y = _dot(a_ref[...], w_ref[...])
    o_ref[...] = x_ref[...] + _rms(y, g_ref[...])


def matmul_norm_res(a, w, g, x, *, tm):
    n, k = a.shape
    d = w.shape[1]
    return pl.pallas_call(
        _matmul_norm_res_kernel,
        grid=(n // tm,),
        in_specs=[pl.BlockSpec((tm, k), lambda i: (i, 0)),
                  pl.BlockSpec((k, d), lambda i: (0, 0)),
                  pl.BlockSpec((1, d), lambda i: (0, 0)),
                  pl.BlockSpec((tm, d), lambda i: (i, 0))],
        out_specs=pl.BlockSpec((tm, d), lambda i: (i, 0)),
        out_shape=jax.ShapeDtypeStruct((n, d), f32),
        compiler_params=_cparams("parallel"),
        name="matmul_norm_res",
    )(a, w, g, x)


def _retention_kernel(q_ref, k_ref, v_ref, g_ref, cos_ref, sin_ref, intra_ref, din_ref, dup_ref, dall_ref,
                      gn_ref, s0_ref, o_ref, snew_ref, s_scr, *, nchunks):
    c = pl.program_id(1)

    @pl.when(c == 0)
    def _():
        s_scr[...] = s0_ref[0]

    cos = cos_ref[...]
    sin = sin_ref[...]
    for h in range(R_HEADS):
        q = q_ref[:, h * R_DK:(h + 1) * R_DK]
        k = k_ref[:, h * R_DK:(h + 1) * R_DK]
        v = v_ref[:, h * R_DV:(h + 1) * R_DV].astype(bf16)
        qr = (q * cos + pltpu.roll(q, R_DK // 2, 1) * sin)
        kr = (k * cos + pltpu.roll(k, R_DK // 2, 1) * sin) * (R_DK ** -0.5)
        qb = qr.astype(bf16)
        s = s_scr[h]
        scores = _dot_nt(qb, kr.astype(bf16)) * intra_ref[h]
        out = _dot(scores.astype(bf16), v) + _dot(qb, s.astype(bf16)) * din_ref[h]
        s_scr[h] = dall_ref[h] * s + _dot_tn((kr * dup_ref[h]).astype(bf16), v)
        mu = jnp.mean(out, axis=-1, keepdims=True)
        cen = out - mu
        var = jnp.mean(cen * cen, axis=-1, keepdims=True)
        o = cen * lax.rsqrt(var + EPS) * gn_ref[:, h * R_DV:(h + 1) * R_DV]
        gate = g_ref[:, h * R_DV:(h + 1) * R_DV]
        o_ref[:, h * R_DV:(h + 1) * R_DV] = (gate * _sigmoid(gate) * o).astype(bf16)

    @pl.when(c == nchunks - 1)
    def _():
        snew_ref[0] = s_scr[...]


def retention(proj, s0, gn, batch, length, pos0):
    lc = min(length, CHUNK)
    nc = length // lc
    half = R_DK // 2
    inv_freq = jnp.exp(-math.log(ROPE_BASE) * jnp.arange(half, dtype=f32) / half)
    pos = (pos0 + jnp.arange(length, dtype=jnp.int32)).astype(f32)
    ang = pos[:, None] * inv_freq[None]
    cos2 = jnp.concatenate([jnp.cos(ang), jnp.cos(ang)], axis=-1)
    sin2 = jnp.concatenate([-jnp.sin(ang), jnp.sin(ang)], axis=-1)
    log_gamma = jnp.log1p(-jnp.exp2(-5.0 - jnp.arange(R_HEADS, dtype=f32)))
    idx = jnp.arange(lc, dtype=f32)
    lg = log_gamma[:, None]
    intra = jnp.exp(jnp.abs(idx[:, None] - idx[None, :])[None] * lg[:, :, None])
    din = jnp.broadcast_to(jnp.exp((idx + 1.0)[None] * lg)[:, :, None], (R_HEADS, lc, R_DV))
    dup = jnp.broadcast_to(jnp.exp((lc - 1.0 - idx)[None] * lg)[:, :, None], (R_HEADS, lc, R_DK))
    dall = jnp.exp(lc * log_gamma)
    kern = functools.partial(_retention_kernel, nchunks=nc)
    return pl.pallas_call(
        kern,
        grid=(batch, nc),
        in_specs=[pl.BlockSpec((lc, R_QK), lambda b, c: (b * nc + c, OFF_Q // R_QK)),
                  pl.BlockSpec((lc, R_QK), lambda b, c: (b * nc + c, OFF_K // R_QK)),
                  pl.BlockSpec((lc, R_VAL), lambda b, c: (b * nc + c, OFF_V // R_VAL)),
                  pl.BlockSpec((lc, R_VAL), lambda b, c: (b * nc + c, OFF_G // R_VAL)),
                  pl.BlockSpec((lc, R_DK), lambda b, c: (c, 0)),
                  pl.BlockSpec((lc, R_DK), lambda b, c: (c, 0)),
                  pl.BlockSpec((R_HEADS, lc, lc), lambda b, c: (0, 0, 0)),
                  pl.BlockSpec((R_HEADS, lc, R_DV), lambda b, c: (0, 0, 0)),
                  pl.BlockSpec((R_HEADS, lc, R_DK), lambda b, c: (0, 0, 0)),
                  pl.BlockSpec(memory_space=pltpu.SMEM),
                  pl.BlockSpec((1, R_VAL), lambda b, c: (0, 0)),
                  pl.BlockSpec((1, R_HEADS, R_DK, R_DV), lambda b, c: (b, 0, 0, 0))],
        out_specs=[pl.BlockSpec((lc, R_VAL), lambda b, c: (b * nc + c, 0)),
                   pl.BlockSpec((1, R_HEADS, R_DK, R_DV), lambda b, c: (b, 0, 0, 0))],
        out_shape=[jax.ShapeDtypeStruct((batch * length, R_VAL), bf16),
                   jax.ShapeDtypeStruct((batch, R_HEADS, R_DK, R_DV), f32)],
        scratch_shapes=[pltpu.VMEM((R_HEADS, R_DK, R_DV), f32)],
        compiler_params=_cparams("parallel", "arbitrary"),
        name="retention",
    )(proj, proj, proj, proj, cos2, sin2, intra, din, dup, dall, gn, s0)


def _s5_prep_kernel(are_ref, aim_ref, ldt_ref, bre_ref, bim_ref, ar_ref, ai_ref, lr_ref, li_ref, bbr_ref, bbi_ref):
    a_re = are_ref[...]
    a_im = aim_ref[...]
    dt = jnp.exp(ldt_ref[...])
    lr = dt * a_re
    li = dt * a_im
    mag = jnp.exp(lr)
    ar = mag * jnp.cos(li)
    ai = mag * jnp.sin(li)
    nr = ar - 1.0
    den = a_re * a_re + a_im * a_im
    cr = (nr * a_re + ai * a_im) / den
    ci = (ai * a_re - nr * a_im) / den
    ar_ref[...] = ar
    ai_ref[...] = ai
    lr_ref[...] = lr
    li_ref[...] = li
    b_re = bre_ref[...]
    b_im = bim_ref[...]
    bbr_ref[...] = cr[:, None, :] * b_re - ci[:, None, :] * b_im
    bbi_ref[...] = cr[:, None, :] * b_im + ci[:, None, :] * b_re


def s5_prep(a_re, a_im, log_dt, b_re, b_im):
    gp = jax.ShapeDtypeStruct((S5_GROUPS, S5_STATE), f32)
    gb = jax.ShapeDtypeStruct((S5_GROUPS, S5_GROUP, S5_STATE), f32)
    return pl.pallas_call(
        _s5_prep_kernel,
        out_shape=[gp, gp, gp, gp, gb, gb],
        compiler_params=pltpu.CompilerParams(vmem_limit_bytes=VMEM_LIMIT_BYTES),
        name="s5_prep",
    )(a_re, a_im, log_dt.reshape(S5_GROUPS, 1), b_re, b_im)


_LANES = 128


def _s5_permute_in(u_refs, up_scr, nseq, steps):
    for k, u_ref in enumerate(u_refs):
        for t in range(steps):
            up_scr[t * nseq:(t + 1) * nseq, k * _LANES:(k + 1) * _LANES] = u_ref[pl.ds(t, nseq, stride=steps), :]


def _s5_scan(bur_scr, bui_scr, ar_row, ai_row, init_r, init_i, nseq, steps):
    lw = min(S5_TL, S5_LANES // nseq)
    fin_r, fin_i = [], []
    for lb in range(S5_TL // lw):
        sl = slice(lb * lw, (lb + 1) * lw)
        ar = jnp.broadcast_to(ar_row[:, sl], (nseq, lw))
        ai = jnp.broadcast_to(ai_row[:, sl], (nseq, lw))
        xr = init_r[:, sl]
        xi = init_i[:, sl]
        for t in range(steps):
            rows = slice(t * nseq, (t + 1) * nseq)
            nxr = ar * xr - ai * xi + bur_scr[rows, sl]
            nxi = ar * xi + ai * xr + bui_scr[rows, sl]
            xr, xi = nxr, nxi
            bur_scr[rows, sl] = xr
            bui_scr[rows, sl] = xi
        fin_r.append(xr)
        fin_i.append(xi)
    return fin_r, fin_i, lw


def _s5_output(up_scr, bur_scr, bui_scr, cre_ref, cim_ref, d_ref, yp_scrs, y_ref, nseq, steps):
    up = up_scr[...]
    y = _dot(bur_scr[...].astype(bf16), cre_ref[0]) - _dot(bui_scr[...].astype(bf16), cim_ref[0])
    y = jax.nn.gelu(y + d_ref[...] * up)
    for k, yp_scr in enumerate(yp_scrs):
        for t in range(steps):
            yp_scr[pl.ds(t, nseq, stride=steps), :] = y[t * nseq:(t + 1) * nseq, k * _LANES:(k + 1) * _LANES]
        y_ref[:, k * _LANES:(k + 1) * _LANES] = yp_scr[...].astype(bf16)


def _s5_prompt_kernel(u0_ref, u1_ref, d_ref, bre_ref, bim_ref, cre_ref, cim_ref, ar_ref, ai_ref, lr_ref, li_ref,
                      y_ref, hr_ref, hi_ref,
                      up_scr, bur_scr, bui_scr, yp0_scr, yp1_scr, apr_scr, api_scr, xsr_scr, xsi_scr, car_scr,
                      cai_scr, *, nseq, steps, nsc):
    b = pl.program_id(1)
    sc = pl.program_id(2)

    @pl.when((b == 0) & (sc == 0))
    def _():
        tt = (lax.broadcasted_iota(jnp.int32, (steps, S5_TL), 0) + 1).astype(f32)
        mag = jnp.exp(tt * lr_ref[...])
        apr_scr[...] = mag * jnp.cos(tt * li_ref[...])
        api_scr[...] = mag * jnp.sin(tt * li_ref[...])

    @pl.when(sc == 0)
    def _():
        car_scr[...] = jnp.zeros_like(car_scr)
        cai_scr[...] = jnp.zeros_like(cai_scr)

    _s5_permute_in((u0_ref, u1_ref), up_scr, nseq, steps)
    ub = up_scr[...].astype(bf16)
    bur_scr[...] = _dot(ub, bre_ref[0])
    bui_scr[...] = _dot(ub, bim_ref[0])
    zero = jnp.zeros((nseq, S5_TL), f32)
    _s5_scan(bur_scr, bui_scr, ar_ref[...], ai_ref[...], zero, zero, nseq, steps)
    last = slice((steps - 1) * nseq, steps * nseq)
    pr = apr_scr[steps - 1:steps, :]
    pi = api_scr[steps - 1:steps, :]
    xr = car_scr[...]
    xi = cai_scr[...]
    for r in range(nseq):
        xsr_scr[r:r + 1, :] = xr
        xsi_scr[r:r + 1, :] = xi
        er = bur_scr[(steps - 1) * nseq + r:(steps - 1) * nseq + r + 1, :]
        ei = bui_scr[(steps - 1) * nseq + r:(steps - 1) * nseq + r + 1, :]
        xr, xi = er + pr * xr - pi * xi, ei + pr * xi + pi * xr
    car_scr[...] = xr
    cai_scr[...] = xi
    xsr = xsr_scr[...]
    xsi = xsi_scr[...]
    for t in range(steps):
        rows = slice(t * nseq, (t + 1) * nseq)
        ptr = jnp.broadcast_to(apr_scr[t:t + 1, :], (nseq, S5_TL))
        pti = jnp.broadcast_to(api_scr[t:t + 1, :], (nseq, S5_TL))
        bur_scr[rows, :] = bur_scr[rows, :] + ptr * xsr - pti * xsi
        bui_scr[rows, :] = bui_scr[rows, :] + ptr * xsi + pti * xsr
    _s5_output(up_scr, bur_scr, bui_scr, cre_ref, cim_ref, d_ref, (yp0_scr, yp1_scr), y_ref, nseq, steps)

    @pl.when(sc == nsc - 1)
    def _():
        hr_ref[0] = car_scr[...]
        hi_ref[0] = cai_scr[...]


def _s5_sample_kernel(u0_ref, u1_ref, d_ref, bre_ref, bim_ref, cre_ref, cim_ref, ar_ref, ai_ref, h0r_ref, h0i_ref,
                      y_ref, hr_ref, hi_ref, up_scr, bur_scr, bui_scr, yp0_scr, yp1_scr, *, nseq, steps):
    _s5_permute_in((u0_ref, u1_ref), up_scr, nseq, steps)
    ub = up_scr[...].astype(bf16)
    bur_scr[...] = _dot(ub, bre_ref[0])
    bui_scr[...] = _dot(ub, bim_ref[0])
    fin_r, fin_i, lw = _s5_scan(bur_scr, bui_scr, ar_ref[...], ai_ref[...], h0r_ref[...], h0i_ref[...], nseq, steps)
    for lb in range(len(fin_r)):
        hr_ref[:, lb * lw:(lb + 1) * lw] = fin_r[lb]
        hi_ref[:, lb * lw:(lb + 1) * lw] = fin_i[lb]
    _s5_output(up_scr, bur_scr, bui_scr, cre_ref, cim_ref, d_ref, (yp0_scr, yp1_scr), y_ref, nseq, steps)


def _s5_blockdiag_in(bb):
    t = bb.reshape(S5_NGT, S5_GT, S5_GROUP, S5_STATE)
    eye = jnp.eye(S5_GT, dtype=bool)
    out = jnp.where(eye[None, :, None, :, None], t[:, :, :, None, :], 0.0)
    return out.reshape(S5_NGT, S5_TU, S5_TL).astype(bf16)


def _s5_blockdiag_out(c):
    t = jnp.transpose(c.reshape(S5_NGT, S5_GT, S5_GROUP, S5_STATE), (0, 1, 3, 2))
    eye = jnp.eye(S5_GT, dtype=bool)
    out = jnp.where(eye[None, :, None, :, None], t[:, :, :, None, :], 0.0)
    return out.reshape(S5_NGT, S5_TL, S5_TU).astype(bf16)


def _s5_common_specs(rows, row_map0, row_map1):
    lane = lambda *a: (0, a[0])
    mat = lambda *a: (a[0], 0, 0)
    return [pl.BlockSpec((rows, _LANES), row_map0),
            pl.BlockSpec((rows, _LANES), row_map1),
            pl.BlockSpec((1, S5_TU), lane),
            pl.BlockSpec((1, S5_TU, S5_TL), mat),
            pl.BlockSpec((1, S5_TU, S5_TL), mat),
            pl.BlockSpec((1, S5_TL, S5_TU), mat),
            pl.BlockSpec((1, S5_TL, S5_TU), mat),
            pl.BlockSpec((1, S5_TL), lane),
            pl.BlockSpec((1, S5_TL), lane)]


def s5_prompt(proj, d, mats, batch, length):
    bre, bim, cre, cim, ar, ai, lr, li = mats
    nseq, steps = 8, CHUNK
    rows = nseq * steps
    nsc = length // rows
    ucol = OFF_U // _LANES
    kern = functools.partial(_s5_prompt_kernel, nseq=nseq, steps=steps, nsc=nsc)
    lane = lambda g, b, s: (0, g)
    st = jax.ShapeDtypeStruct((batch, 1, S5_LANES), f32)
    y, hr, hi = pl.pallas_call(
        kern,
        grid=(S5_NGT, batch, nsc),
        in_specs=_s5_common_specs(rows, lambda g, b, s: (b * nsc + s, ucol + 2 * g),
                                  lambda g, b, s: (b * nsc + s, ucol + 2 * g + 1))
        + [pl.BlockSpec((1, S5_TL), lane), pl.BlockSpec((1, S5_TL), lane)],
        out_specs=[pl.BlockSpec((rows, S5_TU), lambda g, b, s: (b * nsc + s, g)),
                   pl.BlockSpec((1, 1, S5_TL), lambda g, b, s: (b, 0, g)),
                   pl.BlockSpec((1, 1, S5_TL), lambda g, b, s: (b, 0, g))],
        out_shape=[jax.ShapeDtypeStruct((batch * length, D_MODEL), bf16), st, st],
        scratch_shapes=[pltpu.VMEM((rows, S5_TU), f32),
                        pltpu.VMEM((rows, S5_TL), f32), pltpu.VMEM((rows, S5_TL), f32),
                        pltpu.VMEM((rows, _LANES), f32), pltpu.VMEM((rows, _LANES), f32),
                        pltpu.VMEM((steps, S5_TL), f32), pltpu.VMEM((steps, S5_TL), f32),
                        pltpu.VMEM((nseq, S5_TL), f32), pltpu.VMEM((nseq, S5_TL), f32),
                        pltpu.VMEM((1, S5_TL), f32), pltpu.VMEM((1, S5_TL), f32)],
        compiler_params=_cparams("arbitrary", "arbitrary", "arbitrary"),
        name="s5_prompt",
    )(proj, proj, d, bre, bim, cre, cim, ar, ai, lr, li)
    return y, hr.reshape(batch, S5_GROUPS, S5_STATE), hi.reshape(batch, S5_GROUPS, S5_STATE)


def s5_sample(proj, d, mats, h0r, h0i, batch, length):
    bre, bim, cre, cim, ar, ai, _, _ = mats
    rows = batch * length
    ucol = OFF_U // _LANES
    kern = functools.partial(_s5_sample_kernel, nseq=batch, steps=length)
    st = jax.ShapeDtypeStruct((batch, S5_LANES), f32)
    state_spec = pl.BlockSpec((batch, S5_TL), lambda g: (0, g))
    y, hr, hi = pl.pallas_call(
        kern,
        grid=(S5_NGT,),
        in_specs=_s5_common_specs(rows, lambda g: (0, ucol + 2 * g), lambda g: (0, ucol + 2 * g + 1))
        + [state_spec, state_spec],
        out_specs=[pl.BlockSpec((rows, S5_TU), lambda g: (0, g)), state_spec, state_spec],
        out_shape=[jax.ShapeDtypeStruct((rows, D_MODEL), bf16), st, st],
        scratch_shapes=[pltpu.VMEM((rows, S5_TU), f32),
                        pltpu.VMEM((rows, S5_TL), f32), pltpu.VMEM((rows, S5_TL), f32),
                        pltpu.VMEM((rows, _LANES), f32), pltpu.VMEM((rows, _LANES), f32)],
        compiler_params=_cparams("arbitrary"),
        name="s5_sample",
    )(proj, proj, d, bre, bim, cre, cim, ar, ai, h0r.reshape(batch, S5_LANES), h0i.reshape(batch, S5_LANES))
    return y, hr.reshape(batch, S5_GROUPS, S5_STATE), hi.reshape(batch, S5_GROUPS, S5_STATE)


_SSD_PAD = 8
_SSD_TP = 128


def _ssd_kernel(z_ref, xbc_ref, dt_ref, conv0_ref, cw_ref, cb_ref, dtb_ref, alog_ref, dexp_ref, ng_ref, h0_ref,
                y_ref, hnew_ref, convnew_ref, stage, h_scr, pad_scr, y_scr, *, lc, nchunks):
    c = pl.program_id(1)
    keep = SSD_CONV_W - 1
    lo = _SSD_PAD - keep

    @pl.when(c == 0)
    def _():
        stage[lo:_SSD_PAD, :] = conv0_ref[0]
        h_scr[...] = h0_ref[0]

    stage[_SSD_PAD:_SSD_PAD + lc, :] = xbc_ref[...]
    xc = cb_ref[...]
    for j in range(SSD_CONV_W):
        xc = xc + cw_ref[j:j + 1, :] * stage[lo + j:lo + j + lc, :]
    xc = xc * _sigmoid(xc)
    tail = stage[lo + lc:_SSD_PAD + lc, :]

    @pl.when(c == nchunks - 1)
    def _():
        convnew_ref[0] = tail

    stage[lo:_SSD_PAD, :] = tail

    dtx = dt_ref[...] + dtb_ref[...]
    dtv = jnp.maximum(dtx, 0.0) + jnp.log1p(jnp.exp(-jnp.abs(dtx)))
    da = dtv * (-jnp.exp(alog_ref[...]))
    ri = lax.broadcasted_iota(jnp.int32, (lc, lc), 0)
    ci = lax.broadcasted_iota(jnp.int32, (lc, lc), 1)
    causal = ri >= ci
    tri = jnp.where(causal, 1.0, 0.0).astype(bf16)
    p0 = da.astype(bf16)
    r0 = da - p0.astype(f32)
    p1 = r0.astype(bf16)
    p2 = (r0 - p1.astype(f32)).astype(bf16)
    cs = _dot(tri, p0) + _dot(tri, p1) + _dot(tri, p2)
    pad_scr[...] = jnp.zeros_like(pad_scr)
    pad_scr[0:lc, 0:SSD_HEADS] = cs
    cst = pad_scr[...].T
    ecs = jnp.exp(cs)
    cs_last = cs[lc - 1:lc, :]
    wdec = jnp.exp(cs_last - cs)
    elast = jnp.exp(cs_last)
    xs_off, b_off, c_off = 0, SSD_INNER, SSD_INNER + SSD_GROUPS * SSD_STATE
    for g in range(SSD_GROUPS):
        bg = xc[:, b_off + g * SSD_STATE:b_off + (g + 1) * SSD_STATE].astype(bf16)
        cg = xc[:, c_off + g * SSD_STATE:c_off + (g + 1) * SSD_STATE].astype(bf16)
        cb = _dot_nt(cg, bg)
        for r in range(SSD_HPG):
            h = g * SSD_HPG + r
            hs = slice(h * SSD_HEADDIM, (h + 1) * SSD_HEADDIM)
            col = cs[:, h:h + 1]
            row = cst[h:h + 1, 0:lc]
            decay = jnp.exp(jnp.where(causal, col - row, -jnp.inf))
            xh = xc[:, hs]
            xdt = xh * dtv[:, h:h + 1]
            hst = h_scr[h]
            y = _dot((cb * decay).astype(bf16), xdt.astype(bf16))
            y = y + _dot_nt(cg, hst.astype(bf16)) * ecs[:, h:h + 1]
            xw = (xdt * wdec[:, h:h + 1]).astype(bf16)
            h_scr[h] = elast[:, h:h + 1] * hst + _dot_tn(xw, bg)
            y_scr[:, hs] = y + dexp_ref[:, hs] * xh
        gs = slice(g * SSD_HPG * SSD_HEADDIM, (g + 1) * SSD_HPG * SSD_HEADDIM)
        zg = z_ref[:, gs]
        yg = y_scr[:, gs] * (zg * _sigmoid(zg))
        yg = yg * lax.rsqrt(jnp.mean(yg * yg, axis=-1, keepdims=True) + EPS) * ng_ref[:, gs]
        y_ref[:, gs] = yg.astype(bf16)

    @pl.when(c == nchunks - 1)
    def _():
        hnew_ref[0] = h_scr[...]


def ssd(proj, dt_raw, h0, conv0, conv_w, conv_b, dt_bias, a_log, d, norm_g, batch, length):
    lc = min(length, CHUNK)
    nc = length // lc
    dexp = jnp.repeat(d, SSD_HEADDIM).reshape(1, SSD_INNER)
    kern = functools.partial(_ssd_kernel, lc=lc, nchunks=nc)
    row = lambda b, c: (b * nc + c, 0)
    const2 = lambda b, c: (0, 0)
    return pl.pallas_call(
        kern,
        grid=(batch, nc),
        in_specs=[pl.BlockSpec((lc, SSD_INNER), lambda b, c: (b * nc + c, OFF_Z // SSD_INNER)),
                  pl.BlockSpec((lc, SSD_CONV_DIM), lambda b, c: (b * nc + c, OFF_XBC // SSD_CONV_DIM)),
                  pl.BlockSpec((lc, SSD_HEADS), row),
                  pl.BlockSpec((1, SSD_CONV_W - 1, SSD_CONV_DIM), lambda b, c: (b, 0, 0)),
                  pl.BlockSpec((SSD_CONV_W, SSD_CONV_DIM), const2),
                  pl.BlockSpec((1, SSD_CONV_DIM), const2),
                  pl.BlockSpec((1, SSD_HEADS), const2),
                  pl.BlockSpec((1, SSD_HEADS), const2),
                  pl.BlockSpec((1, SSD_INNER), const2),
                  pl.BlockSpec((1, SSD_INNER), const2),
                  pl.BlockSpec((1, SSD_HEADS, SSD_HEADDIM, SSD_STATE), lambda b, c: (b, 0, 0, 0))],
        out_specs=[pl.BlockSpec((lc, SSD_INNER), row),
                   pl.BlockSpec((1, SSD_HEADS, SSD_HEADDIM, SSD_STATE), lambda b, c: (b, 0, 0, 0)),
                   pl.BlockSpec((1, SSD_CONV_W - 1, SSD_CONV_DIM), lambda b, c: (b, 0, 0))],
        out_shape=[jax.ShapeDtypeStruct((batch * length, SSD_INNER), bf16),
                   jax.ShapeDtypeStruct((batch, SSD_HEADS, SSD_HEADDIM, SSD_STATE), f32),
                   jax.ShapeDtypeStruct((batch, SSD_CONV_W - 1, SSD_CONV_DIM), f32)],
        scratch_shapes=[pltpu.VMEM((_SSD_PAD + lc, SSD_CONV_DIM), f32),
                        pltpu.VMEM((SSD_HEADS, SSD_HEADDIM, SSD_STATE), f32),
                        pltpu.VMEM((_SSD_TP, _SSD_TP), f32),
                        pltpu.VMEM((lc, SSD_INNER), f32)],
        compiler_params=_cparams("parallel", "arbitrary"),
        name="ssd",
    )(proj, proj, dt_raw, conv0, conv_w, conv_b.reshape(1, -1), dt_bias.reshape(1, -1), a_log.reshape(1, -1),
      dexp, norm_g.reshape(1, -1), h0)


def _merge_kernel(ra_ref, sa_ref, da_ref, gr_ref, gs_ref, gd_ref, wr_ref, wv_ref, wg_ref, wd_ref, o_ref):
    sa = sa_ref[...]
    ret = _dot(ra_ref[...], wr_ref[...])
    s5 = _dot(sa, wv_ref[...]) * _sigmoid(_dot(sa, wg_ref[...]))
    sd = _dot(da_ref[...], wd_ref[...])
    merged = _sigmoid(gr_ref[...]) * ret + _sigmoid(gs_ref[...]) * s5 + _sigmoid(gd_ref[...]) * sd
    o_ref[...] = merged.astype(bf16)


def merge(ret_act, s5_act, ssd_act, gates, w_ret_o, w_glu, w_ssd_out, *, tm, tn):
    n = ret_act.shape[0]
    nj = D_MODEL // tn
    act = lambda i, j: (i, 0)
    return pl.pallas_call(
        _merge_kernel,
        grid=(n // tm, nj),
        in_specs=[pl.BlockSpec((tm, R_VAL), act),
                  pl.BlockSpec((tm, D_MODEL), act),
                  pl.BlockSpec((tm, SSD_INNER), act),
                  pl.BlockSpec((tm, tn), lambda i, j: (i, j)),
                  pl.BlockSpec((tm, tn), lambda i, j: (i, nj + j)),
                  pl.BlockSpec((tm, tn), lambda i, j: (i, 2 * nj + j)),
                  pl.BlockSpec((R_VAL, tn), lambda i, j: (0, j)),
                  pl.BlockSpec((D_MODEL, tn), lambda i, j: (0, j)),
                  pl.BlockSpec((D_MODEL, tn), lambda i, j: (0, nj + j)),
                  pl.BlockSpec((SSD_INNER, tn), lambda i, j: (0, j))],
        out_specs=pl.BlockSpec((tm, tn), lambda i, j: (i, j)),
        out_shape=jax.ShapeDtypeStruct((n, D_MODEL), bf16),
        compiler_params=_cparams("parallel", "arbitrary"),
        name="merge",
    )(ret_act, s5_act, ssd_act, gates, gates, gates, w_ret_o, w_glu, w_glu, w_ssd_out)


def _attn_kernel(q_ref, k_ref, v_ref, o_ref):
    scale = X_HEAD_DIM ** -0.5
    for h in range(X_HEADS):
        hs = slice(h * X_HEAD_DIM, (h + 1) * X_HEAD_DIM)
        s = _dot_nt(q_ref[:, hs], k_ref[0, :, hs].astype(bf16)) * scale
        e = jnp.exp(s - jnp.max(s, axis=-1, keepdims=True))
        p = e / jnp.sum(e, axis=-1, keepdims=True)
        o_ref[:, hs] = _dot(p.astype(bf16), v_ref[0, :, hs].astype(bf16)).astype(bf16)


def attention(q, mem_k, mem_v, batch, length, *, tq):
    nt = length // tq
    kv = pl.BlockSpec((1, MEM_LEN, D_MODEL), lambda b, i: (b, 0, 0))
    return pl.pallas_call(
        _attn_kernel,
        grid=(batch, nt),
        in_specs=[pl.BlockSpec((tq, D_MODEL), lambda b, i: (b * nt + i, 0)), kv, kv],
        out_specs=pl.BlockSpec((tq, D_MODEL), lambda b, i: (b * nt + i, 0)),
        out_shape=jax.ShapeDtypeStruct((batch * length, D_MODEL), bf16),
        compiler_params=_cparams("parallel", "arbitrary"),
        name="attention",
    )(q, mem_k, mem_v)


def _mlp_kernel(x_ref, g_in_ref, g_out_ref, wu_ref, wd_ref, o_ref, hn_scr, acc_scr, *, nff):
    j = pl.program_id(1)

    @pl.when(j == 0)
    def _():
        hn_scr[...] = _rms(x_ref[...], g_in_ref[...]).astype(bf16)
        acc_scr[...] = jnp.zeros_like(acc_scr)

    h = jnp.maximum(_dot(hn_scr[...], wu_ref[...]), 0.0)
    acc_scr[...] += _dot((h * h).astype(bf16), wd_ref[...])

    @pl.when(j == nff - 1)
    def _():
        o_ref[...] = x_ref[...] + _rms(acc_scr[...], g_out_ref[...])


def mlp(x, g_in, g_out, w_up, w_down, *, tm, tf):
    n, d = x.shape
    nff = D_FF // tf
    kern = functools.partial(_mlp_kernel, nff=nff)
    return pl.pallas_call(
        kern,
        grid=(n // tm, nff),
        in_specs=[pl.BlockSpec((tm, d), lambda i, j: (i, 0)),
                  pl.BlockSpec((1, d), lambda i, j: (0, 0)),
                  pl.BlockSpec((1, d), lambda i, j: (0, 0)),
                  pl.BlockSpec((d, tf), lambda i, j: (0, j)),
                  pl.BlockSpec((tf, d), lambda i, j: (j, 0))],
        out_specs=pl.BlockSpec((tm, d), lambda i, j: (i, 0)),
        out_shape=jax.ShapeDtypeStruct((n, d), f32),
        scratch_shapes=[pltpu.VMEM((tm, d), bf16), pltpu.VMEM((tm, d), f32)],
        compiler_params=_cparams("parallel", "arbitrary"),
        name="mlp",
    )(x, g_in, g_out, w_up, w_down)


def _run_layer(x, batch, length, pos0, mem_k, mem_v, ret_s, s5_r, s5_i, ssd_s, conv_s, lw, *, prompt):
    gains = lw["gains"]
    tm = min(512, x.shape[0])
    proj = norm_matmul(x, gains[0:1], lw["w_main"], tm=tm, tn=512, out_dtype=f32)
    dt_raw = norm_matmul(x, gains[0:1], lw["w_dt"], tm=tm, tn=SSD_HEADS, out_dtype=f32)
    gates = norm_matmul(x, gains[0:1], lw["w_gate"], tm=tm, tn=512, out_dtype=f32)
    ret_act, ret_new = retention(proj, ret_s, lw["ret_gn"], batch, length, pos0)
    if prompt:
        s5_act, s5_r_new, s5_i_new = s5_prompt(proj, lw["s5_d"], lw["s5_mats"], batch, length)
    else:
        s5_act, s5_r_new, s5_i_new = s5_sample(proj, lw["s5_d"], lw["s5_mats"], s5_r, s5_i, batch, length)
    ssd_act, ssd_new, conv_new = ssd(proj, dt_raw, ssd_s, conv_s, lw["ssd_conv_w"], lw["ssd_conv_b"],
                                     lw["ssd_dt_bias"], lw["ssd_a_log"], lw["ssd_d"], lw["ssd_norm"], batch, length)
    merged = merge(ret_act, s5_act, ssd_act, gates, lw["w_ret_o"], lw["w_s5_glu"], lw["w_ssd_out"], tm=tm, tn=256)
    x = matmul_norm_res(merged, lw["w_mix_out"], gains[1:2], x, tm=tm)
    q = norm_matmul(x, gains[2:3], lw["w_xq"], tm=tm, tn=512, out_dtype=bf16)
    att = attention(q, mem_k, mem_v, batch, length, tq=min(length, 512))
    x = matmul_norm_res(att, lw["w_xo"], gains[3:4], x, tm=tm)
    x = mlp(x, gains[4:5], gains[5:6], lw["w_up"], lw["w_down"], tm=tm, tf=1024)
    return x, (ret_new, s5_r_new, s5_i_new, ssd_new, conv_new)


def kernel(x_prompt, x_sample, mem_prompt, state_ret, state_s5_re, state_s5_im, state_ssd, cache_ssd_conv,
           cache_mem_k, cache_mem_v, norm_gains, w_in, ret_gn, w_ret_o, s5_a_re, s5_a_im, s5_b_re, s5_b_im,
           s5_c_re, s5_c_im, s5_d, s5_log_dt, w_s5_glu, ssd_conv_w, ssd_conv_b, ssd_dt_bias, ssd_a_log, ssd_d,
           ssd_norm, w_ssd_out, w_mix_out, w_xq, w_xkv, w_xo, w_up, w_down):
    b, l, _ = x_prompt.shape
    db, dl, _ = x_sample.shape
    depth = w_in.shape[0]
    yp = x_prompt.reshape(b * l, D_MODEL)
    ys = x_sample.reshape(db * dl, D_MODEL)
    mem2d = mem_prompt.reshape(b * MEM_LEN, D_MODEL)
    p_states, s_states, p_mk, p_mv = [], [], [], []
    for i in range(depth):
        ar, ai, lr, li, bbr, bbi = s5_prep(s5_a_re[i], s5_a_im[i], s5_log_dt[i],
                                           jnp.swapaxes(s5_b_re[i], 1, 2), jnp.swapaxes(s5_b_im[i], 1, 2))
        lane = lambda t: t.reshape(1, S5_LANES)
        lw = dict(
            gains=norm_gains[i],
            w_main=w_in[i, :, :MAIN_COLS].astype(bf16),
            w_dt=w_in[i, :, OFF_DT:OFF_GATE].astype(bf16),
            w_gate=w_in[i, :, OFF_GATE:].astype(bf16),
            ret_gn=ret_gn[i].reshape(1, R_VAL),
            w_ret_o=w_ret_o[i].astype(bf16),
            s5_d=s5_d[i].reshape(1, D_MODEL),
            s5_mats=(_s5_blockdiag_in(bbr), _s5_blockdiag_in(bbi),
                     _s5_blockdiag_out(s5_c_re[i]), _s5_blockdiag_out(s5_c_im[i]),
                     lane(ar), lane(ai), lane(lr), lane(li)),
            w_s5_glu=w_s5_glu[i].astype(bf16),
            ssd_conv_w=ssd_conv_w[i], ssd_conv_b=ssd_conv_b[i], ssd_dt_bias=ssd_dt_bias[i],
            ssd_a_log=ssd_a_log[i], ssd_d=ssd_d[i], ssd_norm=ssd_norm[i],
            w_ssd_out=w_ssd_out[i].astype(bf16),
            w_mix_out=w_mix_out[i].astype(bf16),
            w_xq=w_xq[i].astype(bf16),
            w_xo=w_xo[i].astype(bf16),
            w_up=w_up[i].astype(bf16),
            w_down=w_down[i].astype(bf16),
        )
        kv = norm_matmul(mem2d, norm_gains[i, 6:7], w_xkv[i].astype(bf16), tm=b * MEM_LEN, tn=512, out_dtype=f32)
        mk = kv[:, :D_MODEL].reshape(b, MEM_LEN, D_MODEL)
        mv = kv[:, D_MODEL:].reshape(b, MEM_LEN, D_MODEL)
        yp, st_p = _run_layer(
            yp, b, l, 0, mk, mv,
            jnp.zeros((b, R_HEADS, R_DK, R_DV), f32), None, None,
            jnp.zeros((b, SSD_HEADS, SSD_HEADDIM, SSD_STATE), f32),
            jnp.zeros((b, SSD_CONV_W - 1, SSD_CONV_DIM), f32), lw, prompt=True)
        p_states.append(st_p)
        p_mk.append(mk.reshape(b, MEM_LEN, X_HEADS, X_HEAD_DIM))
        p_mv.append(mv.reshape(b, MEM_LEN, X_HEADS, X_HEAD_DIM))
        ys, st_s = _run_layer(
            ys, db, dl, PAST_LEN,
            cache_mem_k[i].reshape(db, MEM_LEN, D_MODEL), cache_mem_v[i].reshape(db, MEM_LEN, D_MODEL),
            state_ret[i], state_s5_re[i], state_s5_im[i], state_ssd[i], cache_ssd_conv[i], lw, prompt=False)
        s_states.append(st_s)
    p_ret, p_s5_re, p_s5_im, p_ssd, p_conv = [jnp.stack(t) for t in zip(*p_states)]
    s_ret, s_s5_re, s_s5_im, s_ssd, s_conv = [jnp.stack(t) for t in zip(*s_states)]
    return (yp.reshape(b, l, D_MODEL), ys.reshape(db, dl, D_MODEL), p_ret, p_s5_re, p_s5_im, p_ssd, p_conv,
            jnp.stack(p_mk), jnp.stack(p_mv), s_ret, s_s5_re, s_s5_im, s_ssd, s_conv)
```

```python
import functools
import math

import jax
import jax.numpy as jnp
from jax import lax
from jax.experimental import pallas as pl
from jax.experimental.pallas import tpu as pltpu

f32 = jnp.float32
bf16 = jnp.bfloat16

D_MODEL = 2048
PAST_LEN = 1024
CHUNK = 64
EPS = 1e-6
R_HEADS = 8
R_DK = 128
R_DV = 256
R_QK = R_HEADS * R_DK
R_VAL = R_HEADS * R_DV
ROPE_BASE = 10000.0
S5_GROUP = 16
S5_GROUPS = D_MODEL // S5_GROUP
S5_STATE = 64
S5_LANES = S5_GROUPS * S5_STATE
S5_GT = 16
S5_NGT = S5_GROUPS // S5_GT
S5_TU = S5_GT * S5_GROUP
S5_TL = S5_GT * S5_STATE
SSD_INNER = 2 * D_MODEL
SSD_HEADDIM = 64
SSD_HEADS = SSD_INNER // SSD_HEADDIM
SSD_GROUPS = 8
SSD_HPG = SSD_HEADS // SSD_GROUPS
SSD_STATE = 128
SSD_CONV_W = 4
SSD_CONV_DIM = SSD_INNER + 2 * SSD_GROUPS * SSD_STATE
MEM_LEN = 256
X_HEADS = 4
X_HEAD_DIM = D_MODEL // X_HEADS
D_FF = 4 * D_MODEL
OFF_Q = 0
OFF_K = OFF_Q + R_QK
OFF_V = OFF_K + R_QK
OFF_G = OFF_V + R_VAL
OFF_U = OFF_G + R_VAL
OFF_Z = OFF_U + D_MODEL
OFF_XBC = OFF_Z + SSD_INNER
OFF_DT = OFF_XBC + SSD_CONV_DIM
OFF_GATE = OFF_DT + SSD_HEADS
MAIN_COLS = OFF_DT

VMEM_LIMIT_BYTES = 52 * 1024 * 1024


def _cparams(*sem):
    return pltpu.CompilerParams(dimension_semantics=sem, vmem_limit_bytes=VMEM_LIMIT_BYTES)


def _rms(x, g):
    return x * lax.rsqrt(jnp.mean(x * x, axis=-1, keepdims=True) + EPS) * g


def _dot(a, b):
    return jnp.dot(a, b, preferred_element_type=f32)


def _dot_nt(a, b):
    return lax.dot_general(a, b, (((1,), (1,)), ((), ())), preferred_element_type=f32)


def _dot_tn(a, b):
    return lax.dot_general(a, b, (((0,), (0,)), ((), ())), preferred_element_type=f32)


def _sigmoid(x):
    return 1.0 / (1.0 + jnp.exp(-x))


def _norm_matmul_kernel(x_ref, g_ref, w_ref, o_ref, hn_ref):
    @pl.when(pl.program_id(1) == 0)
    def _():
        hn_ref[...] = _rms(x_ref[...], g_ref[...]).astype(bf16)

    o_ref[...] = _dot(hn_ref[...], w_ref[...]).astype(o_ref.dtype)


def norm_matmul(x, g, w, *, tm, tn, out_dtype, ncols=None):
    n, k = x.shape
    m = ncols if ncols is not None else w.shape[1]
    return pl.pallas_call(
        _norm_matmul_kernel,
        grid=(n // tm, m // tn),
        in_specs=[pl.BlockSpec((tm, k), lambda i, j: (i, 0)),
                  pl.BlockSpec((1, k), lambda i, j: (0, 0)),
                  pl.BlockSpec((k, tn), lambda i, j: (0, j))],
        out_specs=pl.BlockSpec((tm, tn), lambda i, j: (i, j)),
        out_shape=jax.ShapeDtypeStruct((n, m), out_dtype),
        scratch_shapes=[pltpu.VMEM((tm, k), bf16)],
        compiler_params=_cparams("parallel", "arbitrary"),
        name="norm_matmul",
    )(x, g, w)


def _matmul_norm_res_kernel(a_ref, w_ref, g_ref, x_ref, o_ref):
    y = _dot(a_ref[...], w_ref[...])
    o_ref[...] = x_ref[...] + _rms(y, g_ref[...])


def matmul_norm_res(a, w, g, x, *, tm):
    n, k = a.shape
    d = w.shape[1]
    return pl.pallas_call(
        _matmul_norm_res_kernel,
        grid=(n // tm,),
        in_specs=[pl.BlockSpec((tm, k), lambda i: (i, 0)),
                  pl.BlockSpec((k, d), lambda i: (0, 0)),
                  pl.BlockSpec((1, d), lambda i: (0, 0)),
                  pl.BlockSpec((tm, d), lambda i: (i, 0))],
        out_specs=pl.BlockSpec((tm, d), lambda i: (i, 0)),
        out_shape=jax.ShapeDtypeStruct((n, d), f32),
        compiler_params=_cparams("parallel"),
        name="matmul_norm_res",
    )(a, w, g, x)


def _layered_state_specs(layer_rows, tail, state_all, prev_out):
    zeros = (0,) * len(tail)
    spec = pl.BlockSpec((1,) + tail, lambda b, c: (layer_rows + b,) + zeros)
    ins, specs = [], []
    if state_all is not None:
        ins.append(state_all)
        specs.append(spec)
    if prev_out is not None:
        ins.append(prev_out)
        specs.append(pl.BlockSpec(memory_space=pl.ANY))
    return ins, specs, spec


def _retention_kernel(*refs, nchunks, has_init, n_alias):
    q_ref, k_ref, v_ref, g_ref, cos_ref, sin_ref, intra_ref, din_ref, dup_ref, dall_ref, gn_ref = refs[:11]
    s0_ref = refs[11] if has_init else None
    o_ref, snew_ref, s_scr = refs[11 + has_init + n_alias:]
    c = pl.program_id(1)

    @pl.when(c == 0)
    def _():
        if has_init:
            s_scr[...] = s0_ref[0]
        else:
            s_scr[...] = jnp.zeros_like(s_scr)

    cos = cos_ref[...]
    sin = sin_ref[...]
    for h in range(R_HEADS):
        q = q_ref[:, h * R_DK:(h + 1) * R_DK]
        k = k_ref[:, h * R_DK:(h + 1) * R_DK]
        v = v_ref[:, h * R_DV:(h + 1) * R_DV].astype(bf16)
        qr = (q * cos + pltpu.roll(q, R_DK // 2, 1) * sin)
        kr = (k * cos + pltpu.roll(k, R_DK // 2, 1) * sin) * (R_DK ** -0.5)
        qb = qr.astype(bf16)
        s = s_scr[h]
        scores = _dot_nt(qb, kr.astype(bf16)) * intra_ref[h]
        out = _dot(scores.astype(bf16), v) + _dot(qb, s.astype(bf16)) * din_ref[h]
        s_scr[h] = dall_ref[h] * s + _dot_tn((kr * dup_ref[h]).astype(bf16), v)
        mu = jnp.mean(out, axis=-1, keepdims=True)
        cen = out - mu
        var = jnp.mean(cen * cen, axis=-1, keepdims=True)
        o = cen * lax.rsqrt(var + EPS) * gn_ref[:, h * R_DV:(h + 1) * R_DV]
        gate = g_ref[:, h * R_DV:(h + 1) * R_DV]
        o_ref[:, h * R_DV:(h + 1) * R_DV] = (gate * _sigmoid(gate) * o).astype(bf16)

    @pl.when(c == nchunks - 1)
    def _():
        snew_ref[0] = s_scr[...]


def retention(proj, gn, batch, length, pos0, *, state_all, prev_out, layer, depth):
    lc = min(length, CHUNK)
    nc = length // lc
    half = R_DK // 2
    inv_freq = jnp.exp(-math.log(ROPE_BASE) * jnp.arange(half, dtype=f32) / half)
    pos = (pos0 + jnp.arange(length, dtype=jnp.int32)).astype(f32)
    ang = pos[:, None] * inv_freq[None]
    cos2 = jnp.concatenate([jnp.cos(ang), jnp.cos(ang)], axis=-1)
    sin2 = jnp.concatenate([-jnp.sin(ang), jnp.sin(ang)], axis=-1)
    log_gamma = jnp.log1p(-jnp.exp2(-5.0 - jnp.arange(R_HEADS, dtype=f32)))
    idx = jnp.arange(lc, dtype=f32)
    lg = log_gamma[:, None]
    intra = jnp.exp(jnp.abs(idx[:, None] - idx[None, :])[None] * lg[:, :, None])
    din = jnp.broadcast_to(jnp.exp((idx + 1.0)[None] * lg)[:, :, None], (R_HEADS, lc, R_DV))
    dup = jnp.broadcast_to(jnp.exp((lc - 1.0 - idx)[None] * lg)[:, :, None], (R_HEADS, lc, R_DK))
    dall = jnp.exp(lc * log_gamma)
    tail = (R_HEADS, R_DK, R_DV)
    st_in, st_specs, st_out_spec = _layered_state_specs(layer * batch, tail, state_all, prev_out)
    n_fixed = 11
    kern = functools.partial(_retention_kernel, nchunks=nc, has_init=state_all is not None,
                             n_alias=int(prev_out is not None))
    aliases = {n_fixed + len(st_in) - 1: 1} if prev_out is not None else {}
    return pl.pallas_call(
        kern,
        grid=(batch, nc),
        in_specs=[pl.BlockSpec((lc, R_QK), lambda b, c: (b * nc + c, OFF_Q // R_QK)),
                  pl.BlockSpec((lc, R_QK), lambda b, c: (b * nc + c, OFF_K // R_QK)),
                  pl.BlockSpec((lc, R_VAL), lambda b, c: (b * nc + c, OFF_V // R_VAL)),
                  pl.BlockSpec((lc, R_VAL), lambda b, c: (b * nc + c, OFF_G // R_VAL)),
                  pl.BlockSpec((lc, R_DK), lambda b, c: (c, 0)),
                  pl.BlockSpec((lc, R_DK), lambda b, c: (c, 0)),
                  pl.BlockSpec((R_HEADS, lc, lc), lambda b, c: (0, 0, 0)),
                  pl.BlockSpec((R_HEADS, lc, R_DV), lambda b, c: (0, 0, 0)),
                  pl.BlockSpec((R_HEADS, lc, R_DK), lambda b, c: (0, 0, 0)),
                  pl.BlockSpec(memory_space=pltpu.SMEM),
                  pl.BlockSpec((1, R_VAL), lambda b, c: (0, 0))] + st_specs,
        out_specs=[pl.BlockSpec((lc, R_VAL), lambda b, c: (b * nc + c, 0)), st_out_spec],
        out_shape=[jax.ShapeDtypeStruct((batch * length, R_VAL), bf16),
                   jax.ShapeDtypeStruct((depth * batch,) + tail, f32)],
        scratch_shapes=[pltpu.VMEM(tail, f32)],
        input_output_aliases=aliases,
        compiler_params=_cparams("parallel", "arbitrary"),
        name="retention",
    )(proj, proj, proj, proj, cos2, sin2, intra, din, dup, dall, gn, *st_in)


def _s5_prep_kernel(are_ref, aim_ref, ldt_ref, bre_ref, bim_ref, ar_ref, ai_ref, lr_ref, li_ref, bbr_ref, bbi_ref):
    a_re = are_ref[...]
    a_im = aim_ref[...]
    dt = jnp.exp(ldt_ref[...])
    lr = dt * a_re
    li = dt * a_im
    mag = jnp.exp(lr)
    ar = mag * jnp.cos(li)
    ai = mag * jnp.sin(li)
    nr = ar - 1.0
    den = a_re * a_re + a_im * a_im
    cr = (nr * a_re + ai * a_im) / den
    ci = (ai * a_re - nr * a_im) / den
    ar_ref[...] = ar
    ai_ref[...] = ai
    lr_ref[...] = lr
    li_ref[...] = li
    b_re = bre_ref[...]
    b_im = bim_ref[...]
    bbr_ref[...] = cr[:, None, :] * b_re - ci[:, None, :] * b_im
    bbi_ref[...] = cr[:, None, :] * b_im + ci[:, None, :] * b_re


def s5_prep(a_re, a_im, log_dt, b_re, b_im):
    gp = jax.ShapeDtypeStruct((S5_GROUPS, S5_STATE), f32)
    gb = jax.ShapeDtypeStruct((S5_GROUPS, S5_GROUP, S5_STATE), f32)
    return pl.pallas_call(
        _s5_prep_kernel,
        out_shape=[gp, gp, gp, gp, gb, gb],
        compiler_params=pltpu.CompilerParams(vmem_limit_bytes=VMEM_LIMIT_BYTES),
        name="s5_prep",
    )(a_re, a_im, log_dt.reshape(S5_GROUPS, 1), b_re, b_im)


_LANES = 128


def _s5_permute_in(u_refs, up_scr, nseq, steps):
    for k, u_ref in enumerate(u_refs):
        for t in range(steps):
            up_scr[t * nseq:(t + 1) * nseq, k * _LANES:(k + 1) * _LANES] = u_ref[pl.ds(t, nseq, stride=steps), :]


def _s5_scan(bur_scr, bui_scr, ar_row, ai_row, init_r, init_i, nseq, steps):
    lw = min(S5_TL, S5_LANES // nseq)
    fin_r, fin_i = [], []
    for lb in range(S5_TL // lw):
        sl = slice(lb * lw, (lb + 1) * lw)
        ar = jnp.broadcast_to(ar_row[:, sl], (nseq, lw))
        ai = jnp.broadcast_to(ai_row[:, sl], (nseq, lw))
        xr = init_r[:, sl]
        xi = init_i[:, sl]
        for t in range(steps):
            rows = slice(t * nseq, (t + 1) * nseq)
            nxr = ar * xr - ai * xi + bur_scr[rows, sl]
            nxi = ar * xi + ai * xr + bui_scr[rows, sl]
            xr, xi = nxr, nxi
            bur_scr[rows, sl] = xr
            bui_scr[rows, sl] = xi
        fin_r.append(xr)
        fin_i.append(xi)
    return fin_r, fin_i, lw


def _s5_output(up_scr, bur_scr, bui_scr, cre_ref, cim_ref, d_ref, yp_scrs, y_ref, nseq, steps):
    up = up_scr[...]
    y = _dot(bur_scr[...].astype(bf16), cre_ref[0]) - _dot(bui_scr[...].astype(bf16), cim_ref[0])
    y = jax.nn.gelu(y + d_ref[...] * up)
    for k, yp_scr in enumerate(yp_scrs):
        for t in range(steps):
            yp_scr[pl.ds(t, nseq, stride=steps), :] = y[t * nseq:(t + 1) * nseq, k * _LANES:(k + 1) * _LANES]
        y_ref[:, k * _LANES:(k + 1) * _LANES] = yp_scr[...].astype(bf16)


def _s5_prompt_kernel(u0_ref, u1_ref, d_ref, bre_ref, bim_ref, cre_ref, cim_ref, ar_ref, ai_ref, lr_ref, li_ref,
                      y_ref, hr_ref, hi_ref,
                      up_scr, bur_scr, bui_scr, yp0_scr, yp1_scr, apr_scr, api_scr, xsr_scr, xsi_scr, car_scr,
                      cai_scr, *, nseq, steps, nsc):
    b = pl.program_id(1)
    sc = pl.program_id(2)

    @pl.when((b == 0) & (sc == 0))
    def _():
        tt = (lax.broadcasted_iota(jnp.int32, (steps, S5_TL), 0) + 1).astype(f32)
        mag = jnp.exp(tt * lr_ref[...])
        apr_scr[...] = mag * jnp.cos(tt * li_ref[...])
        api_scr[...] = mag * jnp.sin(tt * li_ref[...])

    @pl.when(sc == 0)
    def _():
        car_scr[...] = jnp.zeros_like(car_scr)
        cai_scr[...] = jnp.zeros_like(cai_scr)

    _s5_permute_in((u0_ref, u1_ref), up_scr, nseq, steps)
    ub = up_scr[...].astype(bf16)
    bur_scr[...] = _dot(ub, bre_ref[0])
    bui_scr[...] = _dot(ub, bim_ref[0])
    zero = jnp.zeros((nseq, S5_TL), f32)
    _s5_scan(bur_scr, bui_scr, ar_ref[...], ai_ref[...], zero, zero, nseq, steps)
    last = slice((steps - 1) * nseq, steps * nseq)
    pr = apr_scr[steps - 1:steps, :]
    pi = api_scr[steps - 1:steps, :]
    xr = car_scr[...]
    xi = cai_scr[...]
    for r in range(nseq):
        xsr_scr[r:r + 1, :] = xr
        xsi_scr[r:r + 1, :] = xi
        er = bur_scr[(steps - 1) * nseq + r:(steps - 1) * nseq + r + 1, :]
        ei = bui_scr[(steps - 1) * nseq + r:(steps - 1) * nseq + r + 1, :]
        xr, xi = er + pr * xr - pi * xi, ei + pr * xi + pi * xr
    car_scr[...] = xr
    cai_scr[...] = xi
    xsr = xsr_scr[...]
    xsi = xsi_scr[...]
    for t in range(steps):
        rows = slice(t * nseq, (t + 1) * nseq)
        ptr = jnp.broadcast_to(apr_scr[t:t + 1, :], (nseq, S5_TL))
        pti = jnp.broadcast_to(api_scr[t:t + 1, :], (nseq, S5_TL))
        bur_scr[rows, :] = bur_scr[rows, :] + ptr * xsr - pti * xsi
        bui_scr[rows, :] = bui_scr[rows, :] + ptr * xsi + pti * xsr
    _s5_output(up_scr, bur_scr, bui_scr, cre_ref, cim_ref, d_ref, (yp0_scr, yp1_scr), y_ref, nseq, steps)

    @pl.when(sc == nsc - 1)
    def _():
        hr_ref[0] = car_scr[...]
        hi_ref[0] = cai_scr[...]


def _s5_sample_kernel(u0_ref, u1_ref, d_ref, bre_ref, bim_ref, cre_ref, cim_ref, ar_ref, ai_ref, h0r_ref, h0i_ref,
                      y_ref, hr_ref, hi_ref, up_scr, bur_scr, bui_scr, yp0_scr, yp1_scr, *, nseq, steps):
    _s5_permute_in((u0_ref, u1_ref), up_scr, nseq, steps)
    ub = up_scr[...].astype(bf16)
    bur_scr[...] = _dot(ub, bre_ref[0])
    bui_scr[...] = _dot(ub, bim_ref[0])
    fin_r, fin_i, lw = _s5_scan(bur_scr, bui_scr, ar_ref[...], ai_ref[...], h0r_ref[...], h0i_ref[...], nseq, steps)
    for lb in range(len(fin_r)):
        hr_ref[:, lb * lw:(lb + 1) * lw] = fin_r[lb]
        hi_ref[:, lb * lw:(lb + 1) * lw] = fin_i[lb]
    _s5_output(up_scr, bur_scr, bui_scr, cre_ref, cim_ref, d_ref, (yp0_scr, yp1_scr), y_ref, nseq, steps)


def _s5_blockdiag_in(bb):
    t = bb.reshape(S5_NGT, S5_GT, S5_GROUP, S5_STATE)
    eye = jnp.eye(S5_GT, dtype=bool)
    out = jnp.where(eye[None, :, None, :, None], t[:, :, :, None, :], 0.0)
    return out.reshape(S5_NGT, S5_TU, S5_TL).astype(bf16)


def _s5_blockdiag_out(c):
    t = jnp.transpose(c.reshape(S5_NGT, S5_GT, S5_GROUP, S5_STATE), (0, 1, 3, 2))
    eye = jnp.eye(S5_GT, dtype=bool)
    out = jnp.where(eye[None, :, None, :, None], t[:, :, :, None, :], 0.0)
    return out.reshape(S5_NGT, S5_TL, S5_TU).astype(bf16)


def _s5_common_specs(rows, row_map0, row_map1):
    lane = lambda *a: (0, a[0])
    mat = lambda *a: (a[0], 0, 0)
    return [pl.BlockSpec((rows, _LANES), row_map0),
            pl.BlockSpec((rows, _LANES), row_map1),
            pl.BlockSpec((1, S5_TU), lane),
            pl.BlockSpec((1, S5_TU, S5_TL), mat),
            pl.BlockSpec((1, S5_TU, S5_TL), mat),
            pl.BlockSpec((1, S5_TL, S5_TU), mat),
            pl.BlockSpec((1, S5_TL, S5_TU), mat),
            pl.BlockSpec((1, S5_TL), lane),
            pl.BlockSpec((1, S5_TL), lane)]


def s5_prompt(proj, d, mats, batch, length):
    bre, bim, cre, cim, ar, ai, lr, li = mats
    nseq, steps = 8, CHUNK
    rows = nseq * steps
    nsc = length // rows
    ucol = OFF_U // _LANES
    kern = functools.partial(_s5_prompt_kernel, nseq=nseq, steps=steps, nsc=nsc)
    lane = lambda g, b, s: (0, g)
    st = jax.ShapeDtypeStruct((batch, 1, S5_LANES), f32)
    y, hr, hi = pl.pallas_call(
        kern,
        grid=(S5_NGT, batch, nsc),
        in_specs=_s5_common_specs(rows, lambda g, b, s: (b * nsc + s, ucol + 2 * g),
                                  lambda g, b, s: (b * nsc + s, ucol + 2 * g + 1))
        + [pl.BlockSpec((1, S5_TL), lane), pl.BlockSpec((1, S5_TL), lane)],
        out_specs=[pl.BlockSpec((rows, S5_TU), lambda g, b, s: (b * nsc + s, g)),
                   pl.BlockSpec((1, 1, S5_TL), lambda g, b, s: (b, 0, g)),
                   pl.BlockSpec((1, 1, S5_TL), lambda g, b, s: (b, 0, g))],
        out_shape=[jax.ShapeDtypeStruct((batch * length, D_MODEL), bf16), st, st],
        scratch_shapes=[pltpu.VMEM((rows, S5_TU), f32),
                        pltpu.VMEM((rows, S5_TL), f32), pltpu.VMEM((rows, S5_TL), f32),
                        pltpu.VMEM((rows, _LANES), f32), pltpu.VMEM((rows, _LANES), f32),
                        pltpu.VMEM((steps, S5_TL), f32), pltpu.VMEM((steps, S5_TL), f32),
                        pltpu.VMEM((nseq, S5_TL), f32), pltpu.VMEM((nseq, S5_TL), f32),
                        pltpu.VMEM((1, S5_TL), f32), pltpu.VMEM((1, S5_TL), f32)],
        compiler_params=_cparams("arbitrary", "arbitrary", "arbitrary"),
        name="s5_prompt",
    )(proj, proj, d, bre, bim, cre, cim, ar, ai, lr, li)
    return y, hr.reshape(batch, S5_GROUPS, S5_STATE), hi.reshape(batch, S5_GROUPS, S5_STATE)


def s5_sample(proj, d, mats, h0r, h0i, batch, length):
    bre, bim, cre, cim, ar, ai, _, _ = mats
    rows = batch * length
    ucol = OFF_U // _LANES
    kern = functools.partial(_s5_sample_kernel, nseq=batch, steps=length)
    st = jax.ShapeDtypeStruct((batch, S5_LANES), f32)
    state_spec = pl.BlockSpec((batch, S5_TL), lambda g: (0, g))
    y, hr, hi = pl.pallas_call(
        kern,
        grid=(S5_NGT,),
        in_specs=_s5_common_specs(rows, lambda g: (0, ucol + 2 * g), lambda g: (0, ucol + 2 * g + 1))
        + [state_spec, state_spec],
        out_specs=[pl.BlockSpec((rows, S5_TU), lambda g: (0, g)), state_spec, state_spec],
        out_shape=[jax.ShapeDtypeStruct((rows, D_MODEL), bf16), st, st],
        scratch_shapes=[pltpu.VMEM((rows, S5_TU), f32),
                        pltpu.VMEM((rows, S5_TL), f32), pltpu.VMEM((rows, S5_TL), f32),
                        pltpu.VMEM((rows, _LANES), f32), pltpu.VMEM((rows, _LANES), f32)],
        compiler_params=_cparams("arbitrary"),
        name="s5_sample",
    )(proj, proj, d, bre, bim, cre, cim, ar, ai, h0r.reshape(batch, S5_LANES), h0i.reshape(batch, S5_LANES))
    return y, hr.reshape(batch, S5_GROUPS, S5_STATE), hi.reshape(batch, S5_GROUPS, S5_STATE)


_SSD_PAD = 8
_SSD_TP = 128


def _split3(v):
    p0 = v.astype(bf16)
    r0 = v - p0.astype(f32)
    p1 = r0.astype(bf16)
    p2 = (r0 - p1.astype(f32)).astype(bf16)
    return p0, p1, p2


def _ssd_conv(c, nchunks, xbc_ref, conv0_ref, cw_ref, cb_ref, convnew_ref, stage, lc):
    keep = SSD_CONV_W - 1
    lo = _SSD_PAD - keep

    @pl.when(c == 0)
    def _():
        if conv0_ref is not None:
            stage[lo:_SSD_PAD, :] = conv0_ref[0]
        else:
            stage[0:_SSD_PAD, :] = jnp.zeros((_SSD_PAD, SSD_CONV_DIM), f32)

    stage[_SSD_PAD:_SSD_PAD + lc, :] = xbc_ref[...]
    xc = cb_ref[...]
    for j in range(SSD_CONV_W):
        xc = xc + cw_ref[j:j + 1, :] * stage[lo + j:lo + j + lc, :]
    xc = xc * _sigmoid(xc)
    tail = stage[lo + lc:_SSD_PAD + lc, :]

    @pl.when(c == nchunks - 1)
    def _():
        convnew_ref[0] = tail

    stage[lo:_SSD_PAD, :] = tail
    return xc


def _ssd_dt_cumsum(dt_ref, dtb_ref, alog_ref, lc):
    dtx = dt_ref[...] + dtb_ref[...]
    dtv = jnp.maximum(dtx, 0.0) + jnp.log1p(jnp.exp(-jnp.abs(dtx)))
    da = dtv * (-jnp.exp(alog_ref[...]))
    ri = lax.broadcasted_iota(jnp.int32, (lc, lc), 0)
    ci = lax.broadcasted_iota(jnp.int32, (lc, lc), 1)
    tri = jnp.where(ri >= ci, 1.0, 0.0).astype(bf16)
    p0, p1, p2 = _split3(da)
    cs = _dot(tri, p0) + _dot(tri, p1) + _dot(tri, p2)
    return dtv, cs


def _ssd_parse(refs, has_init, n_alias, n_fixed):
    fixed = refs[:n_fixed]
    conv0_ref, h0_ref = (refs[n_fixed], refs[n_fixed + 1]) if has_init else (None, None)
    rest = refs[n_fixed + 2 * has_init + n_alias:]
    return fixed, conv0_ref, h0_ref, rest


def _ssd_wide_kernel(*refs, nchunks, has_init, n_alias):
    lc = SSD_HEADDIM
    fixed, conv0_ref, h0_ref, rest = _ssd_parse(refs, has_init, n_alias, 10)
    z_ref, xbc_ref, dt_ref, cw_ref, cb_ref, dtb_ref, alog_ref, dexp_ref, ng_ref, rep_ref = fixed
    y_ref, hnew_ref, convnew_ref, stage, ht_scr, xc_scr, cm_scr, dx_scr = rest
    c = pl.program_id(1)

    @pl.when(c == 0)
    def _():
        if has_init:
            for h in range(SSD_HEADS):
                ht_scr[:, h * lc:(h + 1) * lc] = h0_ref[0, h].T
        else:
            ht_scr[...] = jnp.zeros_like(ht_scr)

    xc_scr[...] = _ssd_conv(c, nchunks, xbc_ref, conv0_ref, cw_ref, cb_ref, convnew_ref, stage, lc)
    dtv, cs = _ssd_dt_cumsum(dt_ref, dtb_ref, alog_ref, lc)
    rep = rep_ref[...]
    c0, c1, c2 = _split3(cs)
    cm_scr[...] = _dot(c0, rep) + _dot(c1, rep) + _dot(c2, rep)
    d0, d1, d2 = _split3(dtv)
    dx_scr[...] = _dot(d0, rep) + _dot(d1, rep) + _dot(d2, rep)

    gw = SSD_HPG * SSD_HEADDIM
    ti = lax.broadcasted_iota(jnp.int32, (lc, gw), 0)
    si = lax.broadcasted_iota(jnp.int32, (lc, gw), 1) % lc
    diag = ti == si
    causal = ti >= si
    pw = 2 * SSD_HEADDIM
    bdmask = (lax.broadcasted_iota(jnp.int32, (pw, pw), 0) // lc) == (lax.broadcasted_iota(jnp.int32, (pw, pw), 1) // lc)
    b_off, c_off = SSD_INNER, SSD_INNER + SSD_GROUPS * SSD_STATE
    for g in range(SSD_GROUPS):
        gs = slice(g * gw, (g + 1) * gw)
        cm = cm_scr[:, gs]
        row = jnp.sum(jnp.where(diag, cm, 0.0), axis=0, keepdims=True)
        decay = jnp.exp(jnp.where(causal, cm - row, -jnp.inf))
        last = cm[lc - 1:lc, :]
        xg = xc_scr[:, gs]
        xdt = xg * dx_scr[:, gs]
        bg = xc_scr[:, b_off + g * SSD_STATE:b_off + (g + 1) * SSD_STATE].astype(bf16)
        cg = xc_scr[:, c_off + g * SSD_STATE:c_off + (g + 1) * SSD_STATE].astype(bf16)
        cb2 = _dot_nt(cg, jnp.concatenate([bg, bg], axis=0))
        parts = []
        for pr in range(gw // pw):
            ps = slice(pr * pw, (pr + 1) * pw)
            m = (cb2 * decay[:, ps]).astype(bf16)
            xp = xdt[:, ps]
            bd = jnp.where(bdmask, jnp.concatenate([xp, xp], axis=0), 0.0).astype(bf16)
            parts.append(_dot(m, bd))
        y = jnp.concatenate(parts, axis=1)
        ht = ht_scr[:, gs]
        y = y + _dot(cg, ht.astype(bf16)) * jnp.exp(cm)
        xw = (xdt * jnp.exp(last - cm)).astype(bf16)
        ht_scr[:, gs] = jnp.exp(last) * ht + _dot_tn(bg, xw)
        y = y + dexp_ref[:, gs] * xg
        zg = z_ref[:, gs]
        yg = y * (zg * _sigmoid(zg))
        yg = yg * lax.rsqrt(jnp.mean(yg * yg, axis=-1, keepdims=True) + EPS) * ng_ref[:, gs]
        y_ref[:, gs] = yg.astype(bf16)

    @pl.when(c == nchunks - 1)
    def _():
        for h in range(SSD_HEADS):
            hnew_ref[0, h] = ht_scr[:, h * lc:(h + 1) * lc].T


def _ssd_kernel(*refs, lc, nchunks, has_init, n_alias):
    fixed, conv0_ref, h0_ref, rest = _ssd_parse(refs, has_init, n_alias, 9)
    z_ref, xbc_ref, dt_ref, cw_ref, cb_ref, dtb_ref, alog_ref, dexp_ref, ng_ref = fixed
    y_ref, hnew_ref, convnew_ref, stage, h_scr, pad_scr, y_scr = rest
    c = pl.program_id(1)

    @pl.when(c == 0)
    def _():
        if has_init:
            h_scr[...] = h0_ref[0]
        else:
            h_scr[...] = jnp.zeros_like(h_scr)

    xc = _ssd_conv(c, nchunks, xbc_ref, conv0_ref, cw_ref, cb_ref, convnew_ref, stage, lc)
    dtv, cs = _ssd_dt_cumsum(dt_ref, dtb_ref, alog_ref, lc)
    causal = lax.broadcasted_iota(jnp.int32, (lc, lc), 0) >= lax.broadcasted_iota(jnp.int32, (lc, lc), 1)
    pad_scr[...] = jnp.zeros_like(pad_scr)
    pad_scr[0:lc, 0:SSD_HEADS] = cs
    cst = pad_scr[...].T
    ecs = jnp.exp(cs)
    cs_last = cs[lc - 1:lc, :]
    wdec = jnp.exp(cs_last - cs)
    elast = jnp.exp(cs_last)
    xs_off, b_off, c_off = 0, SSD_INNER, SSD_INNER + SSD_GROUPS * SSD_STATE
    for g in range(SSD_GROUPS):
        bg = xc[:, b_off + g * SSD_STATE:b_off + (g + 1) * SSD_STATE].astype(bf16)
        cg = xc[:, c_off + g * SSD_STATE:c_off + (g + 1) * SSD_STATE].astype(bf16)
        cb = _dot_nt(cg, bg)
        for r in range(SSD_HPG):
            h = g * SSD_HPG + r
            hs = slice(h * SSD_HEADDIM, (h + 1) * SSD_HEADDIM)
            col = cs[:, h:h + 1]
            row = cst[h:h + 1, 0:lc]
            decay = jnp.exp(jnp.where(causal, col - row, -jnp.inf))
            xh = xc[:, hs]
            xdt = xh * dtv[:, h:h + 1]
            hst = h_scr[h]
            y = _dot((cb * decay).astype(bf16), xdt.astype(bf16))
            y = y + _dot_nt(cg, hst.astype(bf16)) * ecs[:, h:h + 1]
            xw = (xdt * wdec[:, h:h + 1]).astype(bf16)
            h_scr[h] = elast[:, h:h + 1] * hst + _dot_tn(xw, bg)
            y_scr[:, hs] = y + dexp_ref[:, hs] * xh
        gs = slice(g * SSD_HPG * SSD_HEADDIM, (g + 1) * SSD_HPG * SSD_HEADDIM)
        zg = z_ref[:, gs]
        yg = y_scr[:, gs] * (zg * _sigmoid(zg))
        yg = yg * lax.rsqrt(jnp.mean(yg * yg, axis=-1, keepdims=True) + EPS) * ng_ref[:, gs]
        y_ref[:, gs] = yg.astype(bf16)

    @pl.when(c == nchunks - 1)
    def _():
        hnew_ref[0] = h_scr[...]


def ssd(proj, dt_raw, conv_w, conv_b, dt_bias, a_log, d, norm_g, batch, length, *, h_all, conv_all, prev_h,
        prev_conv, layer, depth):
    lc = min(length, CHUNK)
    nc = length // lc
    wide = lc == SSD_HEADDIM
    dexp = jnp.repeat(d, SSD_HEADDIM).reshape(1, SSD_INNER)
    row = lambda b, c: (b * nc + c, 0)
    const2 = lambda b, c: (0, 0)
    h_tail = (SSD_HEADS, SSD_HEADDIM, SSD_STATE)
    c_tail = (SSD_CONV_W - 1, SSD_CONV_DIM)
    has_init = h_all is not None
    has_prev = prev_h is not None
    h_in, h_specs, h_out_spec = _layered_state_specs(layer * batch, h_tail, h_all, None)
    c_in, c_specs, c_out_spec = _layered_state_specs(layer * batch, c_tail, conv_all, None)
    ins = [proj, proj, dt_raw, conv_w, conv_b.reshape(1, -1), dt_bias.reshape(1, -1), a_log.reshape(1, -1), dexp,
           norm_g.reshape(1, -1)]
    specs = [pl.BlockSpec((lc, SSD_INNER), lambda b, c: (b * nc + c, OFF_Z // SSD_INNER)),
             pl.BlockSpec((lc, SSD_CONV_DIM), lambda b, c: (b * nc + c, OFF_XBC // SSD_CONV_DIM)),
             pl.BlockSpec((lc, SSD_HEADS), row),
             pl.BlockSpec((SSD_CONV_W, SSD_CONV_DIM), const2),
             pl.BlockSpec((1, SSD_CONV_DIM), const2),
             pl.BlockSpec((1, SSD_HEADS), const2),
             pl.BlockSpec((1, SSD_HEADS), const2),
             pl.BlockSpec((1, SSD_INNER), const2),
             pl.BlockSpec((1, SSD_INNER), const2)]
    if wide:
        rep = jnp.repeat(jnp.eye(SSD_HEADS, dtype=bf16), SSD_HEADDIM, axis=1)
        ins.append(rep)
        specs.append(pl.BlockSpec((SSD_HEADS, SSD_INNER), const2))
        kern = functools.partial(_ssd_wide_kernel, nchunks=nc, has_init=has_init, n_alias=2 * has_prev)
        scratch = [pltpu.VMEM((_SSD_PAD + lc, SSD_CONV_DIM), f32),
                   pltpu.VMEM((SSD_STATE, SSD_INNER), f32),
                   pltpu.VMEM((lc, SSD_CONV_DIM), f32),
                   pltpu.VMEM((lc, SSD_INNER), f32),
                   pltpu.VMEM((lc, SSD_INNER), f32)]
    else:
        kern = functools.partial(_ssd_kernel, lc=lc, nchunks=nc, has_init=has_init, n_alias=2 * has_prev)
        scratch = [pltpu.VMEM((_SSD_PAD + lc, SSD_CONV_DIM), f32),
                   pltpu.VMEM(h_tail, f32),
                   pltpu.VMEM((_SSD_TP, _SSD_TP), f32),
                   pltpu.VMEM((lc, SSD_INNER), f32)]
    ins += c_in + h_in
    specs += c_specs + h_specs
    aliases = {}
    if has_prev:
        aliases = {len(ins): 1, len(ins) + 1: 2}
        ins += [prev_h, prev_conv]
        specs += [pl.BlockSpec(memory_space=pl.ANY), pl.BlockSpec(memory_space=pl.ANY)]
    return pl.pallas_call(
        kern,
        grid=(batch, nc),
        in_specs=specs,
        out_specs=[pl.BlockSpec((lc, SSD_INNER), row), h_out_spec, c_out_spec],
        out_shape=[jax.ShapeDtypeStruct((batch * length, SSD_INNER), bf16),
                   jax.ShapeDtypeStruct((depth * batch,) + h_tail, f32),
                   jax.ShapeDtypeStruct((depth * batch,) + c_tail, f32)],
        scratch_shapes=scratch,
        input_output_aliases=aliases,
        compiler_params=_cparams("parallel", "arbitrary"),
        name="ssd_wide" if wide else "ssd",
    )(*ins)


def _merge_kernel(ra_ref, sa_ref, da_ref, gr_ref, gs_ref, gd_ref, wr_ref, wv_ref, wg_ref, wd_ref, o_ref):
    sa = sa_ref[...]
    ret = _dot(ra_ref[...], wr_ref[...])
    s5 = _dot(sa, wv_ref[...]) * _sigmoid(_dot(sa, wg_ref[...]))
    sd = _dot(da_ref[...], wd_ref[...])
    merged = _sigmoid(gr_ref[...]) * ret + _sigmoid(gs_ref[...]) * s5 + _sigmoid(gd_ref[...]) * sd
    o_ref[...] = merged.astype(bf16)


def merge(ret_act, s5_act, ssd_act, gates, w_ret_o, w_glu, w_ssd_out, *, tm, tn):
    n = ret_act.shape[0]
    nj = D_MODEL // tn
    act = lambda i, j: (i, 0)
    return pl.pallas_call(
        _merge_kernel,
        grid=(n // tm, nj),
        in_specs=[pl.BlockSpec((tm, R_VAL), act),
                  pl.BlockSpec((tm, D_MODEL), act),
                  pl.BlockSpec((tm, SSD_INNER), act),
                  pl.BlockSpec((tm, tn), lambda i, j: (i, j)),
                  pl.BlockSpec((tm, tn), lambda i, j: (i, nj + j)),
                  pl.BlockSpec((tm, tn), lambda i, j: (i, 2 * nj + j)),
                  pl.BlockSpec((R_VAL, tn), lambda i, j: (0, j)),
                  pl.BlockSpec((D_MODEL, tn), lambda i, j: (0, j)),
                  pl.BlockSpec((D_MODEL, tn), lambda i, j: (0, nj + j)),
                  pl.BlockSpec((SSD_INNER, tn), lambda i, j: (0, j))],
        out_specs=pl.BlockSpec((tm, tn), lambda i, j: (i, j)),
        out_shape=jax.ShapeDtypeStruct((n, D_MODEL), bf16),
        compiler_params=_cparams("parallel", "arbitrary"),
        name="merge",
    )(ret_act, s5_act, ssd_act, gates, gates, gates, w_ret_o, w_glu, w_glu, w_ssd_out)


def _attn_kernel(q_ref, k_ref, v_ref, o_ref):
    scale = X_HEAD_DIM ** -0.5
    for h in range(X_HEADS):
        hs = slice(h * X_HEAD_DIM, (h + 1) * X_HEAD_DIM)
        s = _dot_nt(q_ref[:, hs], k_ref[0, :, hs].astype(bf16)) * scale
        e = jnp.exp(s - jnp.max(s, axis=-1, keepdims=True))
        p = e / jnp.sum(e, axis=-1, keepdims=True)
        o_ref[:, hs] = _dot(p.astype(bf16), v_ref[0, :, hs].astype(bf16)).astype(bf16)


def attention(q, mem_k, mem_v, batch, length, *, tq, kv_base):
    nt = length // tq
    kv = pl.BlockSpec((1, MEM_LEN, D_MODEL), lambda b, i: (kv_base + b, 0, 0))
    return pl.pallas_call(
        _attn_kernel,
        grid=(batch, nt),
        in_specs=[pl.BlockSpec((tq, D_MODEL), lambda b, i: (b * nt + i, 0)), kv, kv],
        out_specs=pl.BlockSpec((tq, D_MODEL), lambda b, i: (b * nt + i, 0)),
        out_shape=jax.ShapeDtypeStruct((batch * length, D_MODEL), bf16),
        compiler_params=_cparams("parallel", "arbitrary"),
        name="attention",
    )(q, mem_k, mem_v)


def _mlp_kernel(x_ref, g_in_ref, g_out_ref, wu_ref, wd_ref, o_ref, hn_scr, acc_scr, *, nff):
    j = pl.program_id(1)

    @pl.when(j == 0)
    def _():
        hn_scr[...] = _rms(x_ref[...], g_in_ref[...]).astype(bf16)
        acc_scr[...] = jnp.zeros_like(acc_scr)

    h = jnp.maximum(_dot(hn_scr[...], wu_ref[...]), 0.0)
    acc_scr[...] += _dot((h * h).astype(bf16), wd_ref[...])

    @pl.when(j == nff - 1)
    def _():
        o_ref[...] = x_ref[...] + _rms(acc_scr[...], g_out_ref[...])


def mlp(x, g_in, g_out, w_up, w_down, *, tm, tf):
    n, d = x.shape
    nff = D_FF // tf
    kern = functools.partial(_mlp_kernel, nff=nff)
    return pl.pallas_call(
        kern,
        grid=(n // tm, nff),
        in_specs=[pl.BlockSpec((tm, d), lambda i, j: (i, 0)),
                  pl.BlockSpec((1, d), lambda i, j: (0, 0)),
                  pl.BlockSpec((1, d), lambda i, j: (0, 0)),
                  pl.BlockSpec((d, tf), lambda i, j: (0, j)),
                  pl.BlockSpec((tf, d), lambda i, j: (j, 0))],
        out_specs=pl.BlockSpec((tm, d), lambda i, j: (i, 0)),
        out_shape=jax.ShapeDtypeStruct((n, d), f32),
        scratch_shapes=[pltpu.VMEM((tm, d), bf16), pltpu.VMEM((tm, d), f32)],
        compiler_params=_cparams("parallel", "arbitrary"),
        name="mlp",
    )(x, g_in, g_out, w_up, w_down)


def _run_layer(x, batch, length, pos0, mem_k, mem_v, kv_base, carried, prev, layer, depth, lw, *, prompt):
    gains = lw["gains"]
    n = x.shape[0]
    tm = min(512, n)
    tm_in = min(1024, n)
    proj = norm_matmul(x, gains[0:1], lw["w_main"], tm=tm_in, tn=512, out_dtype=f32)
    dt_raw = norm_matmul(x, gains[0:1], lw["w_dt"], tm=tm_in, tn=SSD_HEADS, out_dtype=f32)
    gates = norm_matmul(x, gains[0:1], lw["w_gate"], tm=tm_in, tn=512, out_dtype=f32)
    ret_act, ret_all = retention(proj, lw["ret_gn"], batch, length, pos0, state_all=carried["ret"],
                                 prev_out=prev["ret"], layer=layer, depth=depth)
    if prompt:
        s5_act, s5_r_new, s5_i_new = s5_prompt(proj, lw["s5_d"], lw["s5_mats"], batch, length)
    else:
        s5_act, s5_r_new, s5_i_new = s5_sample(proj, lw["s5_d"], lw["s5_mats"], carried["s5_r"], carried["s5_i"],
                                               batch, length)
    ssd_act, ssd_all, conv_all = ssd(proj, dt_raw, lw["ssd_conv_w"], lw["ssd_conv_b"], lw["ssd_dt_bias"],
                                     lw["ssd_a_log"], lw["ssd_d"], lw["ssd_norm"], batch, length,
                                     h_all=carried["ssd"], conv_all=carried["conv"], prev_h=prev["ssd"],
                                     prev_conv=prev["conv"], layer=layer, depth=depth)
    merged = merge(ret_act, s5_act, ssd_act, gates, lw["w_ret_o"], lw["w_s5_glu"], lw["w_ssd_out"], tm=tm, tn=256)
    x = matmul_norm_res(merged, lw["w_mix_out"], gains[1:2], x, tm=tm)
    q = norm_matmul(x, gains[2:3], lw["w_xq"], tm=tm_in, tn=512, out_dtype=bf16)
    att = attention(q, mem_k, mem_v, batch, length, tq=min(length, 512), kv_base=kv_base)
    x = matmul_norm_res(att, lw["w_xo"], gains[3:4], x, tm=tm)
    x = mlp(x, gains[4:5], gains[5:6], lw["w_up"], lw["w_down"], tm=tm, tf=1024)
    return x, dict(ret=ret_all, ssd=ssd_all, conv=conv_all), (s5_r_new, s5_i_new)


def kernel(x_prompt, x_sample, mem_prompt, state_ret, state_s5_re, state_s5_im, state_ssd, cache_ssd_conv,
           cache_mem_k, cache_mem_v, norm_gains, w_in, ret_gn, w_ret_o, s5_a_re, s5_a_im, s5_b_re, s5_b_im,
           s5_c_re, s5_c_im, s5_d, s5_log_dt, w_s5_glu, ssd_conv_w, ssd_conv_b, ssd_dt_bias, ssd_a_log, ssd_d,
           ssd_norm, w_ssd_out, w_mix_out, w_xq, w_xkv, w_xo, w_up, w_down):
    b, l, _ = x_prompt.shape
    db, dl, _ = x_sample.shape
    depth = w_in.shape[0]
    yp = x_prompt.reshape(b * l, D_MODEL)
    ys = x_sample.reshape(db * dl, D_MODEL)
    mem2d = mem_prompt.reshape(b * MEM_LEN, D_MODEL)
    cache_k = cache_mem_k.reshape(depth * db, MEM_LEN, D_MODEL)
    cache_v = cache_mem_v.reshape(depth * db, MEM_LEN, D_MODEL)
    s_carried = dict(ret=state_ret.reshape(depth * db, R_HEADS, R_DK, R_DV),
                     ssd=state_ssd.reshape(depth * db, SSD_HEADS, SSD_HEADDIM, SSD_STATE),
                     conv=cache_ssd_conv.reshape(depth * db, SSD_CONV_W - 1, SSD_CONV_DIM))
    p_carried = dict(ret=None, ssd=None, conv=None)
    p_prev = dict(ret=None, ssd=None, conv=None)
    s_prev = dict(ret=None, ssd=None, conv=None)
    p_s5, s_s5, p_mk, p_mv = [], [], [], []
    for i in range(depth):
        ar, ai, lr, li, bbr, bbi = s5_prep(s5_a_re[i], s5_a_im[i], s5_log_dt[i],
                                           jnp.swapaxes(s5_b_re[i], 1, 2), jnp.swapaxes(s5_b_im[i], 1, 2))
        lane = lambda t: t.reshape(1, S5_LANES)
        lw = dict(
            gains=norm_gains[i],
            w_main=w_in[i, :, :MAIN_COLS].astype(bf16),
            w_dt=w_in[i, :, OFF_DT:OFF_GATE].astype(bf16),
            w_gate=w_in[i, :, OFF_GATE:].astype(bf16),
            ret_gn=ret_gn[i].reshape(1, R_VAL),
            w_ret_o=w_ret_o[i].astype(bf16),
            s5_d=s5_d[i].reshape(1, D_MODEL),
            s5_mats=(_s5_blockdiag_in(bbr), _s5_blockdiag_in(bbi),
                     _s5_blockdiag_out(s5_c_re[i]), _s5_blockdiag_out(s5_c_im[i]),
                     lane(ar), lane(ai), lane(lr), lane(li)),
            w_s5_glu=w_s5_glu[i].astype(bf16),
            ssd_conv_w=ssd_conv_w[i], ssd_conv_b=ssd_conv_b[i], ssd_dt_bias=ssd_dt_bias[i],
            ssd_a_log=ssd_a_log[i], ssd_d=ssd_d[i], ssd_norm=ssd_norm[i],
            w_ssd_out=w_ssd_out[i].astype(bf16),
            w_mix_out=w_mix_out[i].astype(bf16),
            w_xq=w_xq[i].astype(bf16),
            w_xo=w_xo[i].astype(bf16),
            w_up=w_up[i].astype(bf16),
            w_down=w_down[i].astype(bf16),
        )
        kv = norm_matmul(mem2d, norm_gains[i, 6:7], w_xkv[i].astype(bf16), tm=b * MEM_LEN, tn=512, out_dtype=f32)
        mk = kv[:, :D_MODEL].reshape(b, MEM_LEN, D_MODEL)
        mv = kv[:, D_MODEL:].reshape(b, MEM_LEN, D_MODEL)
        p_mk.append(mk.reshape(b, MEM_LEN, X_HEADS, X_HEAD_DIM))
        p_mv.append(mv.reshape(b, MEM_LEN, X_HEADS, X_HEAD_DIM))
        yp, p_prev, s5p = _run_layer(yp, b, l, 0, mk, mv, 0, p_carried, p_prev, i, depth, lw, prompt=True)
        p_s5.append(s5p)
        s_carried_i = dict(s_carried, s5_r=state_s5_re[i], s5_i=state_s5_im[i])
        ys, s_prev, s5s = _run_layer(ys, db, dl, PAST_LEN, cache_k, cache_v, i * db, s_carried_i, s_prev, i, depth,
                                     lw, prompt=False)
        s_s5.append(s5s)
    p_s5_re, p_s5_im = [jnp.stack(t) for t in zip(*p_s5)]
    s_s5_re, s_s5_im = [jnp.stack(t) for t in zip(*s_s5)]
    unstack = lambda t, bsz: t.reshape((depth, bsz) + t.shape[1:])
    return (yp.reshape(b, l, D_MODEL), ys.reshape(db, dl, D_MODEL),
            unstack(p_prev["ret"], b), p_s5_re, p_s5_im, unstack(p_prev["ssd"], b), unstack(p_prev["conv"], b),
            jnp.stack(p_mk), jnp.stack(p_mv),
            unstack(s_prev["ret"], db), s_s5_re, s_s5_im, unstack(s_prev["ssd"], db), unstack(s_prev["conv"], db))
```

```python
import functools
import math

import jax
import jax.numpy as jnp
from jax import lax
from jax.experimental import pallas as pl
from jax.experimental.pallas import tpu as pltpu

f32 = jnp.float32
bf16 = jnp.bfloat16

D_MODEL = 2048
PAST_LEN = 1024
CHUNK = 64
EPS = 1e-6
R_HEADS = 8
R_DK = 128
R_DV = 256
R_QK = R_HEADS * R_DK
R_VAL = R_HEADS * R_DV
ROPE_BASE = 10000.0
S5_GROUP = 16
S5_GROUPS = D_MODEL // S5_GROUP
S5_STATE = 64
S5_LANES = S5_GROUPS * S5_STATE
S5_GT = 16
S5_NGT = S5_GROUPS // S5_GT
S5_TU = S5_GT * S5_GROUP
S5_TL = S5_GT * S5_STATE
SSD_INNER = 2 * D_MODEL
SSD_HEADDIM = 64
SSD_HEADS = SSD_INNER // SSD_HEADDIM
SSD_GROUPS = 8
SSD_HPG = SSD_HEADS // SSD_GROUPS
SSD_STATE = 128
SSD_CONV_W = 4
SSD_CONV_DIM = SSD_INNER + 2 * SSD_GROUPS * SSD_STATE
MEM_LEN = 256
X_HEADS = 4
X_HEAD_DIM = D_MODEL // X_HEADS
D_FF = 4 * D_MODEL
OFF_Q = 0
OFF_K = OFF_Q + R_QK
OFF_V = OFF_K + R_QK
OFF_G = OFF_V + R_VAL
OFF_U = OFF_G + R_VAL
OFF_Z = OFF_U + D_MODEL
OFF_XBC = OFF_Z + SSD_INNER
OFF_DT = OFF_XBC + SSD_CONV_DIM
OFF_GATE = OFF_DT + SSD_HEADS
MAIN_COLS = OFF_DT

VMEM_LIMIT_BYTES = 52 * 1024 * 1024


def _cparams(*sem):
    return pltpu.CompilerParams(dimension_semantics=sem, vmem_limit_bytes=VMEM_LIMIT_BYTES)


def _rms(x, g):
    return x * lax.rsqrt(jnp.mean(x * x, axis=-1, keepdims=True) + EPS) * g


def _dot(a, b):
    return jnp.dot(a, b, preferred_element_type=f32)


def _dot_nt(a, b):
    return lax.dot_general(a, b, (((1,), (1,)), ((), ())), preferred_element_type=f32)


def _dot_tn(a, b):
    return lax.dot_general(a, b, (((0,), (0,)), ((), ())), preferred_element_type=f32)


def _sigmoid(x):
    return 1.0 / (1.0 + jnp.exp(-x))


def _norm_matmul_kernel(x_ref, g_ref, w_ref, o_ref, hn_ref):
    @pl.when(pl.program_id(1) == 0)
    def _():
        hn_ref[...] = _rms(x_ref[...], g_ref[...]).astype(bf16)

    o_ref[...] = _dot(hn_ref[...], w_ref[...]).astype(o_ref.dtype)


def norm_matmul(x, g, w, *, tm, tn, out_dtype, ncols=None, layer=None):
    n, k = x.shape
    m = ncols if ncols is not None else w.shape[-1]
    if layer is None:
        w_spec = pl.BlockSpec((k, tn), lambda i, j: (0, j))
    else:
        w_spec = pl.BlockSpec((None, k, tn), lambda i, j: (layer, 0, j))
    return pl.pallas_call(
        _norm_matmul_kernel,
        grid=(n // tm, m // tn),
        in_specs=[pl.BlockSpec((tm, k), lambda i, j: (i, 0)),
                  pl.BlockSpec((1, k), lambda i, j: (0, 0)),
                  w_spec],
        out_specs=pl.BlockSpec((tm, tn), lambda i, j: (i, j)),
        out_shape=jax.ShapeDtypeStruct((n, m), out_dtype),
        scratch_shapes=[pltpu.VMEM((tm, k), bf16)],
        compiler_params=_cparams("parallel", "arbitrary"),
        name="norm_matmul",
    )(x, g, w)


def _matmul_norm_res_kernel(a_ref, w_ref, g_ref, x_ref, o_ref):
    y = _dot(a_ref[...], w_ref[...])
    o_ref[...] = x_ref[...] + _rms(y, g_ref[...])


def matmul_norm_res(a, w, g, x, *, tm):
    n, k = a.shape
    d = w.shape[1]
    return pl.pallas_call(
        _matmul_norm_res_kernel,
        grid=(n // tm,),
        in_specs=[pl.BlockSpec((tm, k), lambda i: (i, 0)),
                  pl.BlockSpec((k, d), lambda i: (0, 0)),
                  pl.BlockSpec((1, d), lambda i: (0, 0)),
                  pl.BlockSpec((tm, d), lambda i: (i, 0))],
        out_specs=pl.BlockSpec((tm, d), lambda i: (i, 0)),
        out_shape=jax.ShapeDtypeStruct((n, d), f32),
        compiler_params=_cparams("parallel"),
        name="matmul_norm_res",
    )(a, w, g, x)


def _layered_state_specs(layer_rows, tail, state_all, prev_out):
    zeros = (0,) * len(tail)
    spec = pl.BlockSpec((1,) + tail, lambda b, c: (layer_rows + b,) + zeros)
    ins, specs = [], []
    if state_all is not None:
        ins.append(state_all)
        specs.append(spec)
    if prev_out is not None:
        ins.append(prev_out)
        specs.append(pl.BlockSpec(memory_space=pl.ANY))
    return ins, specs, spec


def _retention_kernel(*refs, nchunks, has_init, n_alias):
    q_ref, k_ref, v_ref, g_ref, cos_ref, sin_ref, intra_ref, din_ref, dup_ref, dall_ref, gn_ref = refs[:11]
    s0_ref = refs[11] if has_init else None
    o_ref, snew_ref, s_scr = refs[11 + has_init + n_alias:]
    c = pl.program_id(1)

    @pl.when(c == 0)
    def _():
        if has_init:
            s_scr[...] = s0_ref[0]
        else:
            s_scr[...] = jnp.zeros_like(s_scr)

    cos = cos_ref[...]
    sin = sin_ref[...]
    for h in range(R_HEADS):
        q = q_ref[:, h * R_DK:(h + 1) * R_DK]
        k = k_ref[:, h * R_DK:(h + 1) * R_DK]
        v = v_ref[:, h * R_DV:(h + 1) * R_DV].astype(bf16)
        qr = (q * cos + pltpu.roll(q, R_DK // 2, 1) * sin)
        kr = (k * cos + pltpu.roll(k, R_DK // 2, 1) * sin) * (R_DK ** -0.5)
        qb = qr.astype(bf16)
        s = s_scr[h]
        scores = _dot_nt(qb, kr.astype(bf16)) * intra_ref[h]
        out = _dot(scores.astype(bf16), v) + _dot(qb, s.astype(bf16)) * din_ref[h]
        s_scr[h] = dall_ref[h] * s + _dot_tn((kr * dup_ref[h]).astype(bf16), v)
        mu = jnp.mean(out, axis=-1, keepdims=True)
        cen = out - mu
        var = jnp.mean(cen * cen, axis=-1, keepdims=True)
        o = cen * lax.rsqrt(var + EPS) * gn_ref[:, h * R_DV:(h + 1) * R_DV]
        gate = g_ref[:, h * R_DV:(h + 1) * R_DV]
        o_ref[:, h * R_DV:(h + 1) * R_DV] = (gate * _sigmoid(gate) * o).astype(bf16)

    @pl.when(c == nchunks - 1)
    def _():
        snew_ref[0] = s_scr[...]


def retention(proj, gn, batch, length, pos0, *, state_all, prev_out, layer, depth):
    lc = min(length, CHUNK)
    nc = length // lc
    half = R_DK // 2
    inv_freq = jnp.exp(-math.log(ROPE_BASE) * jnp.arange(half, dtype=f32) / half)
    pos = (pos0 + jnp.arange(length, dtype=jnp.int32)).astype(f32)
    ang = pos[:, None] * inv_freq[None]
    cos2 = jnp.concatenate([jnp.cos(ang), jnp.cos(ang)], axis=-1)
    sin2 = jnp.concatenate([-jnp.sin(ang), jnp.sin(ang)], axis=-1)
    log_gamma = jnp.log1p(-jnp.exp2(-5.0 - jnp.arange(R_HEADS, dtype=f32)))
    idx = jnp.arange(lc, dtype=f32)
    lg = log_gamma[:, None]
    intra = jnp.exp(jnp.abs(idx[:, None] - idx[None, :])[None] * lg[:, :, None])
    din = jnp.broadcast_to(jnp.exp((idx + 1.0)[None] * lg)[:, :, None], (R_HEADS, lc, R_DV))
    dup = jnp.broadcast_to(jnp.exp((lc - 1.0 - idx)[None] * lg)[:, :, None], (R_HEADS, lc, R_DK))
    dall = jnp.exp(lc * log_gamma)
    tail = (R_HEADS, R_DK, R_DV)
    st_in, st_specs, st_out_spec = _layered_state_specs(layer * batch, tail, state_all, prev_out)
    n_fixed = 11
    kern = functools.partial(_retention_kernel, nchunks=nc, has_init=state_all is not None,
                             n_alias=int(prev_out is not None))
    aliases = {n_fixed + len(st_in) - 1: 1} if prev_out is not None else {}
    return pl.pallas_call(
        kern,
        grid=(batch, nc),
        in_specs=[pl.BlockSpec((lc, R_QK), lambda b, c: (b * nc + c, OFF_Q // R_QK)),
                  pl.BlockSpec((lc, R_QK), lambda b, c: (b * nc + c, OFF_K // R_QK)),
                  pl.BlockSpec((lc, R_VAL), lambda b, c: (b * nc + c, OFF_V // R_VAL)),
                  pl.BlockSpec((lc, R_VAL), lambda b, c: (b * nc + c, OFF_G // R_VAL)),
                  pl.BlockSpec((lc, R_DK), lambda b, c: (c, 0)),
                  pl.BlockSpec((lc, R_DK), lambda b, c: (c, 0)),
                  pl.BlockSpec((R_HEADS, lc, lc), lambda b, c: (0, 0, 0)),
                  pl.BlockSpec((R_HEADS, lc, R_DV), lambda b, c: (0, 0, 0)),
                  pl.BlockSpec((R_HEADS, lc, R_DK), lambda b, c: (0, 0, 0)),
                  pl.BlockSpec(memory_space=pltpu.SMEM),
                  pl.BlockSpec((1, R_VAL), lambda b, c: (0, 0))] + st_specs,
        out_specs=[pl.BlockSpec((lc, R_VAL), lambda b, c: (b * nc + c, 0)), st_out_spec],
        out_shape=[jax.ShapeDtypeStruct((batch * length, R_VAL), bf16),
                   jax.ShapeDtypeStruct((depth * batch,) + tail, f32)],
        scratch_shapes=[pltpu.VMEM(tail, f32)],
        input_output_aliases=aliases,
        compiler_params=_cparams("parallel", "arbitrary"),
        name="retention",
    )(proj, proj, proj, proj, cos2, sin2, intra, din, dup, dall, gn, *st_in)


def _s5_prep_kernel(are_ref, aim_ref, ldt_ref, bre_ref, bim_ref, ar_ref, ai_ref, lr_ref, li_ref, bbr_ref, bbi_ref):
    a_re = are_ref[...]
    a_im = aim_ref[...]
    dt = jnp.exp(ldt_ref[...])
    lr = dt * a_re
    li = dt * a_im
    mag = jnp.exp(lr)
    ar = mag * jnp.cos(li)
    ai = mag * jnp.sin(li)
    nr = ar - 1.0
    den = a_re * a_re + a_im * a_im
    cr = (nr * a_re + ai * a_im) / den
    ci = (ai * a_re - nr * a_im) / den
    ar_ref[...] = ar
    ai_ref[...] = ai
    lr_ref[...] = lr
    li_ref[...] = li
    b_re = bre_ref[...]
    b_im = bim_ref[...]
    bbr_ref[...] = cr[:, None, :] * b_re - ci[:, None, :] * b_im
    bbi_ref[...] = cr[:, None, :] * b_im + ci[:, None, :] * b_re


def s5_prep(a_re, a_im, log_dt, b_re, b_im):
    gp = jax.ShapeDtypeStruct((S5_GROUPS, S5_STATE), f32)
    gb = jax.ShapeDtypeStruct((S5_GROUPS, S5_GROUP, S5_STATE), f32)
    return pl.pallas_call(
        _s5_prep_kernel,
        out_shape=[gp, gp, gp, gp, gb, gb],
        compiler_params=pltpu.CompilerParams(vmem_limit_bytes=VMEM_LIMIT_BYTES),
        name="s5_prep",
    )(a_re, a_im, log_dt.reshape(S5_GROUPS, 1), b_re, b_im)


def _s5_project_in(u_ref, perm_ref, bre_ref, bim_ref, bur_scr, bui_scr):
    ub = u_ref[...].astype(bf16)
    up = _dot(perm_ref[...], ub).astype(bf16)
    bur_scr[...] = _dot(up, bre_ref[0])
    bui_scr[...] = _dot(up, bim_ref[0])


def _s5_scan(bur_scr, bui_scr, ar_row, ai_row, init_r, init_i, nseq, steps):
    lw = min(S5_TL, S5_LANES // nseq)
    fin_r, fin_i = [], []
    for lb in range(S5_TL // lw):
        sl = slice(lb * lw, (lb + 1) * lw)
        ar = jnp.broadcast_to(ar_row[:, sl], (nseq, lw))
        ai = jnp.broadcast_to(ai_row[:, sl], (nseq, lw))
        xr = init_r[:, sl]
        xi = init_i[:, sl]
        for t in range(steps):
            rows = slice(t * nseq, (t + 1) * nseq)
            nxr = ar * xr - ai * xi + bur_scr[rows, sl]
            nxi = ar * xi + ai * xr + bui_scr[rows, sl]
            xr, xi = nxr, nxi
            bur_scr[rows, sl] = xr
            bui_scr[rows, sl] = xi
        fin_r.append(xr)
        fin_i.append(xi)
    return fin_r, fin_i, lw


def _s5_output(u_ref, xr, xi, cre_ref, cim_ref, d_ref, permt_ref, y_ref):
    y = _dot(xr, cre_ref[0]) - _dot(xi, cim_ref[0])
    pt = permt_ref[...]
    y0, y1, y2 = _split3(y)
    y = _dot(pt, y0) + _dot(pt, y1) + _dot(pt, y2)
    y_ref[...] = jax.nn.gelu(y + d_ref[...] * u_ref[...]).astype(bf16)


def _s5_prompt_kernel(u_ref, d_ref, bre_ref, bim_ref, cre_ref, cim_ref, ar_ref, ai_ref, lr_ref, li_ref,
                      perm_ref, permt_ref, y_ref, hr_ref, hi_ref,
                      bur_scr, bui_scr, xbr_scr, xbi_scr, apr_scr, api_scr, xsr_scr, xsi_scr, car_scr, cai_scr,
                      *, nseq, steps, nsc):
    b = pl.program_id(1)
    sc = pl.program_id(2)

    @pl.when((b == 0) & (sc == 0))
    def _():
        tt = (lax.broadcasted_iota(jnp.int32, (steps, S5_TL), 0) + 1).astype(f32)
        mag = jnp.exp(tt * lr_ref[...])
        pwr = mag * jnp.cos(tt * li_ref[...])
        pwi = mag * jnp.sin(tt * li_ref[...])
        for t in range(steps):
            apr_scr[t * nseq:(t + 1) * nseq, :] = jnp.broadcast_to(pwr[t:t + 1, :], (nseq, S5_TL))
            api_scr[t * nseq:(t + 1) * nseq, :] = jnp.broadcast_to(pwi[t:t + 1, :], (nseq, S5_TL))

    @pl.when(sc == 0)
    def _():
        car_scr[...] = jnp.zeros_like(car_scr)
        cai_scr[...] = jnp.zeros_like(cai_scr)

    _s5_project_in(u_ref, perm_ref, bre_ref, bim_ref, bur_scr, bui_scr)
    zero = jnp.zeros((nseq, S5_TL), f32)
    _s5_scan(bur_scr, bui_scr, ar_ref[...], ai_ref[...], zero, zero, nseq, steps)
    pr = apr_scr[(steps - 1) * nseq:(steps - 1) * nseq + 1, :]
    pi = api_scr[(steps - 1) * nseq:(steps - 1) * nseq + 1, :]
    xr = car_scr[...]
    xi = cai_scr[...]
    for r in range(nseq):
        xsr_scr[r:r + 1, :] = xr
        xsi_scr[r:r + 1, :] = xi
        er = bur_scr[(steps - 1) * nseq + r:(steps - 1) * nseq + r + 1, :]
        ei = bui_scr[(steps - 1) * nseq + r:(steps - 1) * nseq + r + 1, :]
        xr, xi = er + pr * xr - pi * xi, ei + pr * xi + pi * xr
    car_scr[...] = xr
    cai_scr[...] = xi
    blk = 2 * nseq
    lq = S5_TL // 4
    for q in range(S5_TL // lq):
        sl = slice(q * lq, (q + 1) * lq)
        xsr = jnp.concatenate([xsr_scr[:, sl], xsr_scr[:, sl]], axis=0)
        xsi = jnp.concatenate([xsi_scr[:, sl], xsi_scr[:, sl]], axis=0)
        for t2 in range(nseq * steps // blk):
            rows = slice(t2 * blk, (t2 + 1) * blk)
            ptr = apr_scr[rows, sl]
            pti = api_scr[rows, sl]
            xbr_scr[rows, sl] = (bur_scr[rows, sl] + ptr * xsr - pti * xsi).astype(bf16)
            xbi_scr[rows, sl] = (bui_scr[rows, sl] + ptr * xsi + pti * xsr).astype(bf16)
    _s5_output(u_ref, xbr_scr[...], xbi_scr[...], cre_ref, cim_ref, d_ref, permt_ref, y_ref)

    @pl.when(sc == nsc - 1)
    def _():
        hr_ref[0] = car_scr[...]
        hi_ref[0] = cai_scr[...]


def _s5_sample_kernel(u_ref, d_ref, bre_ref, bim_ref, cre_ref, cim_ref, ar_ref, ai_ref, perm_ref, permt_ref,
                      h0r_ref, h0i_ref, y_ref, hr_ref, hi_ref, bur_scr, bui_scr, *, nseq, steps):
    _s5_project_in(u_ref, perm_ref, bre_ref, bim_ref, bur_scr, bui_scr)
    fin_r, fin_i, lw = _s5_scan(bur_scr, bui_scr, ar_ref[...], ai_ref[...], h0r_ref[...], h0i_ref[...], nseq, steps)
    for lb in range(len(fin_r)):
        hr_ref[:, lb * lw:(lb + 1) * lw] = fin_r[lb]
        hi_ref[:, lb * lw:(lb + 1) * lw] = fin_i[lb]
    _s5_output(u_ref, bur_scr[...].astype(bf16), bui_scr[...].astype(bf16), cre_ref, cim_ref, d_ref, permt_ref,
               y_ref)


def _s5_blockdiag_in(bb):
    t = bb.reshape(S5_NGT, S5_GT, S5_GROUP, S5_STATE)
    eye = jnp.eye(S5_GT, dtype=bool)
    out = jnp.where(eye[None, :, None, :, None], t[:, :, :, None, :], 0.0)
    return out.reshape(S5_NGT, S5_TU, S5_TL).astype(bf16)


def _s5_blockdiag_out(c):
    t = jnp.transpose(c.reshape(S5_NGT, S5_GT, S5_GROUP, S5_STATE), (0, 1, 3, 2))
    eye = jnp.eye(S5_GT, dtype=bool)
    out = jnp.where(eye[None, :, None, :, None], t[:, :, :, None, :], 0.0)
    return out.reshape(S5_NGT, S5_TL, S5_TU).astype(bf16)


def _s5_perm(nseq, steps):
    n = nseq * steps
    dst = jnp.arange(n)
    src = (dst % nseq) * steps + dst // nseq
    p = src[:, None] == jnp.arange(n)[None, :]
    return jnp.where(p, 1.0, 0.0).astype(bf16), jnp.where(p.T, 1.0, 0.0).astype(bf16)


def _s5_common_specs(rows, row_map):
    lane = lambda *a: (0, a[0])
    mat = lambda *a: (a[0], 0, 0)
    return [pl.BlockSpec((rows, S5_TU), row_map),
            pl.BlockSpec((1, S5_TU), lane),
            pl.BlockSpec((1, S5_TU, S5_TL), mat),
            pl.BlockSpec((1, S5_TU, S5_TL), mat),
            pl.BlockSpec((1, S5_TL, S5_TU), mat),
            pl.BlockSpec((1, S5_TL, S5_TU), mat),
            pl.BlockSpec((1, S5_TL), lane),
            pl.BlockSpec((1, S5_TL), lane)]


def s5_prompt(proj, d, mats, batch, length):
    bre, bim, cre, cim, ar, ai, lr, li = mats
    nseq, steps = 8, CHUNK
    rows = nseq * steps
    nsc = length // rows
    ucol = OFF_U // S5_TU
    kern = functools.partial(_s5_prompt_kernel, nseq=nseq, steps=steps, nsc=nsc)
    lane = lambda g, b, s: (0, g)
    const = lambda g, b, s: (0, 0)
    perm, permt = _s5_perm(nseq, steps)
    st = jax.ShapeDtypeStruct((batch, 1, S5_LANES), f32)
    y, hr, hi = pl.pallas_call(
        kern,
        grid=(S5_NGT, batch, nsc),
        in_specs=_s5_common_specs(rows, lambda g, b, s: (b * nsc + s, ucol + g))
        + [pl.BlockSpec((1, S5_TL), lane), pl.BlockSpec((1, S5_TL), lane),
           pl.BlockSpec((rows, rows), const), pl.BlockSpec((rows, rows), const)],
        out_specs=[pl.BlockSpec((rows, S5_TU), lambda g, b, s: (b * nsc + s, g)),
                   pl.BlockSpec((1, 1, S5_TL), lambda g, b, s: (b, 0, g)),
                   pl.BlockSpec((1, 1, S5_TL), lambda g, b, s: (b, 0, g))],
        out_shape=[jax.ShapeDtypeStruct((batch * length, D_MODEL), bf16), st, st],
        scratch_shapes=[pltpu.VMEM((rows, S5_TL), f32), pltpu.VMEM((rows, S5_TL), f32),
                        pltpu.VMEM((rows, S5_TL), bf16), pltpu.VMEM((rows, S5_TL), bf16),
                        pltpu.VMEM((rows, S5_TL), f32), pltpu.VMEM((rows, S5_TL), f32),
                        pltpu.VMEM((nseq, S5_TL), f32), pltpu.VMEM((nseq, S5_TL), f32),
                        pltpu.VMEM((1, S5_TL), f32), pltpu.VMEM((1, S5_TL), f32)],
        compiler_params=_cparams("arbitrary", "arbitrary", "arbitrary"),
        name="s5_prompt",
    )(proj, d, bre, bim, cre, cim, ar, ai, lr, li, perm, permt)
    return y, hr.reshape(batch, S5_GROUPS, S5_STATE), hi.reshape(batch, S5_GROUPS, S5_STATE)


def s5_sample(proj, d, mats, h0r, h0i, batch, length):
    bre, bim, cre, cim, ar, ai, _, _ = mats
    rows = batch * length
    ucol = OFF_U // S5_TU
    kern = functools.partial(_s5_sample_kernel, nseq=batch, steps=length)
    perm, permt = _s5_perm(batch, length)
    st = jax.ShapeDtypeStruct((batch, S5_LANES), f32)
    state_spec = pl.BlockSpec((batch, S5_TL), lambda g: (0, g))
    perm_spec = pl.BlockSpec((rows, rows), lambda g: (0, 0))
    y, hr, hi = pl.pallas_call(
        kern,
        grid=(S5_NGT,),
        in_specs=_s5_common_specs(rows, lambda g: (0, ucol + g)) + [perm_spec, perm_spec, state_spec, state_spec],
        out_specs=[pl.BlockSpec((rows, S5_TU), lambda g: (0, g)), state_spec, state_spec],
        out_shape=[jax.ShapeDtypeStruct((rows, D_MODEL), bf16), st, st],
        scratch_shapes=[pltpu.VMEM((rows, S5_TL), f32), pltpu.VMEM((rows, S5_TL), f32)],
        compiler_params=_cparams("arbitrary"),
        name="s5_sample",
    )(proj, d, bre, bim, cre, cim, ar, ai, perm, permt, h0r.reshape(batch, S5_LANES), h0i.reshape(batch, S5_LANES))
    return y, hr.reshape(batch, S5_GROUPS, S5_STATE), hi.reshape(batch, S5_GROUPS, S5_STATE)


_SSD_PAD = 8
_SSD_TP = 128


def _split3(v):
    p0 = v.astype(bf16)
    r0 = v - p0.astype(f32)
    p1 = r0.astype(bf16)
    p2 = (r0 - p1.astype(f32)).astype(bf16)
    return p0, p1, p2


def _ssd_conv(c, nchunks, xbc_ref, conv0_ref, cw_ref, cb_ref, convnew_ref, stage, lc):
    keep = SSD_CONV_W - 1
    lo = _SSD_PAD - keep

    @pl.when(c == 0)
    def _():
        if conv0_ref is not None:
            stage[lo:_SSD_PAD, :] = conv0_ref[0]
        else:
            stage[0:_SSD_PAD, :] = jnp.zeros((_SSD_PAD, SSD_CONV_DIM), f32)

    stage[_SSD_PAD:_SSD_PAD + lc, :] = xbc_ref[...]
    xc = cb_ref[...]
    for j in range(SSD_CONV_W):
        xc = xc + cw_ref[j:j + 1, :] * stage[lo + j:lo + j + lc, :]
    xc = xc * _sigmoid(xc)
    tail = stage[lo + lc:_SSD_PAD + lc, :]

    @pl.when(c == nchunks - 1)
    def _():
        convnew_ref[0] = tail

    stage[lo:_SSD_PAD, :] = tail
    return xc


def _ssd_dt_cumsum(dt_ref, dtb_ref, alog_ref, lc):
    dtx = dt_ref[...] + dtb_ref[...]
    dtv = jnp.maximum(dtx, 0.0) + jnp.log1p(jnp.exp(-jnp.abs(dtx)))
    da = dtv * (-jnp.exp(alog_ref[...]))
    ri = lax.broadcasted_iota(jnp.int32, (lc, lc), 0)
    ci = lax.broadcasted_iota(jnp.int32, (lc, lc), 1)
    tri = jnp.where(ri >= ci, 1.0, 0.0).astype(bf16)
    da3 = _split3(da)
    cs = _dot(tri, da3[0]) + _dot(tri, da3[1]) + _dot(tri, da3[2])
    return dtv, cs, da3


def _ssd_parse(refs, has_init, n_alias, n_fixed):
    fixed = refs[:n_fixed]
    conv0_ref, h0_ref = (refs[n_fixed], refs[n_fixed + 1]) if has_init else (None, None)
    rest = refs[n_fixed + 2 * has_init + n_alias:]
    return fixed, conv0_ref, h0_ref, rest


def _ssd_wide_kernel(*refs, nchunks, has_init, n_alias):
    lc = SSD_HEADDIM
    fixed, conv0_ref, h0_ref, rest = _ssd_parse(refs, has_init, n_alias, 10)
    z_ref, xbc_ref, dt_ref, cw_ref, cb_ref, dtb_ref, alog_ref, dexp_ref, ng_ref, rep_ref = fixed
    y_ref, hnew_ref, convnew_ref, stage, ht_scr, xc_scr, cm_scr, dx_scr = rest
    c = pl.program_id(1)

    @pl.when(c == 0)
    def _():
        if has_init:
            for h in range(SSD_HEADS):
                ht_scr[:, h * lc:(h + 1) * lc] = h0_ref[0, h].T
        else:
            ht_scr[...] = jnp.zeros_like(ht_scr)

    xc_scr[...] = _ssd_conv(c, nchunks, xbc_ref, conv0_ref, cw_ref, cb_ref, convnew_ref, stage, lc)
    dtv, cs, da3 = _ssd_dt_cumsum(dt_ref, dtb_ref, alog_ref, lc)
    pw = 2 * SSD_HEADDIM
    upper2 = jnp.where(lax.broadcasted_iota(jnp.int32, (lc, pw), 0) <= lax.broadcasted_iota(jnp.int32, (lc, pw), 1) % lc,
                       1.0, 0.0).astype(bf16)
    cst2 = _dot_tn(da3[0], upper2) + _dot_tn(da3[1], upper2) + _dot_tn(da3[2], upper2)
    low_half = lax.broadcasted_iota(jnp.int32, (1, pw), 1) < lc
    rep = rep_ref[...]
    c0, c1, c2 = _split3(cs)
    cm_scr[...] = _dot(c0, rep) + _dot(c1, rep) + _dot(c2, rep)
    d0, d1, d2 = _split3(dtv)
    dx_scr[...] = _dot(d0, rep) + _dot(d1, rep) + _dot(d2, rep)

    gw = SSD_HPG * SSD_HEADDIM
    ti = lax.broadcasted_iota(jnp.int32, (lc, gw), 0)
    si = lax.broadcasted_iota(jnp.int32, (lc, gw), 1) % lc
    causal = ti >= si
    bdmask = (lax.broadcasted_iota(jnp.int32, (pw, pw), 0) // lc) == (lax.broadcasted_iota(jnp.int32, (pw, pw), 1) // lc)
    b_off, c_off = SSD_INNER, SSD_INNER + SSD_GROUPS * SSD_STATE
    for g in range(SSD_GROUPS):
        gs = slice(g * gw, (g + 1) * gw)
        cm = cm_scr[:, gs]
        row = jnp.concatenate(
            [jnp.where(low_half, cst2[h0:h0 + 1, :], cst2[h0 + 1:h0 + 2, :])
             for h0 in range(g * SSD_HPG, (g + 1) * SSD_HPG, 2)], axis=1)
        decay = jnp.exp(jnp.where(causal, cm - row, -jnp.inf))
        last = cm[lc - 1:lc, :]
        xg = xc_scr[:, gs]
        xdt = xg * dx_scr[:, gs]
        bg = xc_scr[:, b_off + g * SSD_STATE:b_off + (g + 1) * SSD_STATE].astype(bf16)
        cg = xc_scr[:, c_off + g * SSD_STATE:c_off + (g + 1) * SSD_STATE].astype(bf16)
        cb2 = _dot_nt(cg, jnp.concatenate([bg, bg], axis=0))
        parts = []
        for pr in range(gw // pw):
            ps = slice(pr * pw, (pr + 1) * pw)
            m = (cb2 * decay[:, ps]).astype(bf16)
            xp = xdt[:, ps]
            bd = jnp.where(bdmask, jnp.concatenate([xp, xp], axis=0), 0.0).astype(bf16)
            parts.append(_dot(m, bd))
        y = jnp.concatenate(parts, axis=1)
        ht = ht_scr[:, gs]
        y = y + _dot(cg, ht.astype(bf16)) * jnp.exp(cm)
        xw = (xdt * jnp.exp(last - cm)).astype(bf16)
        ht_scr[:, gs] = jnp.exp(last) * ht + _dot_tn(bg, xw)
        y = y + dexp_ref[:, gs] * xg
        zg = z_ref[:, gs]
        yg = y * (zg * _sigmoid(zg))
        yg = yg * lax.rsqrt(jnp.mean(yg * yg, axis=-1, keepdims=True) + EPS) * ng_ref[:, gs]
        y_ref[:, gs] = yg.astype(bf16)

    @pl.when(c == nchunks - 1)
    def _():
        for h in range(SSD_HEADS):
            hnew_ref[0, h] = ht_scr[:, h * lc:(h + 1) * lc].T


def _ssd_kernel(*refs, lc, nchunks, has_init, n_alias):
    fixed, conv0_ref, h0_ref, rest = _ssd_parse(refs, has_init, n_alias, 9)
    z_ref, xbc_ref, dt_ref, cw_ref, cb_ref, dtb_ref, alog_ref, dexp_ref, ng_ref = fixed
    y_ref, hnew_ref, convnew_ref, stage, h_scr, pad_scr, y_scr = rest
    c = pl.program_id(1)

    @pl.when(c == 0)
    def _():
        if has_init:
            h_scr[...] = h0_ref[0]
        else:
            h_scr[...] = jnp.zeros_like(h_scr)

    xc = _ssd_conv(c, nchunks, xbc_ref, conv0_ref, cw_ref, cb_ref, convnew_ref, stage, lc)
    dtv, cs, _ = _ssd_dt_cumsum(dt_ref, dtb_ref, alog_ref, lc)
    causal = lax.broadcasted_iota(jnp.int32, (lc, lc), 0) >= lax.broadcasted_iota(jnp.int32, (lc, lc), 1)
    pad_scr[...] = jnp.zeros_like(pad_scr)
    pad_scr[0:lc, 0:SSD_HEADS] = cs
    cst = pad_scr[...].T
    ecs = jnp.exp(cs)
    cs_last = cs[lc - 1:lc, :]
    wdec = jnp.exp(cs_last - cs)
    elast = jnp.exp(cs_last)
    xs_off, b_off, c_off = 0, SSD_INNER, SSD_INNER + SSD_GROUPS * SSD_STATE
    for g in range(SSD_GROUPS):
        bg = xc[:, b_off + g * SSD_STATE:b_off + (g + 1) * SSD_STATE].astype(bf16)
        cg = xc[:, c_off + g * SSD_STATE:c_off + (g + 1) * SSD_STATE].astype(bf16)
        cb = _dot_nt(cg, bg)
        for r in range(SSD_HPG):
            h = g * SSD_HPG + r
            hs = slice(h * SSD_HEADDIM, (h + 1) * SSD_HEADDIM)
            col = cs[:, h:h + 1]
            row = cst[h:h + 1, 0:lc]
            decay = jnp.exp(jnp.where(causal, col - row, -jnp.inf))
            xh = xc[:, hs]
            xdt = xh * dtv[:, h:h + 1]
            hst = h_scr[h]
            y = _dot((cb * decay).astype(bf16), xdt.astype(bf16))
            y = y + _dot_nt(cg, hst.astype(bf16)) * ecs[:, h:h + 1]
            xw = (xdt * wdec[:, h:h + 1]).astype(bf16)
            h_scr[h] = elast[:, h:h + 1] * hst + _dot_tn(xw, bg)
            y_scr[:, hs] = y + dexp_ref[:, hs] * xh
        gs = slice(g * SSD_HPG * SSD_HEADDIM, (g + 1) * SSD_HPG * SSD_HEADDIM)
        zg = z_ref[:, gs]
        yg = y_scr[:, gs] * (zg * _sigmoid(zg))
        yg = yg * lax.rsqrt(jnp.mean(yg * yg, axis=-1, keepdims=True) + EPS) * ng_ref[:, gs]
        y_ref[:, gs] = yg.astype(bf16)

    @pl.when(c == nchunks - 1)
    def _():
        hnew_ref[0] = h_scr[...]


def ssd(proj, dt_raw, conv_w, conv_b, dt_bias, a_log, d, norm_g, batch, length, *, h_all, conv_all, prev_h,
        prev_conv, layer, depth):
    lc = min(length, CHUNK)
    nc = length // lc
    wide = lc == SSD_HEADDIM
    dexp = jnp.repeat(d, SSD_HEADDIM).reshape(1, SSD_INNER)
    row = lambda b, c: (b * nc + c, 0)
    const2 = lambda b, c: (0, 0)
    h_tail = (SSD_HEADS, SSD_HEADDIM, SSD_STATE)
    c_tail = (SSD_CONV_W - 1, SSD_CONV_DIM)
    has_init = h_all is not None
    has_prev = prev_h is not None
    h_in, h_specs, h_out_spec = _layered_state_specs(layer * batch, h_tail, h_all, None)
    c_in, c_specs, c_out_spec = _layered_state_specs(layer * batch, c_tail, conv_all, None)
    ins = [proj, proj, dt_raw, conv_w, conv_b.reshape(1, -1), dt_bias.reshape(1, -1), a_log.reshape(1, -1), dexp,
           norm_g.reshape(1, -1)]
    specs = [pl.BlockSpec((lc, SSD_INNER), lambda b, c: (b * nc + c, OFF_Z // SSD_INNER)),
             pl.BlockSpec((lc, SSD_CONV_DIM), lambda b, c: (b * nc + c, OFF_XBC // SSD_CONV_DIM)),
             pl.BlockSpec((lc, SSD_HEADS), row),
             pl.BlockSpec((SSD_CONV_W, SSD_CONV_DIM), const2),
             pl.BlockSpec((1, SSD_CONV_DIM), const2),
             pl.BlockSpec((1, SSD_HEADS), const2),
             pl.BlockSpec((1, SSD_HEADS), const2),
             pl.BlockSpec((1, SSD_INNER), const2),
             pl.BlockSpec((1, SSD_INNER), const2)]
    if wide:
        rep = jnp.repeat(jnp.eye(SSD_HEADS, dtype=bf16), SSD_HEADDIM, axis=1)
        ins.append(rep)
        specs.append(pl.BlockSpec((SSD_HEADS, SSD_INNER), const2))
        kern = functools.partial(_ssd_wide_kernel, nchunks=nc, has_init=has_init, n_alias=2 * has_prev)
        scratch = [pltpu.VMEM((_SSD_PAD + lc, SSD_CONV_DIM), f32),
                   pltpu.VMEM((SSD_STATE, SSD_INNER), f32),
                   pltpu.VMEM((lc, SSD_CONV_DIM), f32),
                   pltpu.VMEM((lc, SSD_INNER), f32),
                   pltpu.VMEM((lc, SSD_INNER), f32)]
    else:
        kern = functools.partial(_ssd_kernel, lc=lc, nchunks=nc, has_init=has_init, n_alias=2 * has_prev)
        scratch = [pltpu.VMEM((_SSD_PAD + lc, SSD_CONV_DIM), f32),
                   pltpu.VMEM(h_tail, f32),
                   pltpu.VMEM((_SSD_TP, _SSD_TP), f32),
                   pltpu.VMEM((lc, SSD_INNER), f32)]
    ins += c_in + h_in
    specs += c_specs + h_specs
    aliases = {}
    if has_prev:
        aliases = {len(ins): 1, len(ins) + 1: 2}
        ins += [prev_h, prev_conv]
        specs += [pl.BlockSpec(memory_space=pl.ANY), pl.BlockSpec(memory_space=pl.ANY)]
    return pl.pallas_call(
        kern,
        grid=(batch, nc),
        in_specs=specs,
        out_specs=[pl.BlockSpec((lc, SSD_INNER), row), h_out_spec, c_out_spec],
        out_shape=[jax.ShapeDtypeStruct((batch * length, SSD_INNER), bf16),
                   jax.ShapeDtypeStruct((depth * batch,) + h_tail, f32),
                   jax.ShapeDtypeStruct((depth * batch,) + c_tail, f32)],
        scratch_shapes=scratch,
        input_output_aliases=aliases,
        compiler_params=_cparams("parallel", "arbitrary"),
        name="ssd_wide" if wide else "ssd",
    )(*ins)


def _merge_kernel(ra_ref, sa_ref, da_ref, gr_ref, gs_ref, gd_ref, wr_ref, wv_ref, wg_ref, wd_ref, o_ref):
    sa = sa_ref[...]
    ret = _dot(ra_ref[...], wr_ref[...])
    s5 = _dot(sa, wv_ref[...]) * _sigmoid(_dot(sa, wg_ref[...]))
    sd = _dot(da_ref[...], wd_ref[...])
    merged = _sigmoid(gr_ref[...]) * ret + _sigmoid(gs_ref[...]) * s5 + _sigmoid(gd_ref[...]) * sd
    o_ref[...] = merged.astype(bf16)


def merge(ret_act, s5_act, ssd_act, gates, w_ret_o, w_glu, w_ssd_out, *, tm, tn):
    n = ret_act.shape[0]
    nj = D_MODEL // tn
    act = lambda i, j: (i, 0)
    return pl.pallas_call(
        _merge_kernel,
        grid=(n // tm, nj),
        in_specs=[pl.BlockSpec((tm, R_VAL), act),
                  pl.BlockSpec((tm, D_MODEL), act),
                  pl.BlockSpec((tm, SSD_INNER), act),
                  pl.BlockSpec((tm, tn), lambda i, j: (i, j)),
                  pl.BlockSpec((tm, tn), lambda i, j: (i, nj + j)),
                  pl.BlockSpec((tm, tn), lambda i, j: (i, 2 * nj + j)),
                  pl.BlockSpec((R_VAL, tn), lambda i, j: (0, j)),
                  pl.BlockSpec((D_MODEL, tn), lambda i, j: (0, j)),
                  pl.BlockSpec((D_MODEL, tn), lambda i, j: (0, nj + j)),
                  pl.BlockSpec((SSD_INNER, tn), lambda i, j: (0, j))],
        out_specs=pl.BlockSpec((tm, tn), lambda i, j: (i, j)),
        out_shape=jax.ShapeDtypeStruct((n, D_MODEL), bf16),
        compiler_params=_cparams("parallel", "arbitrary"),
        name="merge",
    )(ret_act, s5_act, ssd_act, gates, gates, gates, w_ret_o, w_glu, w_glu, w_ssd_out)


def _attn_kernel(q_ref, k_ref, v_ref, o_ref):
    scale = X_HEAD_DIM ** -0.5
    for h in range(X_HEADS):
        hs = slice(h * X_HEAD_DIM, (h + 1) * X_HEAD_DIM)
        s = _dot_nt(q_ref[:, hs], k_ref[0, :, hs].astype(bf16)) * scale
        e = jnp.exp(s - jnp.max(s, axis=-1, keepdims=True))
        p = e / jnp.sum(e, axis=-1, keepdims=True)
        o_ref[:, hs] = _dot(p.astype(bf16), v_ref[0, :, hs].astype(bf16)).astype(bf16)


def attention(q, mem_k, mem_v, batch, length, *, tq, kv_base):
    nt = length // tq
    kv = pl.BlockSpec((1, MEM_LEN, D_MODEL), lambda b, i: (kv_base + b, 0, 0))
    return pl.pallas_call(
        _attn_kernel,
        grid=(batch, nt),
        in_specs=[pl.BlockSpec((tq, D_MODEL), lambda b, i: (b * nt + i, 0)), kv, kv],
        out_specs=pl.BlockSpec((tq, D_MODEL), lambda b, i: (b * nt + i, 0)),
        out_shape=jax.ShapeDtypeStruct((batch * length, D_MODEL), bf16),
        compiler_params=_cparams("parallel", "arbitrary"),
        name="attention",
    )(q, mem_k, mem_v)


def _mlp_kernel(x_ref, g_in_ref, g_out_ref, wu_ref, wd_ref, o_ref, hn_scr, acc_scr, *, nff):
    j = pl.program_id(1)

    @pl.when(j == 0)
    def _():
        hn_scr[...] = _rms(x_ref[...], g_in_ref[...]).astype(bf16)
        acc_scr[...] = jnp.zeros_like(acc_scr)

    h = jnp.maximum(_dot(hn_scr[...], wu_ref[...]), 0.0)
    acc_scr[...] += _dot((h * h).astype(bf16), wd_ref[...])

    @pl.when(j == nff - 1)
    def _():
        o_ref[...] = x_ref[...] + _rms(acc_scr[...], g_out_ref[...])


def mlp(x, g_in, g_out, w_up, w_down, *, tm, tf):
    n, d = x.shape
    nff = D_FF // tf
    kern = functools.partial(_mlp_kernel, nff=nff)
    return pl.pallas_call(
        kern,
        grid=(n // tm, nff),
        in_specs=[pl.BlockSpec((tm, d), lambda i, j: (i, 0)),
                  pl.BlockSpec((1, d), lambda i, j: (0, 0)),
                  pl.BlockSpec((1, d), lambda i, j: (0, 0)),
                  pl.BlockSpec((d, tf), lambda i, j: (0, j)),
                  pl.BlockSpec((tf, d), lambda i, j: (j, 0))],
        out_specs=pl.BlockSpec((tm, d), lambda i, j: (i, 0)),
        out_shape=jax.ShapeDtypeStruct((n, d), f32),
        scratch_shapes=[pltpu.VMEM((tm, d), bf16), pltpu.VMEM((tm, d), f32)],
        compiler_params=_cparams("parallel", "arbitrary"),
        name="mlp",
    )(x, g_in, g_out, w_up, w_down)


def _run_layer(x, batch, length, pos0, mem_k, mem_v, kv_base, carried, prev, layer, depth, lw, *, prompt):
    gains = lw["gains"]
    n = x.shape[0]
    tm = min(512, n)
    tm_in = min(1024, n)
    proj = norm_matmul(x, gains[0:1], lw["w_in_all"], tm=tm_in, tn=1024, out_dtype=f32, ncols=MAIN_COLS, layer=layer)
    dt_raw = norm_matmul(x, gains[0:1], lw["w_dt"], tm=tm_in, tn=SSD_HEADS, out_dtype=f32)
    gates = norm_matmul(x, gains[0:1], lw["w_gate"], tm=tm_in, tn=512, out_dtype=f32)
    ret_act, ret_all = retention(proj, lw["ret_gn"], batch, length, pos0, state_all=carried["ret"],
                                 prev_out=prev["ret"], layer=layer, depth=depth)
    if prompt:
        s5_act, s5_r_new, s5_i_new = s5_prompt(proj, lw["s5_d"], lw["s5_mats"], batch, length)
    else:
        s5_act, s5_r_new, s5_i_new = s5_sample(proj, lw["s5_d"], lw["s5_mats"], carried["s5_r"], carried["s5_i"],
                                               batch, length)
    ssd_act, ssd_all, conv_all = ssd(proj, dt_raw, lw["ssd_conv_w"], lw["ssd_conv_b"], lw["ssd_dt_bias"],
                                     lw["ssd_a_log"], lw["ssd_d"], lw["ssd_norm"], batch, length,
                                     h_all=carried["ssd"], conv_all=carried["conv"], prev_h=prev["ssd"],
                                     prev_conv=prev["conv"], layer=layer, depth=depth)
    merged = merge(ret_act, s5_act, ssd_act, gates, lw["w_ret_o"], lw["w_s5_glu"], lw["w_ssd_out"], tm=tm, tn=512)
    x = matmul_norm_res(merged, lw["w_mix_out"], gains[1:2], x, tm=tm)
    q = norm_matmul(x, gains[2:3], lw["w_xq"], tm=tm_in, tn=512, out_dtype=bf16)
    att = attention(q, mem_k, mem_v, batch, length, tq=min(length, 512), kv_base=kv_base)
    x = matmul_norm_res(att, lw["w_xo"], gains[3:4], x, tm=tm)
    x = mlp(x, gains[4:5], gains[5:6], lw["w_up"], lw["w_down"], tm=tm, tf=1024)
    return x, dict(ret=ret_all, ssd=ssd_all, conv=conv_all), (s5_r_new, s5_i_new)


def kernel(x_prompt, x_sample, mem_prompt, state_ret, state_s5_re, state_s5_im, state_ssd, cache_ssd_conv,
           cache_mem_k, cache_mem_v, norm_gains, w_in, ret_gn, w_ret_o, s5_a_re, s5_a_im, s5_b_re, s5_b_im,
           s5_c_re, s5_c_im, s5_d, s5_log_dt, w_s5_glu, ssd_conv_w, ssd_conv_b, ssd_dt_bias, ssd_a_log, ssd_d,
           ssd_norm, w_ssd_out, w_mix_out, w_xq, w_xkv, w_xo, w_up, w_down):
    b, l, _ = x_prompt.shape
    db, dl, _ = x_sample.shape
    depth = w_in.shape[0]
    yp = x_prompt.reshape(b * l, D_MODEL)
    ys = x_sample.reshape(db * dl, D_MODEL)
    mem2d = mem_prompt.reshape(b * MEM_LEN, D_MODEL)
    cache_k = cache_mem_k.reshape(depth * db, MEM_LEN, D_MODEL)
    cache_v = cache_mem_v.reshape(depth * db, MEM_LEN, D_MODEL)
    s_carried = dict(ret=state_ret.reshape(depth * db, R_HEADS, R_DK, R_DV),
                     ssd=state_ssd.reshape(depth * db, SSD_HEADS, SSD_HEADDIM, SSD_STATE),
                     conv=cache_ssd_conv.reshape(depth * db, SSD_CONV_W - 1, SSD_CONV_DIM))
    p_carried = dict(ret=None, ssd=None, conv=None)
    p_prev = dict(ret=None, ssd=None, conv=None)
    s_prev = dict(ret=None, ssd=None, conv=None)
    p_s5, s_s5, p_mk, p_mv = [], [], [], []
    w_in_bf = w_in.astype(bf16)
    for i in range(depth):
        ar, ai, lr, li, bbr, bbi = s5_prep(s5_a_re[i], s5_a_im[i], s5_log_dt[i],
                                           jnp.swapaxes(s5_b_re[i], 1, 2), jnp.swapaxes(s5_b_im[i], 1, 2))
        lane = lambda t: t.reshape(1, S5_LANES)
        lw = dict(
            gains=norm_gains[i],
            w_in_all=w_in_bf,
            w_dt=w_in_bf[i, :, OFF_DT:OFF_GATE],
            w_gate=w_in_bf[i, :, OFF_GATE:],
            ret_gn=ret_gn[i].reshape(1, R_VAL),
            w_ret_o=w_ret_o[i].astype(bf16),
            s5_d=s5_d[i].reshape(1, D_MODEL),
            s5_mats=(_s5_blockdiag_in(bbr), _s5_blockdiag_in(bbi),
                     _s5_blockdiag_out(s5_c_re[i]), _s5_blockdiag_out(s5_c_im[i]),
                     lane(ar), lane(ai), lane(lr), lane(li)),
            w_s5_glu=w_s5_glu[i].astype(bf16),
            ssd_conv_w=ssd_conv_w[i], ssd_conv_b=ssd_conv_b[i], ssd_dt_bias=ssd_dt_bias[i],
            ssd_a_log=ssd_a_log[i], ssd_d=ssd_d[i], ssd_norm=ssd_norm[i],
            w_ssd_out=w_ssd_out[i].astype(bf16),
            w_mix_out=w_mix_out[i].astype(bf16),
            w_xq=w_xq[i].astype(bf16),
            w_xo=w_xo[i].astype(bf16),
            w_up=w_up[i].astype(bf16),
            w_down=w_down[i].astype(bf16),
        )
        kv = norm_matmul(mem2d, norm_gains[i, 6:7], w_xkv[i].astype(bf16), tm=b * MEM_LEN, tn=512, out_dtype=f32)
        mk = kv[:, :D_MODEL].reshape(b, MEM_LEN, D_MODEL)
        mv = kv[:, D_MODEL:].reshape(b, MEM_LEN, D_MODEL)
        p_mk.append(mk.reshape(b, MEM_LEN, X_HEADS, X_HEAD_DIM))
        p_mv.append(mv.reshape(b, MEM_LEN, X_HEADS, X_HEAD_DIM))
        yp, p_prev, s5p = _run_layer(yp, b, l, 0, mk, mv, 0, p_carried, p_prev, i, depth, lw, prompt=True)
        p_s5.append(s5p)
        s_carried_i = dict(s_carried, s5_r=state_s5_re[i], s5_i=state_s5_im[i])
        ys, s_prev, s5s = _run_layer(ys, db, dl, PAST_LEN, cache_k, cache_v, i * db, s_carried_i, s_prev, i, depth,
                                     lw, prompt=False)
        s_s5.append(s5s)
    p_s5_re, p_s5_im = [jnp.stack(t) for t in zip(*p_s5)]
    s_s5_re, s_s5_im = [jnp.stack(t) for t in zip(*s_s5)]
    unstack = lambda t, bsz: t.reshape((depth, bsz) + t.shape[1:])
    return (yp.reshape(b, l, D_MODEL), ys.reshape(db, dl, D_MODEL),
            unstack(p_prev["ret"], b), p_s5_re, p_s5_im, unstack(p_prev["ssd"], b), unstack(p_prev["conv"], b),
            jnp.stack(p_mk), jnp.stack(p_mv),
            unstack(s_prev["ret"], db), s_s5_re, s_s5_im, unstack(s_prev["ssd"], db), unstack(s_prev["conv"], db))
```

```python
import functools
import math

import jax
import jax.numpy as jnp
from jax import lax
from jax.experimental import pallas as pl
from jax.experimental.pallas import tpu as pltpu

f32 = jnp.float32
bf16 = jnp.bfloat16

D_MODEL = 2048
PAST_LEN = 1024
CHUNK = 64
EPS = 1e-6
R_HEADS = 8
R_DK = 128
R_DV = 256
R_QK = R_HEADS * R_DK
R_VAL = R_HEADS * R_DV
ROPE_BASE = 10000.0
S5_GROUP = 16
S5_GROUPS = D_MODEL // S5_GROUP
S5_STATE = 64
S5_LANES = S5_GROUPS * S5_STATE
S5_GT = 16
S5_NGT = S5_GROUPS // S5_GT
S5_TU = S5_GT * S5_GROUP
S5_TL = S5_GT * S5_STATE
SSD_INNER = 2 * D_MODEL
SSD_HEADDIM = 64
SSD_HEADS = SSD_INNER // SSD_HEADDIM
SSD_GROUPS = 8
SSD_HPG = SSD_HEADS // SSD_GROUPS
SSD_STATE = 128
SSD_CONV_W = 4
SSD_CONV_DIM = SSD_INNER + 2 * SSD_GROUPS * SSD_STATE
MEM_LEN = 256
X_HEADS = 4
X_HEAD_DIM = D_MODEL // X_HEADS
D_FF = 4 * D_MODEL
OFF_Q = 0
OFF_K = OFF_Q + R_QK
OFF_V = OFF_K + R_QK
OFF_G = OFF_V + R_VAL
OFF_U = OFF_G + R_VAL
OFF_Z = OFF_U + D_MODEL
OFF_XBC = OFF_Z + SSD_INNER
OFF_DT = OFF_XBC + SSD_CONV_DIM
OFF_GATE = OFF_DT + SSD_HEADS
MAIN_COLS = OFF_DT

VMEM_LIMIT_BYTES = 52 * 1024 * 1024


def _cparams(*sem):
    return pltpu.CompilerParams(dimension_semantics=sem, vmem_limit_bytes=VMEM_LIMIT_BYTES)


def _rms(x, g):
    return x * lax.rsqrt(jnp.mean(x * x, axis=-1, keepdims=True) + EPS) * g


def _dot(a, b):
    return jnp.dot(a, b, preferred_element_type=f32)


def _dot_nt(a, b):
    return lax.dot_general(a, b, (((1,), (1,)), ((), ())), preferred_element_type=f32)


def _dot_tn(a, b):
    return lax.dot_general(a, b, (((0,), (0,)), ((), ())), preferred_element_type=f32)


def _sigmoid(x):
    return 1.0 / (1.0 + jnp.exp(-x))


def _norm_matmul_kernel(x_ref, g_ref, w_ref, o_ref, hn_ref, *, sigmoid):
    @pl.when(pl.program_id(1) == 0)
    def _():
        hn_ref[...] = _rms(x_ref[...], g_ref[...]).astype(bf16)

    y = _dot(hn_ref[...], w_ref[...])
    if sigmoid:
        y = _sigmoid(y)
    o_ref[...] = y.astype(o_ref.dtype)


def norm_matmul(x, g, w, *, tm, tn, out_dtype, sigmoid=False):
    n, k = x.shape
    m = w.shape[1]
    return pl.pallas_call(
        functools.partial(_norm_matmul_kernel, sigmoid=sigmoid),
        grid=(n // tm, m // tn),
        in_specs=[pl.BlockSpec((tm, k), lambda i, j: (i, 0)),
                  pl.BlockSpec((1, k), lambda i, j: (0, 0)),
                  pl.BlockSpec((k, tn), lambda i, j: (0, j))],
        out_specs=pl.BlockSpec((tm, tn), lambda i, j: (i, j)),
        out_shape=jax.ShapeDtypeStruct((n, m), out_dtype),
        scratch_shapes=[pltpu.VMEM((tm, k), bf16)],
        compiler_params=_cparams("parallel", "arbitrary"),
        name="norm_matmul",
    )(x, g, w)


IN_TN = 1024
_J_G = (OFF_G // IN_TN, OFF_U // IN_TN)
_J_Z = (OFF_Z // IN_TN, OFF_XBC // IN_TN)
_J_XBC = OFF_XBC // IN_TN
_CONV_PAD = 8


def _inproj_kernel(*refs, tm, conv, tiles_per_seq):
    if conv:
        x_ref, g_ref, w_ref, cw_ref, cb_ref, o_ref, tails_ref, hn_ref, stage, tail_scr = refs
    else:
        x_ref, g_ref, w_ref, o_ref, hn_ref = refs
    i = pl.program_id(0)
    j = pl.program_id(1)

    @pl.when(j == 0)
    def _():
        hn_ref[...] = _rms(x_ref[...], g_ref[...]).astype(bf16)

    acc = _dot(hn_ref[...], w_ref[...])
    is_gate = ((j >= _J_G[0]) & (j < _J_G[1])) | ((j >= _J_Z[0]) & (j < _J_Z[1]))
    is_plain = jnp.logical_not(is_gate)
    if conv:
        is_plain = is_plain & (j < _J_XBC)

    @pl.when(is_gate)
    def _():
        o_ref[...] = acc * _sigmoid(acc)

    @pl.when(is_plain)
    def _():
        o_ref[...] = acc

    if conv:
        @pl.when(j >= _J_XBC)
        def _():
            jj = j - _J_XBC
            keep = SSD_CONV_W - 1
            lo = _CONV_PAD - keep

            @pl.when(i % tiles_per_seq == 0)
            def _():
                tail_scr[jj] = jnp.zeros((_CONV_PAD, IN_TN), f32)

            stage[0:_CONV_PAD, :] = tail_scr[jj]
            stage[_CONV_PAD:_CONV_PAD + tm, :] = acc
            xc = cb_ref[...]
            for t in range(SSD_CONV_W):
                xc = xc + cw_ref[t:t + 1, :] * stage[lo + t:lo + t + tm, :]
            o_ref[...] = xc * _sigmoid(xc)
            last = stage[tm:tm + _CONV_PAD, :]
            tail_scr[jj] = last
            tails_ref[0] = last


def inproj(x, g, w_all, layer, *, tm, conv_w=None, conv_b=None, seq_len=None):
    n, k = x.shape
    conv = conv_w is not None
    nj = MAIN_COLS // IN_TN
    nxbc = SSD_CONV_DIM // IN_TN
    in_specs = [pl.BlockSpec((tm, k), lambda i, j: (i, 0)),
                pl.BlockSpec((1, k), lambda i, j: (0, 0)),
                pl.BlockSpec((None, k, IN_TN), lambda i, j: (layer, 0, j))]
    out_specs = pl.BlockSpec((tm, IN_TN), lambda i, j: (i, j))
    out_shape = jax.ShapeDtypeStruct((n, MAIN_COLS), f32)
    scratch = [pltpu.VMEM((tm, k), bf16)]
    args = [x, g, w_all]
    tiles_per_seq = None
    if conv:
        xcol = lambda i, j: (0, jnp.maximum(j - _J_XBC, 0))
        in_specs += [pl.BlockSpec((SSD_CONV_W, IN_TN), xcol), pl.BlockSpec((1, IN_TN), xcol)]
        out_specs = [out_specs, pl.BlockSpec((1, _CONV_PAD, IN_TN), lambda i, j: (i, 0, jnp.maximum(j - _J_XBC, 0)))]
        out_shape = [out_shape, jax.ShapeDtypeStruct((n // tm, _CONV_PAD, SSD_CONV_DIM), f32)]
        scratch += [pltpu.VMEM((_CONV_PAD + tm, IN_TN), f32), pltpu.VMEM((nxbc, _CONV_PAD, IN_TN), f32)]
        args += [conv_w, conv_b.reshape(1, -1)]
        tiles_per_seq = seq_len // tm
    kern = functools.partial(_inproj_kernel, tm=tm, conv=conv, tiles_per_seq=tiles_per_seq)
    out = pl.pallas_call(
        kern,
        grid=(n // tm, nj),
        in_specs=in_specs,
        out_specs=out_specs,
        out_shape=out_shape,
        scratch_shapes=scratch,
        compiler_params=_cparams("arbitrary", "arbitrary"),
        name="inproj",
    )(*args)
    return out if conv else (out, None)


def _matmul_norm_res_kernel(a_ref, w_ref, g_ref, x_ref, o_ref):
    y = _dot(a_ref[...], w_ref[...])
    o_ref[...] = x_ref[...] + _rms(y, g_ref[...])


def matmul_norm_res(a, w, g, x, *, tm):
    n, k = a.shape
    d = w.shape[1]
    return pl.pallas_call(
        _matmul_norm_res_kernel,
        grid=(n // tm,),
        in_specs=[pl.BlockSpec((tm, k), lambda i: (i, 0)),
                  pl.BlockSpec((k, d), lambda i: (0, 0)),
                  pl.BlockSpec((1, d), lambda i: (0, 0)),
                  pl.BlockSpec((tm, d), lambda i: (i, 0))],
        out_specs=pl.BlockSpec((tm, d), lambda i: (i, 0)),
        out_shape=jax.ShapeDtypeStruct((n, d), f32),
        compiler_params=_cparams("parallel"),
        name="matmul_norm_res",
    )(a, w, g, x)


def _layered_state_specs(layer_rows, tail, state_all, prev_out):
    zeros = (0,) * len(tail)
    spec = pl.BlockSpec((1,) + tail, lambda b, c: (layer_rows + b,) + zeros)
    ins, specs = [], []
    if state_all is not None:
        ins.append(state_all)
        specs.append(spec)
    if prev_out is not None:
        ins.append(prev_out)
        specs.append(pl.BlockSpec(memory_space=pl.ANY))
    return ins, specs, spec


def _retention_kernel(*refs, nchunks, has_init, n_alias):
    q_ref, k_ref, v_ref, g_ref, cos_ref, sin_ref, intra_ref, din_ref, dup_ref, dall_ref, gn_ref = refs[:11]
    s0_ref = refs[11] if has_init else None
    o_ref, snew_ref, s_scr = refs[11 + has_init + n_alias:]
    c = pl.program_id(1)

    @pl.when(c == 0)
    def _():
        if has_init:
            s_scr[...] = s0_ref[0]
        else:
            s_scr[...] = jnp.zeros_like(s_scr)

    cos = cos_ref[...]
    sin = sin_ref[...]
    for h in range(R_HEADS):
        q = q_ref[:, h * R_DK:(h + 1) * R_DK]
        k = k_ref[:, h * R_DK:(h + 1) * R_DK]
        v = v_ref[:, h * R_DV:(h + 1) * R_DV].astype(bf16)
        qr = (q * cos + pltpu.roll(q, R_DK // 2, 1) * sin)
        kr = (k * cos + pltpu.roll(k, R_DK // 2, 1) * sin) * (R_DK ** -0.5)
        qb = qr.astype(bf16)
        s = s_scr[h]
        scores = _dot_nt(qb, kr.astype(bf16)) * intra_ref[h]
        out = _dot(scores.astype(bf16), v) + _dot(qb, s.astype(bf16)) * din_ref[h]
        s_scr[h] = dall_ref[h] * s + _dot_tn((kr * dup_ref[h]).astype(bf16), v)
        mu = jnp.mean(out, axis=-1, keepdims=True)
        cen = out - mu
        var = jnp.mean(cen * cen, axis=-1, keepdims=True)
        o = cen * lax.rsqrt(var + EPS) * gn_ref[:, h * R_DV:(h + 1) * R_DV]
        o_ref[:, h * R_DV:(h + 1) * R_DV] = (g_ref[:, h * R_DV:(h + 1) * R_DV] * o).astype(bf16)

    @pl.when(c == nchunks - 1)
    def _():
        snew_ref[0] = s_scr[...]


def retention(proj, gn, batch, length, pos0, *, state_all, prev_out, layer, depth):
    lc = min(length, CHUNK)
    nc = length // lc
    half = R_DK // 2
    inv_freq = jnp.exp(-math.log(ROPE_BASE) * jnp.arange(half, dtype=f32) / half)
    pos = (pos0 + jnp.arange(length, dtype=jnp.int32)).astype(f32)
    ang = pos[:, None] * inv_freq[None]
    cos2 = jnp.concatenate([jnp.cos(ang), jnp.cos(ang)], axis=-1)
    sin2 = jnp.concatenate([-jnp.sin(ang), jnp.sin(ang)], axis=-1)
    log_gamma = jnp.log1p(-jnp.exp2(-5.0 - jnp.arange(R_HEADS, dtype=f32)))
    idx = jnp.arange(lc, dtype=f32)
    lg = log_gamma[:, None]
    intra = jnp.exp(jnp.abs(idx[:, None] - idx[None, :])[None] * lg[:, :, None])
    din = jnp.broadcast_to(jnp.exp((idx + 1.0)[None] * lg)[:, :, None], (R_HEADS, lc, R_DV))
    dup = jnp.broadcast_to(jnp.exp((lc - 1.0 - idx)[None] * lg)[:, :, None], (R_HEADS, lc, R_DK))
    dall = jnp.exp(lc * log_gamma)
    tail = (R_HEADS, R_DK, R_DV)
    st_in, st_specs, st_out_spec = _layered_state_specs(layer * batch, tail, state_all, prev_out)
    n_fixed = 11
    kern = functools.partial(_retention_kernel, nchunks=nc, has_init=state_all is not None,
                             n_alias=int(prev_out is not None))
    aliases = {n_fixed + len(st_in) - 1: 1} if prev_out is not None else {}
    return pl.pallas_call(
        kern,
        grid=(batch, nc),
        in_specs=[pl.BlockSpec((lc, R_QK), lambda b, c: (b * nc + c, OFF_Q // R_QK)),
                  pl.BlockSpec((lc, R_QK), lambda b, c: (b * nc + c, OFF_K // R_QK)),
                  pl.BlockSpec((lc, R_VAL), lambda b, c: (b * nc + c, OFF_V // R_VAL)),
                  pl.BlockSpec((lc, R_VAL), lambda b, c: (b * nc + c, OFF_G // R_VAL)),
                  pl.BlockSpec((lc, R_DK), lambda b, c: (c, 0)),
                  pl.BlockSpec((lc, R_DK), lambda b, c: (c, 0)),
                  pl.BlockSpec((R_HEADS, lc, lc), lambda b, c: (0, 0, 0)),
                  pl.BlockSpec((R_HEADS, lc, R_DV), lambda b, c: (0, 0, 0)),
                  pl.BlockSpec((R_HEADS, lc, R_DK), lambda b, c: (0, 0, 0)),
                  pl.BlockSpec(memory_space=pltpu.SMEM),
                  pl.BlockSpec((1, R_VAL), lambda b, c: (0, 0))] + st_specs,
        out_specs=[pl.BlockSpec((lc, R_VAL), lambda b, c: (b * nc + c, 0)), st_out_spec],
        out_shape=[jax.ShapeDtypeStruct((batch * length, R_VAL), bf16),
                   jax.ShapeDtypeStruct((depth * batch,) + tail, f32)],
        scratch_shapes=[pltpu.VMEM(tail, f32)],
        input_output_aliases=aliases,
        compiler_params=_cparams("parallel", "arbitrary"),
        name="retention",
    )(proj, proj, proj, proj, cos2, sin2, intra, din, dup, dall, gn, *st_in)


def _s5_prep_kernel(are_ref, aim_ref, ldt_ref, bre_ref, bim_ref, ar_ref, ai_ref, lr_ref, li_ref, bbr_ref, bbi_ref):
    a_re = are_ref[...]
    a_im = aim_ref[...]
    dt = jnp.exp(ldt_ref[...])
    lr = dt * a_re
    li = dt * a_im
    mag = jnp.exp(lr)
    ar = mag * jnp.cos(li)
    ai = mag * jnp.sin(li)
    nr = ar - 1.0
    den = a_re * a_re + a_im * a_im
    cr = (nr * a_re + ai * a_im) / den
    ci = (ai * a_re - nr * a_im) / den
    ar_ref[...] = ar
    ai_ref[...] = ai
    lr_ref[...] = lr
    li_ref[...] = li
    b_re = bre_ref[...]
    b_im = bim_ref[...]
    bbr_ref[...] = cr[:, None, :] * b_re - ci[:, None, :] * b_im
    bbi_ref[...] = cr[:, None, :] * b_im + ci[:, None, :] * b_re


def s5_prep(a_re, a_im, log_dt, b_re, b_im):
    gp = jax.ShapeDtypeStruct((S5_GROUPS, S5_STATE), f32)
    gb = jax.ShapeDtypeStruct((S5_GROUPS, S5_GROUP, S5_STATE), f32)
    return pl.pallas_call(
        _s5_prep_kernel,
        out_shape=[gp, gp, gp, gp, gb, gb],
        compiler_params=pltpu.CompilerParams(vmem_limit_bytes=VMEM_LIMIT_BYTES),
        name="s5_prep",
    )(a_re, a_im, log_dt.reshape(S5_GROUPS, 1), b_re, b_im)


def _s5_project_in(u_ref, perm_ref, bre_ref, bim_ref, bur_scr, bui_scr):
    ub = u_ref[...].astype(bf16)
    up = _dot(perm_ref[...], ub).astype(bf16)
    bur_scr[...] = _dot(up, bre_ref[0])
    bui_scr[...] = _dot(up, bim_ref[0])


def _s5_scan(bur_scr, bui_scr, ar_row, ai_row, init_r, init_i, nseq, steps):
    lw = min(S5_TL, S5_LANES // nseq)
    fin_r, fin_i = [], []
    for lb in range(S5_TL // lw):
        sl = slice(lb * lw, (lb + 1) * lw)
        ar = jnp.broadcast_to(ar_row[:, sl], (nseq, lw))
        ai = jnp.broadcast_to(ai_row[:, sl], (nseq, lw))
        xr = init_r[:, sl]
        xi = init_i[:, sl]
        for t in range(steps):
            rows = slice(t * nseq, (t + 1) * nseq)
            nxr = ar * xr - ai * xi + bur_scr[rows, sl]
            nxi = ar * xi + ai * xr + bui_scr[rows, sl]
            xr, xi = nxr, nxi
            bur_scr[rows, sl] = xr
            bui_scr[rows, sl] = xi
        fin_r.append(xr)
        fin_i.append(xi)
    return fin_r, fin_i, lw


def _s5_output(u_ref, xr, xi, cre_ref, cim_ref, d_ref, permt_ref, y_ref):
    y = _dot(xr, cre_ref[0]) - _dot(xi, cim_ref[0])
    pt = permt_ref[...]
    y0, y1, y2 = _split3(y)
    y = _dot(pt, y0) + _dot(pt, y1) + _dot(pt, y2)
    y_ref[...] = jax.nn.gelu(y + d_ref[...] * u_ref[...]).astype(bf16)


def _s5_prompt_kernel(u_ref, d_ref, bre_ref, bim_ref, cre_ref, cim_ref, ar_ref, ai_ref, lr_ref, li_ref,
                      perm_ref, permt_ref, y_ref, hr_ref, hi_ref,
                      bur_scr, bui_scr, xbr_scr, xbi_scr, apr_scr, api_scr, xsr_scr, xsi_scr, car_scr, cai_scr,
                      *, nseq, steps, nsc):
    b = pl.program_id(1)
    sc = pl.program_id(2)

    @pl.when((b == 0) & (sc == 0))
    def _():
        tt = (lax.broadcasted_iota(jnp.int32, (steps, S5_TL), 0) + 1).astype(f32)
        mag = jnp.exp(tt * lr_ref[...])
        pwr = mag * jnp.cos(tt * li_ref[...])
        pwi = mag * jnp.sin(tt * li_ref[...])
        for t in range(steps):
            apr_scr[t * nseq:(t + 1) * nseq, :] = jnp.broadcast_to(pwr[t:t + 1, :], (nseq, S5_TL))
            api_scr[t * nseq:(t + 1) * nseq, :] = jnp.broadcast_to(pwi[t:t + 1, :], (nseq, S5_TL))

    @pl.when(sc == 0)
    def _():
        car_scr[...] = jnp.zeros_like(car_scr)
        cai_scr[...] = jnp.zeros_like(cai_scr)

    _s5_project_in(u_ref, perm_ref, bre_ref, bim_ref, bur_scr, bui_scr)
    zero = jnp.zeros((nseq, S5_TL), f32)
    _s5_scan(bur_scr, bui_scr, ar_ref[...], ai_ref[...], zero, zero, nseq, steps)
    pr = apr_scr[(steps - 1) * nseq:(steps - 1) * nseq + 1, :]
    pi = api_scr[(steps - 1) * nseq:(steps - 1) * nseq + 1, :]
    xr = car_scr[...]
    xi = cai_scr[...]
    for r in range(nseq):
        xsr_scr[r:r + 1, :] = xr
        xsi_scr[r:r + 1, :] = xi
        er = bur_scr[(steps - 1) * nseq + r:(steps - 1) * nseq + r + 1, :]
        ei = bui_scr[(steps - 1) * nseq + r:(steps - 1) * nseq + r + 1, :]
        xr, xi = er + pr * xr - pi * xi, ei + pr * xi + pi * xr
    car_scr[...] = xr
    cai_scr[...] = xi
    blk = 2 * nseq
    lq = S5_TL // 4
    for q in range(S5_TL // lq):
        sl = slice(q * lq, (q + 1) * lq)
        xsr = jnp.concatenate([xsr_scr[:, sl], xsr_scr[:, sl]], axis=0)
        xsi = jnp.concatenate([xsi_scr[:, sl], xsi_scr[:, sl]], axis=0)
        for t2 in range(nseq * steps // blk):
            rows = slice(t2 * blk, (t2 + 1) * blk)
            ptr = apr_scr[rows, sl]
            pti = api_scr[rows, sl]
            xbr_scr[rows, sl] = (bur_scr[rows, sl] + ptr * xsr - pti * xsi).astype(bf16)
            xbi_scr[rows, sl] = (bui_scr[rows, sl] + ptr * xsi + pti * xsr).astype(bf16)
    _s5_output(u_ref, xbr_scr[...], xbi_scr[...], cre_ref, cim_ref, d_ref, permt_ref, y_ref)

    @pl.when(sc == nsc - 1)
    def _():
        hr_ref[0] = car_scr[...]
        hi_ref[0] = cai_scr[...]


def _s5_sample_kernel(u_ref, d_ref, bre_ref, bim_ref, cre_ref, cim_ref, ar_ref, ai_ref, perm_ref, permt_ref,
                      h0r_ref, h0i_ref, y_ref, hr_ref, hi_ref, bur_scr, bui_scr, *, nseq, steps):
    _s5_project_in(u_ref, perm_ref, bre_ref, bim_ref, bur_scr, bui_scr)
    fin_r, fin_i, lw = _s5_scan(bur_scr, bui_scr, ar_ref[...], ai_ref[...], h0r_ref[...], h0i_ref[...], nseq, steps)
    for lb in range(len(fin_r)):
        hr_ref[:, lb * lw:(lb + 1) * lw] = fin_r[lb]
        hi_ref[:, lb * lw:(lb + 1) * lw] = fin_i[lb]
    _s5_output(u_ref, bur_scr[...].astype(bf16), bui_scr[...].astype(bf16), cre_ref, cim_ref, d_ref, permt_ref,
               y_ref)


def _s5_blockdiag_in(bb):
    t = bb.reshape(S5_NGT, S5_GT, S5_GROUP, S5_STATE)
    eye = jnp.eye(S5_GT, dtype=bool)
    out = jnp.where(eye[None, :, None, :, None], t[:, :, :, None, :], 0.0)
    return out.reshape(S5_NGT, S5_TU, S5_TL).astype(bf16)


def _s5_blockdiag_out(c):
    t = jnp.transpose(c.reshape(S5_NGT, S5_GT, S5_GROUP, S5_STATE), (0, 1, 3, 2))
    eye = jnp.eye(S5_GT, dtype=bool)
    out = jnp.where(eye[None, :, None, :, None], t[:, :, :, None, :], 0.0)
    return out.reshape(S5_NGT, S5_TL, S5_TU).astype(bf16)


def _s5_perm(nseq, steps):
    n = nseq * steps
    dst = jnp.arange(n)
    src = (dst % nseq) * steps + dst // nseq
    p = src[:, None] == jnp.arange(n)[None, :]
    return jnp.where(p, 1.0, 0.0).astype(bf16), jnp.where(p.T, 1.0, 0.0).astype(bf16)


def _s5_common_specs(rows, row_map):
    lane = lambda *a: (0, a[0])
    mat = lambda *a: (a[0], 0, 0)
    return [pl.BlockSpec((rows, S5_TU), row_map),
            pl.BlockSpec((1, S5_TU), lane),
            pl.BlockSpec((1, S5_TU, S5_TL), mat),
            pl.BlockSpec((1, S5_TU, S5_TL), mat),
            pl.BlockSpec((1, S5_TL, S5_TU), mat),
            pl.BlockSpec((1, S5_TL, S5_TU), mat),
            pl.BlockSpec((1, S5_TL), lane),
            pl.BlockSpec((1, S5_TL), lane)]


def s5_prompt(proj, d, mats, batch, length):
    bre, bim, cre, cim, ar, ai, lr, li = mats
    nseq, steps = 8, CHUNK
    rows = nseq * steps
    nsc = length // rows
    ucol = OFF_U // S5_TU
    kern = functools.partial(_s5_prompt_kernel, nseq=nseq, steps=steps, nsc=nsc)
    lane = lambda g, b, s: (0, g)
    const = lambda g, b, s: (0, 0)
    perm, permt = _s5_perm(nseq, steps)
    st = jax.ShapeDtypeStruct((batch, 1, S5_LANES), f32)
    y, hr, hi = pl.pallas_call(
        kern,
        grid=(S5_NGT, batch, nsc),
        in_specs=_s5_common_specs(rows, lambda g, b, s: (b * nsc + s, ucol + g))
        + [pl.BlockSpec((1, S5_TL), lane), pl.BlockSpec((1, S5_TL), lane),
           pl.BlockSpec((rows, rows), const), pl.BlockSpec((rows, rows), const)],
        out_specs=[pl.BlockSpec((rows, S5_TU), lambda g, b, s: (b * nsc + s, g)),
                   pl.BlockSpec((1, 1, S5_TL), lambda g, b, s: (b, 0, g)),
                   pl.BlockSpec((1, 1, S5_TL), lambda g, b, s: (b, 0, g))],
        out_shape=[jax.ShapeDtypeStruct((batch * length, D_MODEL), bf16), st, st],
        scratch_shapes=[pltpu.VMEM((rows, S5_TL), f32), pltpu.VMEM((rows, S5_TL), f32),
                        pltpu.VMEM((rows, S5_TL), bf16), pltpu.VMEM((rows, S5_TL), bf16),
                        pltpu.VMEM((rows, S5_TL), f32), pltpu.VMEM((rows, S5_TL), f32),
                        pltpu.VMEM((nseq, S5_TL), f32), pltpu.VMEM((nseq, S5_TL), f32),
                        pltpu.VMEM((1, S5_TL), f32), pltpu.VMEM((1, S5_TL), f32)],
        compiler_params=_cparams("arbitrary", "arbitrary", "arbitrary"),
        name="s5_prompt",
    )(proj, d, bre, bim, cre, cim, ar, ai, lr, li, perm, permt)
    return y, hr.reshape(batch, S5_GROUPS, S5_STATE), hi.reshape(batch, S5_GROUPS, S5_STATE)


def s5_sample(proj, d, mats, h0r, h0i, batch, length):
    bre, bim, cre, cim, ar, ai, _, _ = mats
    rows = batch * length
    ucol = OFF_U // S5_TU
    kern = functools.partial(_s5_sample_kernel, nseq=batch, steps=length)
    perm, permt = _s5_perm(batch, length)
    st = jax.ShapeDtypeStruct((batch, S5_LANES), f32)
    state_spec = pl.BlockSpec((batch, S5_TL), lambda g: (0, g))
    perm_spec = pl.BlockSpec((rows, rows), lambda g: (0, 0))
    y, hr, hi = pl.pallas_call(
        kern,
        grid=(S5_NGT,),
        in_specs=_s5_common_specs(rows, lambda g: (0, ucol + g)) + [perm_spec, perm_spec, state_spec, state_spec],
        out_specs=[pl.BlockSpec((rows, S5_TU), lambda g: (0, g)), state_spec, state_spec],
        out_shape=[jax.ShapeDtypeStruct((rows, D_MODEL), bf16), st, st],
        scratch_shapes=[pltpu.VMEM((rows, S5_TL), f32), pltpu.VMEM((rows, S5_TL), f32)],
        compiler_params=_cparams("arbitrary"),
        name="s5_sample",
    )(proj, d, bre, bim, cre, cim, ar, ai, perm, permt, h0r.reshape(batch, S5_LANES), h0i.reshape(batch, S5_LANES))
    return y, hr.reshape(batch, S5_GROUPS, S5_STATE), hi.reshape(batch, S5_GROUPS, S5_STATE)


_SSD_PAD = 8
_SSD_TP = 128


def _split3(v):
    p0 = v.astype(bf16)
    r0 = v - p0.astype(f32)
    p1 = r0.astype(bf16)
    p2 = (r0 - p1.astype(f32)).astype(bf16)
    return p0, p1, p2


def _ssd_conv(c, nchunks, xbc_ref, conv0_ref, cw_ref, cb_ref, convnew_ref, stage, lc):
    keep = SSD_CONV_W - 1
    lo = _SSD_PAD - keep

    @pl.when(c == 0)
    def _():
        if conv0_ref is not None:
            stage[lo:_SSD_PAD, :] = conv0_ref[0]
        else:
            stage[0:_SSD_PAD, :] = jnp.zeros((_SSD_PAD, SSD_CONV_DIM), f32)

    stage[_SSD_PAD:_SSD_PAD + lc, :] = xbc_ref[...]
    xc = cb_ref[...]
    for j in range(SSD_CONV_W):
        xc = xc + cw_ref[j:j + 1, :] * stage[lo + j:lo + j + lc, :]
    xc = xc * _sigmoid(xc)
    tail = stage[lo + lc:_SSD_PAD + lc, :]

    @pl.when(c == nchunks - 1)
    def _():
        convnew_ref[0] = tail

    stage[lo:_SSD_PAD, :] = tail
    return xc


def _ssd_dt_cumsum(dt_ref, dtb_ref, alog_ref, lc):
    dtx = dt_ref[...] + dtb_ref[...]
    dtv = jnp.maximum(dtx, 0.0) + jnp.log1p(jnp.exp(-jnp.abs(dtx)))
    da = dtv * (-jnp.exp(alog_ref[...]))
    ri = lax.broadcasted_iota(jnp.int32, (lc, lc), 0)
    ci = lax.broadcasted_iota(jnp.int32, (lc, lc), 1)
    tri = jnp.where(ri >= ci, 1.0, 0.0).astype(bf16)
    da3 = _split3(da)
    cs = _dot(tri, da3[0]) + _dot(tri, da3[1]) + _dot(tri, da3[2])
    return dtv, cs, da3


def _ssd_parse(refs, has_init, n_alias, n_fixed):
    fixed = refs[:n_fixed]
    conv0_ref, h0_ref = (refs[n_fixed], refs[n_fixed + 1]) if has_init else (None, None)
    rest = refs[n_fixed + 2 * has_init + n_alias:]
    return fixed, conv0_ref, h0_ref, rest


def _ssd_wide_kernel(*refs, nchunks, has_init, n_alias):
    lc = SSD_HEADDIM
    z_ref, xc_ref, dt_ref, dtb_ref, alog_ref, dexp_ref, ng_ref, rep_ref = refs[:8]
    h0_ref = refs[8] if has_init else None
    y_ref, hnew_ref, ht_scr, cm_scr, dx_scr = refs[8 + has_init + n_alias:]
    c = pl.program_id(1)

    @pl.when(c == 0)
    def _():
        if has_init:
            for h in range(SSD_HEADS):
                ht_scr[:, h * lc:(h + 1) * lc] = h0_ref[0, h].T
        else:
            ht_scr[...] = jnp.zeros_like(ht_scr)

    dtv, cs, da3 = _ssd_dt_cumsum(dt_ref, dtb_ref, alog_ref, lc)
    pw = 2 * SSD_HEADDIM
    upper2 = jnp.where(lax.broadcasted_iota(jnp.int32, (lc, pw), 0) <= lax.broadcasted_iota(jnp.int32, (lc, pw), 1) % lc,
                       1.0, 0.0).astype(bf16)
    cst2 = _dot_tn(da3[0], upper2) + _dot_tn(da3[1], upper2) + _dot_tn(da3[2], upper2)
    low_half = lax.broadcasted_iota(jnp.int32, (1, pw), 1) < lc
    rep = rep_ref[...]
    c0, c1, c2 = _split3(cs)
    cm_scr[...] = _dot(c0, rep) + _dot(c1, rep) + _dot(c2, rep)
    d0, d1, d2 = _split3(dtv)
    dx_scr[...] = _dot(d0, rep) + _dot(d1, rep) + _dot(d2, rep)

    gw = SSD_HPG * SSD_HEADDIM
    ti = lax.broadcasted_iota(jnp.int32, (lc, gw), 0)
    si = lax.broadcasted_iota(jnp.int32, (lc, gw), 1) % lc
    causal = ti >= si
    bdmask = (lax.broadcasted_iota(jnp.int32, (pw, pw), 0) // lc) == (lax.broadcasted_iota(jnp.int32, (pw, pw), 1) // lc)
    b_off, c_off = SSD_INNER, SSD_INNER + SSD_GROUPS * SSD_STATE
    for g in range(SSD_GROUPS):
        gs = slice(g * gw, (g + 1) * gw)
        cm = cm_scr[:, gs]
        row = jnp.concatenate(
            [jnp.where(low_half, cst2[h0:h0 + 1, :], cst2[h0 + 1:h0 + 2, :])
             for h0 in range(g * SSD_HPG, (g + 1) * SSD_HPG, 2)], axis=1)
        decay = jnp.exp(jnp.where(causal, cm - row, -jnp.inf))
        last = cm[lc - 1:lc, :]
        xg = xc_ref[:, gs]
        xdt = xg * dx_scr[:, gs]
        bg = xc_ref[:, b_off + g * SSD_STATE:b_off + (g + 1) * SSD_STATE].astype(bf16)
        cg = xc_ref[:, c_off + g * SSD_STATE:c_off + (g + 1) * SSD_STATE].astype(bf16)
        cb2 = _dot_nt(cg, jnp.concatenate([bg, bg], axis=0))
        parts = []
        for pr in range(gw // pw):
            ps = slice(pr * pw, (pr + 1) * pw)
            m = (cb2 * decay[:, ps]).astype(bf16)
            xp = xdt[:, ps]
            bd = jnp.where(bdmask, jnp.concatenate([xp, xp], axis=0), 0.0).astype(bf16)
            parts.append(_dot(m, bd))
        y = jnp.concatenate(parts, axis=1)
        ht = ht_scr[:, gs]
        y = y + _dot(cg, ht.astype(bf16)) * jnp.exp(cm)
        xw = (xdt * jnp.exp(last - cm)).astype(bf16)
        ht_scr[:, gs] = jnp.exp(last) * ht + _dot_tn(bg, xw)
        y = y + dexp_ref[:, gs] * xg
        yg = y * z_ref[:, gs]
        yg = yg * lax.rsqrt(jnp.mean(yg * yg, axis=-1, keepdims=True) + EPS) * ng_ref[:, gs]
        y_ref[:, gs] = yg.astype(bf16)

    @pl.when(c == nchunks - 1)
    def _():
        for h in range(SSD_HEADS):
            hnew_ref[0, h] = ht_scr[:, h * lc:(h + 1) * lc].T


def _ssd_kernel(*refs, lc, nchunks, has_init, n_alias):
    fixed, conv0_ref, h0_ref, rest = _ssd_parse(refs, has_init, n_alias, 9)
    z_ref, xbc_ref, dt_ref, cw_ref, cb_ref, dtb_ref, alog_ref, dexp_ref, ng_ref = fixed
    y_ref, hnew_ref, convnew_ref, stage, h_scr, pad_scr, y_scr = rest
    c = pl.program_id(1)

    @pl.when(c == 0)
    def _():
        if has_init:
            h_scr[...] = h0_ref[0]
        else:
            h_scr[...] = jnp.zeros_like(h_scr)

    xc = _ssd_conv(c, nchunks, xbc_ref, conv0_ref, cw_ref, cb_ref, convnew_ref, stage, lc)
    dtv, cs, _ = _ssd_dt_cumsum(dt_ref, dtb_ref, alog_ref, lc)
    causal = lax.broadcasted_iota(jnp.int32, (lc, lc), 0) >= lax.broadcasted_iota(jnp.int32, (lc, lc), 1)
    pad_scr[...] = jnp.zeros_like(pad_scr)
    pad_scr[0:lc, 0:SSD_HEADS] = cs
    cst = pad_scr[...].T
    ecs = jnp.exp(cs)
    cs_last = cs[lc - 1:lc, :]
    wdec = jnp.exp(cs_last - cs)
    elast = jnp.exp(cs_last)
    xs_off, b_off, c_off = 0, SSD_INNER, SSD_INNER + SSD_GROUPS * SSD_STATE
    for g in range(SSD_GROUPS):
        bg = xc[:, b_off + g * SSD_STATE:b_off + (g + 1) * SSD_STATE].astype(bf16)
        cg = xc[:, c_off + g * SSD_STATE:c_off + (g + 1) * SSD_STATE].astype(bf16)
        cb = _dot_nt(cg, bg)
        for r in range(SSD_HPG):
            h = g * SSD_HPG + r
            hs = slice(h * SSD_HEADDIM, (h + 1) * SSD_HEADDIM)
            col = cs[:, h:h + 1]
            row = cst[h:h + 1, 0:lc]
            decay = jnp.exp(jnp.where(causal, col - row, -jnp.inf))
            xh = xc[:, hs]
            xdt = xh * dtv[:, h:h + 1]
            hst = h_scr[h]
            y = _dot((cb * decay).astype(bf16), xdt.astype(bf16))
            y = y + _dot_nt(cg, hst.astype(bf16)) * ecs[:, h:h + 1]
            xw = (xdt * wdec[:, h:h + 1]).astype(bf16)
            h_scr[h] = elast[:, h:h + 1] * hst + _dot_tn(xw, bg)
            y_scr[:, hs] = y + dexp_ref[:, hs] * xh
        gs = slice(g * SSD_HPG * SSD_HEADDIM, (g + 1) * SSD_HPG * SSD_HEADDIM)
        yg = y_scr[:, gs] * z_ref[:, gs]
        yg = yg * lax.rsqrt(jnp.mean(yg * yg, axis=-1, keepdims=True) + EPS) * ng_ref[:, gs]
        y_ref[:, gs] = yg.astype(bf16)

    @pl.when(c == nchunks - 1)
    def _():
        hnew_ref[0] = h_scr[...]


def ssd(proj, dt_raw, conv_w, conv_b, dt_bias, a_log, d, norm_g, batch, length, *, h_all, conv_all, prev_h,
        prev_conv, layer, depth):
    lc = min(length, CHUNK)
    nc = length // lc
    dexp = jnp.repeat(d, SSD_HEADDIM).reshape(1, SSD_INNER)
    row = lambda b, c: (b * nc + c, 0)
    const2 = lambda b, c: (0, 0)
    h_tail = (SSD_HEADS, SSD_HEADDIM, SSD_STATE)
    c_tail = (SSD_CONV_W - 1, SSD_CONV_DIM)
    has_init = h_all is not None
    has_prev = prev_h is not None
    h_in, h_specs, h_out_spec = _layered_state_specs(layer * batch, h_tail, h_all, None)
    c_in, c_specs, c_out_spec = _layered_state_specs(layer * batch, c_tail, conv_all, None)
    ins = [proj, proj, dt_raw, conv_w, conv_b.reshape(1, -1), dt_bias.reshape(1, -1), a_log.reshape(1, -1), dexp,
           norm_g.reshape(1, -1)]
    specs = [pl.BlockSpec((lc, SSD_INNER), lambda b, c: (b * nc + c, OFF_Z // SSD_INNER)),
             pl.BlockSpec((lc, SSD_CONV_DIM), lambda b, c: (b * nc + c, OFF_XBC // SSD_CONV_DIM)),
             pl.BlockSpec((lc, SSD_HEADS), row),
             pl.BlockSpec((SSD_CONV_W, SSD_CONV_DIM), const2),
             pl.BlockSpec((1, SSD_CONV_DIM), const2),
             pl.BlockSpec((1, SSD_HEADS), const2),
             pl.BlockSpec((1, SSD_HEADS), const2),
             pl.BlockSpec((1, SSD_INNER), const2),
             pl.BlockSpec((1, SSD_INNER), const2)]
    kern = functools.partial(_ssd_kernel, lc=lc, nchunks=nc, has_init=has_init, n_alias=2 * has_prev)
    scratch = [pltpu.VMEM((_SSD_PAD + lc, SSD_CONV_DIM), f32),
               pltpu.VMEM(h_tail, f32),
               pltpu.VMEM((_SSD_TP, _SSD_TP), f32),
               pltpu.VMEM((lc, SSD_INNER), f32)]
    ins += c_in + h_in
    specs += c_specs + h_specs
    aliases = {}
    if has_prev:
        aliases = {len(ins): 1, len(ins) + 1: 2}
        ins += [prev_h, prev_conv]
        specs += [pl.BlockSpec(memory_space=pl.ANY), pl.BlockSpec(memory_space=pl.ANY)]
    return pl.pallas_call(
        kern,
        grid=(batch, nc),
        in_specs=specs,
        out_specs=[pl.BlockSpec((lc, SSD_INNER), row), h_out_spec, c_out_spec],
        out_shape=[jax.ShapeDtypeStruct((batch * length, SSD_INNER), bf16),
                   jax.ShapeDtypeStruct((depth * batch,) + h_tail, f32),
                   jax.ShapeDtypeStruct((depth * batch,) + c_tail, f32)],
        scratch_shapes=scratch,
        input_output_aliases=aliases,
        compiler_params=_cparams("parallel", "arbitrary"),
        name="ssd",
    )(*ins)


def ssd_wide(proj, dt_raw, dt_bias, a_log, d, norm_g, batch, length, *, h_all, prev_h, layer, depth):
    lc = SSD_HEADDIM
    nc = length // lc
    dexp = jnp.repeat(d, SSD_HEADDIM).reshape(1, SSD_INNER)
    rep = jnp.repeat(jnp.eye(SSD_HEADS, dtype=bf16), SSD_HEADDIM, axis=1)
    row = lambda b, c: (b * nc + c, 0)
    const2 = lambda b, c: (0, 0)
    h_tail = (SSD_HEADS, SSD_HEADDIM, SSD_STATE)
    h_in, h_specs, h_out_spec = _layered_state_specs(layer * batch, h_tail, h_all, prev_h)
    ins = [proj, proj, dt_raw, dt_bias.reshape(1, -1), a_log.reshape(1, -1), dexp, norm_g.reshape(1, -1), rep]
    specs = [pl.BlockSpec((lc, SSD_INNER), lambda b, c: (b * nc + c, OFF_Z // SSD_INNER)),
             pl.BlockSpec((lc, SSD_CONV_DIM), lambda b, c: (b * nc + c, OFF_XBC // SSD_CONV_DIM)),
             pl.BlockSpec((lc, SSD_HEADS), row),
             pl.BlockSpec((1, SSD_HEADS), const2),
             pl.BlockSpec((1, SSD_HEADS), const2),
             pl.BlockSpec((1, SSD_INNER), const2),
             pl.BlockSpec((1, SSD_INNER), const2),
             pl.BlockSpec((SSD_HEADS, SSD_INNER), const2)]
    aliases = {len(ins) + len(h_in) - 1: 1} if prev_h is not None else {}
    kern = functools.partial(_ssd_wide_kernel, nchunks=nc, has_init=h_all is not None,
                             n_alias=int(prev_h is not None))
    return pl.pallas_call(
        kern,
        grid=(batch, nc),
        in_specs=specs + h_specs,
        out_specs=[pl.BlockSpec((lc, SSD_INNER), row), h_out_spec],
        out_shape=[jax.ShapeDtypeStruct((batch * length, SSD_INNER), bf16),
                   jax.ShapeDtypeStruct((depth * batch,) + h_tail, f32)],
        scratch_shapes=[pltpu.VMEM((SSD_STATE, SSD_INNER), f32),
                        pltpu.VMEM((lc, SSD_INNER), f32),
                        pltpu.VMEM((lc, SSD_INNER), f32)],
        input_output_aliases=aliases,
        compiler_params=_cparams("parallel", "arbitrary"),
        name="ssd_wide",
    )(*ins, *h_in)


def _merge_kernel(ra_ref, sa_ref, da_ref, gr_ref, gs_ref, gd_ref, wr_ref, wv_ref, wg_ref, wd_ref, o_ref):
    sa = sa_ref[...]
    ret = _dot(ra_ref[...], wr_ref[...])
    s5 = _dot(sa, wv_ref[...]) * _sigmoid(_dot(sa, wg_ref[...]))
    sd = _dot(da_ref[...], wd_ref[...])
    merged = gr_ref[...] * ret + gs_ref[...] * s5 + gd_ref[...] * sd
    o_ref[...] = merged.astype(bf16)


def merge(ret_act, s5_act, ssd_act, gates, w_ret_o, w_glu, w_ssd_out, *, tm, tn):
    n = ret_act.shape[0]
    nj = D_MODEL // tn
    act = lambda i, j: (i, 0)
    return pl.pallas_call(
        _merge_kernel,
        grid=(n // tm, nj),
        in_specs=[pl.BlockSpec((tm, R_VAL), act),
                  pl.BlockSpec((tm, D_MODEL), act),
                  pl.BlockSpec((tm, SSD_INNER), act),
                  pl.BlockSpec((tm, tn), lambda i, j: (i, j)),
                  pl.BlockSpec((tm, tn), lambda i, j: (i, nj + j)),
                  pl.BlockSpec((tm, tn), lambda i, j: (i, 2 * nj + j)),
                  pl.BlockSpec((R_VAL, tn), lambda i, j: (0, j)),
                  pl.BlockSpec((D_MODEL, tn), lambda i, j: (0, j)),
                  pl.BlockSpec((D_MODEL, tn), lambda i, j: (0, nj + j)),
                  pl.BlockSpec((SSD_INNER, tn), lambda i, j: (0, j))],
        out_specs=pl.BlockSpec((tm, tn), lambda i, j: (i, j)),
        out_shape=jax.ShapeDtypeStruct((n, D_MODEL), bf16),
        compiler_params=_cparams("parallel", "arbitrary"),
        name="merge",
    )(ret_act, s5_act, ssd_act, gates, gates, gates, w_ret_o, w_glu, w_glu, w_ssd_out)


def _attn_kernel(q_ref, k_ref, v_ref, o_ref):
    scale = X_HEAD_DIM ** -0.5
    for h in range(X_HEADS):
        hs = slice(h * X_HEAD_DIM, (h + 1) * X_HEAD_DIM)
        s = _dot_nt(q_ref[:, hs], k_ref[0, :, hs].astype(bf16)) * scale
        e = jnp.exp(s - jnp.max(s, axis=-1, keepdims=True))
        p = e / jnp.sum(e, axis=-1, keepdims=True)
        o_ref[:, hs] = _dot(p.astype(bf16), v_ref[0, :, hs].astype(bf16)).astype(bf16)


def attention(q, mem_k, mem_v, batch, length, *, tq, kv_base):
    nt = length // tq
    kv = pl.BlockSpec((1, MEM_LEN, D_MODEL), lambda b, i: (kv_base + b, 0, 0))
    return pl.pallas_call(
        _attn_kernel,
        grid=(batch, nt),
        in_specs=[pl.BlockSpec((tq, D_MODEL), lambda b, i: (b * nt + i, 0)), kv, kv],
        out_specs=pl.BlockSpec((tq, D_MODEL), lambda b, i: (b * nt + i, 0)),
        out_shape=jax.ShapeDtypeStruct((batch * length, D_MODEL), bf16),
        compiler_params=_cparams("parallel", "arbitrary"),
        name="attention",
    )(q, mem_k, mem_v)


def _mlp_kernel(x_ref, g_in_ref, g_out_ref, wu_ref, wd_ref, o_ref, hn_scr, acc_scr, *, nff):
    j = pl.program_id(1)

    @pl.when(j == 0)
    def _():
        hn_scr[...] = _rms(x_ref[...], g_in_ref[...]).astype(bf16)
        acc_scr[...] = jnp.zeros_like(acc_scr)

    h = jnp.maximum(_dot(hn_scr[...], wu_ref[...]), 0.0)
    acc_scr[...] += _dot((h * h).astype(bf16), wd_ref[...])

    @pl.when(j == nff - 1)
    def _():
        o_ref[...] = x_ref[...] + _rms(acc_scr[...], g_out_ref[...])


def mlp(x, g_in, g_out, w_up, w_down, *, tm, tf):
    n, d = x.shape
    nff = D_FF // tf
    kern = functools.partial(_mlp_kernel, nff=nff)
    return pl.pallas_call(
        kern,
        grid=(n // tm, nff),
        in_specs=[pl.BlockSpec((tm, d), lambda i, j: (i, 0)),
                  pl.BlockSpec((1, d), lambda i, j: (0, 0)),
                  pl.BlockSpec((1, d), lambda i, j: (0, 0)),
                  pl.BlockSpec((d, tf), lambda i, j: (0, j)),
                  pl.BlockSpec((tf, d), lambda i, j: (j, 0))],
        out_specs=pl.BlockSpec((tm, d), lambda i, j: (i, 0)),
        out_shape=jax.ShapeDtypeStruct((n, d), f32),
        scratch_shapes=[pltpu.VMEM((tm, d), bf16), pltpu.VMEM((tm, d), f32)],
        compiler_params=_cparams("parallel", "arbitrary"),
        name="mlp",
    )(x, g_in, g_out, w_up, w_down)


def _run_layer(x, batch, length, pos0, mem_k, mem_v, kv_base, carried, prev, layer, depth, lw, *, prompt):
    gains = lw["gains"]
    n = x.shape[0]
    tm = min(512, n)
    tm_in = min(1024, n)
    if prompt:
        proj, tails = inproj(x, gains[0:1], lw["w_in_all"], layer, tm=tm_in, conv_w=lw["ssd_conv_w"],
                             conv_b=lw["ssd_conv_b"], seq_len=length)
    else:
        proj, tails = inproj(x, gains[0:1], lw["w_in_all"], layer, tm=tm_in)
    dt_raw = norm_matmul(x, gains[0:1], lw["w_dt"], tm=tm_in, tn=SSD_HEADS, out_dtype=f32)
    gates = norm_matmul(x, gains[0:1], lw["w_gate"], tm=tm_in, tn=1024, out_dtype=f32, sigmoid=True)
    ret_act, ret_all = retention(proj, lw["ret_gn"], batch, length, pos0, state_all=carried["ret"],
                                 prev_out=prev["ret"], layer=layer, depth=depth)
    if prompt:
        s5_act, s5_r_new, s5_i_new = s5_prompt(proj, lw["s5_d"], lw["s5_mats"], batch, length)
    else:
        s5_act, s5_r_new, s5_i_new = s5_sample(proj, lw["s5_d"], lw["s5_mats"], carried["s5_r"], carried["s5_i"],
                                               batch, length)
    if prompt:
        ssd_act, ssd_all = ssd_wide(proj, dt_raw, lw["ssd_dt_bias"], lw["ssd_a_log"], lw["ssd_d"], lw["ssd_norm"],
                                    batch, length, h_all=carried["ssd"], prev_h=prev["ssd"], layer=layer, depth=depth)
        tiles_per_seq = length // tm_in
        seq_tails = tails.reshape(batch, tiles_per_seq, _CONV_PAD, SSD_CONV_DIM)[:, -1, _CONV_PAD - (SSD_CONV_W - 1):]
        conv_all = seq_tails if prev["conv"] is None else jnp.concatenate([prev["conv"], seq_tails], axis=0)
    else:
        ssd_act, ssd_all, conv_all = ssd(proj, dt_raw, lw["ssd_conv_w"], lw["ssd_conv_b"], lw["ssd_dt_bias"],
                                         lw["ssd_a_log"], lw["ssd_d"], lw["ssd_norm"], batch, length,
                                         h_all=carried["ssd"], conv_all=carried["conv"], prev_h=prev["ssd"],
                                         prev_conv=prev["conv"], layer=layer, depth=depth)
    merged = merge(ret_act, s5_act, ssd_act, gates, lw["w_ret_o"], lw["w_s5_glu"], lw["w_ssd_out"], tm=tm, tn=512)
    x = matmul_norm_res(merged, lw["w_mix_out"], gains[1:2], x, tm=tm)
    q = norm_matmul(x, gains[2:3], lw["w_xq"], tm=tm_in, tn=512, out_dtype=bf16)
    att = attention(q, mem_k, mem_v, batch, length, tq=min(length, 512), kv_base=kv_base)
    x = matmul_norm_res(att, lw["w_xo"], gains[3:4], x, tm=tm)
    x = mlp(x, gains[4:5], gains[5:6], lw["w_up"], lw["w_down"], tm=tm, tf=1024)
    return x, dict(ret=ret_all, ssd=ssd_all, conv=conv_all), (s5_r_new, s5_i_new)


def kernel(x_prompt, x_sample, mem_prompt, state_ret, state_s5_re, state_s5_im, state_ssd, cache_ssd_conv,
           cache_mem_k, cache_mem_v, norm_gains, w_in, ret_gn, w_ret_o, s5_a_re, s5_a_im, s5_b_re, s5_b_im,
           s5_c_re, s5_c_im, s5_d, s5_log_dt, w_s5_glu, ssd_conv_w, ssd_conv_b, ssd_dt_bias, ssd_a_log, ssd_d,
           ssd_norm, w_ssd_out, w_mix_out, w_xq, w_xkv, w_xo, w_up, w_down):
    b, l, _ = x_prompt.shape
    db, dl, _ = x_sample.shape
    depth = w_in.shape[0]
    yp = x_prompt.reshape(b * l, D_MODEL)
    ys = x_sample.reshape(db * dl, D_MODEL)
    mem2d = mem_prompt.reshape(b * MEM_LEN, D_MODEL)
    cache_k = cache_mem_k.reshape(depth * db, MEM_LEN, D_MODEL)
    cache_v = cache_mem_v.reshape(depth * db, MEM_LEN, D_MODEL)
    s_carried = dict(ret=state_ret.reshape(depth * db, R_HEADS, R_DK, R_DV),
                     ssd=state_ssd.reshape(depth * db, SSD_HEADS, SSD_HEADDIM, SSD_STATE),
                     conv=cache_ssd_conv.reshape(depth * db, SSD_CONV_W - 1, SSD_CONV_DIM))
    p_carried = dict(ret=None, ssd=None, conv=None)
    p_prev = dict(ret=None, ssd=None, conv=None)
    s_prev = dict(ret=None, ssd=None, conv=None)
    p_s5, s_s5, p_mk, p_mv = [], [], [], []
    w_in_bf = w_in.astype(bf16)
    for i in range(depth):
        ar, ai, lr, li, bbr, bbi = s5_prep(s5_a_re[i], s5_a_im[i], s5_log_dt[i],
                                           jnp.swapaxes(s5_b_re[i], 1, 2), jnp.swapaxes(s5_b_im[i], 1, 2))
        lane = lambda t: t.reshape(1, S5_LANES)
        lw = dict(
            gains=norm_gains[i],
            w_in_all=w_in_bf,
            w_dt=w_in_bf[i, :, OFF_DT:OFF_GATE],
            w_gate=w_in_bf[i, :, OFF_GATE:],
            ret_gn=ret_gn[i].reshape(1, R_VAL),
            w_ret_o=w_ret_o[i].astype(bf16),
            s5_d=s5_d[i].reshape(1, D_MODEL),
            s5_mats=(_s5_blockdiag_in(bbr), _s5_blockdiag_in(bbi),
                     _s5_blockdiag_out(s5_c_re[i]), _s5_blockdiag_out(s5_c_im[i]),
                     lane(ar), lane(ai), lane(lr), lane(li)),
            w_s5_glu=w_s5_glu[i].astype(bf16),
            ssd_conv_w=ssd_conv_w[i], ssd_conv_b=ssd_conv_b[i], ssd_dt_bias=ssd_dt_bias[i],
            ssd_a_log=ssd_a_log[i], ssd_d=ssd_d[i], ssd_norm=ssd_norm[i],
            w_ssd_out=w_ssd_out[i].astype(bf16),
            w_mix_out=w_mix_out[i].astype(bf16),
            w_xq=w_xq[i].astype(bf16),
            w_xo=w_xo[i].astype(bf16),
            w_up=w_up[i].astype(bf16),
            w_down=w_down[i].astype(bf16),
        )
        kv = norm_matmul(mem2d, norm_gains[i, 6:7], w_xkv[i].astype(bf16), tm=b * MEM_LEN, tn=512, out_dtype=f32)
        mk = kv[:, :D_MODEL].reshape(b, MEM_LEN, D_MODEL)
        mv = kv[:, D_MODEL:].reshape(b, MEM_LEN, D_MODEL)
        p_mk.append(mk.reshape(b, MEM_LEN, X_HEADS, X_HEAD_DIM))
        p_mv.append(mv.reshape(b, MEM_LEN, X_HEADS, X_HEAD_DIM))
        yp, p_prev, s5p = _run_layer(yp, b, l, 0, mk, mv, 0, p_carried, p_prev, i, depth, lw, prompt=True)
        p_s5.append(s5p)
        s_carried_i = dict(s_carried, s5_r=state_s5_re[i], s5_i=state_s5_im[i])
        ys, s_prev, s5s = _run_layer(ys, db, dl, PAST_LEN, cache_k, cache_v, i * db, s_carried_i, s_prev, i, depth,
                                     lw, prompt=False)
        s_s5.append(s5s)
    p_s5_re, p_s5_im = [jnp.stack(t) for t in zip(*p_s5)]
    s_s5_re, s_s5_im = [jnp.stack(t) for t in zip(*s_s5)]
    unstack = lambda t, bsz: t.reshape((depth, bsz) + t.shape[1:])
    return (yp.reshape(b, l, D_MODEL), ys.reshape(db, dl, D_MODEL),
            unstack(p_prev["ret"], b), p_s5_re, p_s5_im, unstack(p_prev["ssd"], b), unstack(p_prev["conv"], b),
            jnp.stack(p_mk), jnp.stack(p_mv),
            unstack(s_prev["ret"], db), s_s5_re, s_s5_im, unstack(s_prev["ssd"], db), unstack(s_prev["conv"], db))
```

```python
import functools
import math

import jax
import jax.numpy as jnp
from jax import lax
from jax.experimental import pallas as pl
from jax.experimental.pallas import tpu as pltpu

f32 = jnp.float32
bf16 = jnp.bfloat16

D_MODEL = 2048
PAST_LEN = 1024
CHUNK = 64
EPS = 1e-6
R_HEADS = 8
R_DK = 128
R_DV = 256
R_QK = R_HEADS * R_DK
R_VAL = R_HEADS * R_DV
ROPE_BASE = 10000.0
S5_GROUP = 16
S5_GROUPS = D_MODEL // S5_GROUP
S5_STATE = 64
S5_LANES = S5_GROUPS * S5_STATE
S5_GT = 16
S5_NGT = S5_GROUPS // S5_GT
S5_TU = S5_GT * S5_GROUP
S5_TL = S5_GT * S5_STATE
SSD_INNER = 2 * D_MODEL
SSD_HEADDIM = 64
SSD_HEADS = SSD_INNER // SSD_HEADDIM
SSD_GROUPS = 8
SSD_HPG = SSD_HEADS // SSD_GROUPS
SSD_STATE = 128
SSD_CONV_W = 4
SSD_CONV_DIM = SSD_INNER + 2 * SSD_GROUPS * SSD_STATE
MEM_LEN = 256
X_HEADS = 4
X_HEAD_DIM = D_MODEL // X_HEADS
D_FF = 4 * D_MODEL
OFF_Q = 0
OFF_K = OFF_Q + R_QK
OFF_V = OFF_K + R_QK
OFF_G = OFF_V + R_VAL
OFF_U = OFF_G + R_VAL
OFF_Z = OFF_U + D_MODEL
OFF_XBC = OFF_Z + SSD_INNER
OFF_DT = OFF_XBC + SSD_CONV_DIM
OFF_GATE = OFF_DT + SSD_HEADS
MAIN_COLS = OFF_DT

VMEM_LIMIT_BYTES = 52 * 1024 * 1024


def _cparams(*sem):
    return pltpu.CompilerParams(dimension_semantics=sem, vmem_limit_bytes=VMEM_LIMIT_BYTES)


def _rms(x, g):
    return x * lax.rsqrt(jnp.mean(x * x, axis=-1, keepdims=True) + EPS) * g


def _dot(a, b):
    return jnp.dot(a, b, preferred_element_type=f32)


def _dot_nt(a, b):
    return lax.dot_general(a, b, (((1,), (1,)), ((), ())), preferred_element_type=f32)


def _dot_tn(a, b):
    return lax.dot_general(a, b, (((0,), (0,)), ((), ())), preferred_element_type=f32)


def _sigmoid(x):
    return 1.0 / (1.0 + jnp.exp(-x))


def _norm_matmul_kernel(x_ref, g_ref, w_ref, o_ref, hn_ref, *, sigmoid):
    @pl.when(pl.program_id(1) == 0)
    def _():
        hn_ref[...] = _rms(x_ref[...], g_ref[...]).astype(bf16)

    y = _dot(hn_ref[...], w_ref[...])
    if sigmoid:
        y = _sigmoid(y)
    o_ref[...] = y.astype(o_ref.dtype)


def norm_matmul(x, g, w, *, tm, tn, out_dtype, sigmoid=False):
    n, k = x.shape
    m = w.shape[1]
    return pl.pallas_call(
        functools.partial(_norm_matmul_kernel, sigmoid=sigmoid),
        grid=(n // tm, m // tn),
        in_specs=[pl.BlockSpec((tm, k), lambda i, j: (i, 0)),
                  pl.BlockSpec((1, k), lambda i, j: (0, 0)),
                  pl.BlockSpec((k, tn), lambda i, j: (0, j))],
        out_specs=pl.BlockSpec((tm, tn), lambda i, j: (i, j)),
        out_shape=jax.ShapeDtypeStruct((n, m), out_dtype),
        scratch_shapes=[pltpu.VMEM((tm, k), bf16)],
        compiler_params=_cparams("parallel", "arbitrary"),
        name="norm_matmul",
    )(x, g, w)


IN_TN = 1024
_J_G = (OFF_G // IN_TN, OFF_U // IN_TN)
_J_Z = (OFF_Z // IN_TN, OFF_XBC // IN_TN)
_J_XBC = OFF_XBC // IN_TN
_CONV_PAD = 8
_IN_SUB = 256


def _inproj_kernel(*refs, tm, conv, tiles_per_seq):
    if conv:
        x_ref, g_ref, w_ref, cw_ref, cb_ref, o_ref, tails_ref, hn_ref, tail_scr = refs
    else:
        x_ref, g_ref, w_ref, o_ref, hn_ref = refs
    i = pl.program_id(0)
    j = pl.program_id(1)

    @pl.when(j == 0)
    def _():
        hn_ref[...] = _rms(x_ref[...], g_ref[...]).astype(bf16)

    is_gate = ((j >= _J_G[0]) & (j < _J_G[1])) | ((j >= _J_Z[0]) & (j < _J_Z[1]))
    is_plain = jnp.logical_not(is_gate)
    if conv:
        is_plain = is_plain & (j < _J_XBC)
    subs = [slice(s * _IN_SUB, (s + 1) * _IN_SUB) for s in range(IN_TN // _IN_SUB)]

    @pl.when(is_gate)
    def _():
        for cs in subs:
            acc = _dot(hn_ref[...], w_ref[:, cs])
            o_ref[:, cs] = acc * _sigmoid(acc)

    @pl.when(is_plain)
    def _():
        o_ref[...] = _dot(hn_ref[...], w_ref[...])

    if conv:
        @pl.when(j >= _J_XBC)
        def _():
            jj = j - _J_XBC

            @pl.when(i % tiles_per_seq == 0)
            def _():
                tail_scr[jj] = jnp.zeros((_CONV_PAD, IN_TN), f32)

            for cs in subs:
                acc = _dot(hn_ref[...], w_ref[:, cs])
                ext = jnp.concatenate([tail_scr[jj, :, cs], acc], axis=0)
                h = cw_ref[0:1, cs] * ext
                for t in range(1, SSD_CONV_W):
                    h = pltpu.roll(h, 1, 0) + cw_ref[t:t + 1, cs] * ext
                xc = h[_CONV_PAD:, :] + cb_ref[:, cs]
                o_ref[:, cs] = xc * _sigmoid(xc)
                last = acc[tm - _CONV_PAD:, :]
                tail_scr[jj, :, cs] = last
                tails_ref[0, :, cs] = last


def inproj(x, g, w_all, layer, *, tm, conv_w=None, conv_b=None, seq_len=None):
    n, k = x.shape
    conv = conv_w is not None
    nj = MAIN_COLS // IN_TN
    nxbc = SSD_CONV_DIM // IN_TN
    in_specs = [pl.BlockSpec((tm, k), lambda i, j: (i, 0)),
                pl.BlockSpec((1, k), lambda i, j: (0, 0)),
                pl.BlockSpec((None, k, IN_TN), lambda i, j: (layer, 0, j))]
    out_specs = pl.BlockSpec((tm, IN_TN), lambda i, j: (i, j))
    out_shape = jax.ShapeDtypeStruct((n, MAIN_COLS), f32)
    scratch = [pltpu.VMEM((tm, k), bf16)]
    args = [x, g, w_all]
    tiles_per_seq = None
    if conv:
        xcol = lambda i, j: (0, jnp.maximum(j - _J_XBC, 0))
        in_specs += [pl.BlockSpec((SSD_CONV_W, IN_TN), xcol), pl.BlockSpec((1, IN_TN), xcol)]
        out_specs = [out_specs, pl.BlockSpec((1, _CONV_PAD, IN_TN), lambda i, j: (i, 0, jnp.maximum(j - _J_XBC, 0)))]
        out_shape = [out_shape, jax.ShapeDtypeStruct((n // tm, _CONV_PAD, SSD_CONV_DIM), f32)]
        scratch += [pltpu.VMEM((nxbc, _CONV_PAD, IN_TN), f32)]
        args += [conv_w, conv_b.reshape(1, -1)]
        tiles_per_seq = seq_len // tm
    kern = functools.partial(_inproj_kernel, tm=tm, conv=conv, tiles_per_seq=tiles_per_seq)
    out = pl.pallas_call(
        kern,
        grid=(n // tm, nj),
        in_specs=in_specs,
        out_specs=out_specs,
        out_shape=out_shape,
        scratch_shapes=scratch,
        compiler_params=_cparams("arbitrary", "arbitrary"),
        name="inproj",
    )(*args)
    return out if conv else (out, None)


def _matmul_norm_res_kernel(a_ref, w_ref, g_ref, x_ref, o_ref):
    y = _dot(a_ref[...], w_ref[...])
    o_ref[...] = x_ref[...] + _rms(y, g_ref[...])


def matmul_norm_res(a, w, g, x, *, tm):
    n, k = a.shape
    d = w.shape[1]
    return pl.pallas_call(
        _matmul_norm_res_kernel,
        grid=(n // tm,),
        in_specs=[pl.BlockSpec((tm, k), lambda i: (i, 0)),
                  pl.BlockSpec((k, d), lambda i: (0, 0)),
                  pl.BlockSpec((1, d), lambda i: (0, 0)),
                  pl.BlockSpec((tm, d), lambda i: (i, 0))],
        out_specs=pl.BlockSpec((tm, d), lambda i: (i, 0)),
        out_shape=jax.ShapeDtypeStruct((n, d), f32),
        compiler_params=_cparams("parallel"),
        name="matmul_norm_res",
    )(a, w, g, x)


def _layered_state_specs(layer_rows, tail, state_all, prev_out):
    zeros = (0,) * len(tail)
    spec = pl.BlockSpec((1,) + tail, lambda b, c: (layer_rows + b,) + zeros)
    ins, specs = [], []
    if state_all is not None:
        ins.append(state_all)
        specs.append(spec)
    if prev_out is not None:
        ins.append(prev_out)
        specs.append(pl.BlockSpec(memory_space=pl.ANY))
    return ins, specs, spec


def _retention_kernel(*refs, lc, per_step, nsteps, has_init, n_alias):
    q_ref, k_ref, v_ref, g_ref, cos_ref, sin_ref, intra_ref, din_ref, dup_ref, dall_ref, gn_ref = refs[:11]
    s0_ref = refs[11] if has_init else None
    o_ref, snew_ref, s_scr = refs[11 + has_init + n_alias:]
    c = pl.program_id(1)

    @pl.when(c == 0)
    def _():
        if has_init:
            s_scr[...] = s0_ref[0]
        else:
            s_scr[...] = jnp.zeros_like(s_scr)

    for cc, h in [(cc, h) for cc in range(per_step) for h in range(R_HEADS)]:
        rows = slice(cc * lc, (cc + 1) * lc)
        cos = cos_ref[rows, :]
        sin = sin_ref[rows, :]
        q = q_ref[rows, h * R_DK:(h + 1) * R_DK]
        k = k_ref[rows, h * R_DK:(h + 1) * R_DK]
        v = v_ref[rows, h * R_DV:(h + 1) * R_DV].astype(bf16)
        qr = (q * cos + pltpu.roll(q, R_DK // 2, 1) * sin)
        kr = (k * cos + pltpu.roll(k, R_DK // 2, 1) * sin) * (R_DK ** -0.5)
        qb = qr.astype(bf16)
        s = s_scr[h]
        scores = _dot_nt(qb, kr.astype(bf16)) * intra_ref[h]
        out = _dot(scores.astype(bf16), v) + _dot(qb, s.astype(bf16)) * din_ref[h]
        s_scr[h] = dall_ref[h] * s + _dot_tn((kr * dup_ref[h]).astype(bf16), v)
        mu = jnp.mean(out, axis=-1, keepdims=True)
        cen = out - mu
        var = jnp.mean(cen * cen, axis=-1, keepdims=True)
        o = cen * lax.rsqrt(var + EPS) * gn_ref[:, h * R_DV:(h + 1) * R_DV]
        o_ref[rows, h * R_DV:(h + 1) * R_DV] = (g_ref[rows, h * R_DV:(h + 1) * R_DV] * o).astype(bf16)

    @pl.when(c == nsteps - 1)
    def _():
        snew_ref[0] = s_scr[...]


def retention(proj, gn, batch, length, pos0, *, state_all, prev_out, layer, depth):
    lc = min(length, CHUNK)
    per_step = 2 if length % (2 * lc) == 0 else 1
    rows = per_step * lc
    nc = length // rows
    half = R_DK // 2
    inv_freq = jnp.exp(-math.log(ROPE_BASE) * jnp.arange(half, dtype=f32) / half)
    pos = (pos0 + jnp.arange(length, dtype=jnp.int32)).astype(f32)
    ang = pos[:, None] * inv_freq[None]
    cos2 = jnp.concatenate([jnp.cos(ang), jnp.cos(ang)], axis=-1)
    sin2 = jnp.concatenate([-jnp.sin(ang), jnp.sin(ang)], axis=-1)
    log_gamma = jnp.log1p(-jnp.exp2(-5.0 - jnp.arange(R_HEADS, dtype=f32)))
    idx = jnp.arange(lc, dtype=f32)
    lg = log_gamma[:, None]
    intra = jnp.exp(jnp.abs(idx[:, None] - idx[None, :])[None] * lg[:, :, None])
    din = jnp.broadcast_to(jnp.exp((idx + 1.0)[None] * lg)[:, :, None], (R_HEADS, lc, R_DV))
    dup = jnp.broadcast_to(jnp.exp((lc - 1.0 - idx)[None] * lg)[:, :, None], (R_HEADS, lc, R_DK))
    dall = jnp.exp(lc * log_gamma)
    tail = (R_HEADS, R_DK, R_DV)
    st_in, st_specs, st_out_spec = _layered_state_specs(layer * batch, tail, state_all, prev_out)
    n_fixed = 11
    kern = functools.partial(_retention_kernel, lc=lc, per_step=per_step, nsteps=nc,
                             has_init=state_all is not None, n_alias=int(prev_out is not None))
    aliases = {n_fixed + len(st_in) - 1: 1} if prev_out is not None else {}
    return pl.pallas_call(
        kern,
        grid=(batch, nc),
        in_specs=[pl.BlockSpec((rows, R_QK), lambda b, c: (b * nc + c, OFF_Q // R_QK)),
                  pl.BlockSpec((rows, R_QK), lambda b, c: (b * nc + c, OFF_K // R_QK)),
                  pl.BlockSpec((rows, R_VAL), lambda b, c: (b * nc + c, OFF_V // R_VAL)),
                  pl.BlockSpec((rows, R_VAL), lambda b, c: (b * nc + c, OFF_G // R_VAL)),
                  pl.BlockSpec((rows, R_DK), lambda b, c: (c, 0)),
                  pl.BlockSpec((rows, R_DK), lambda b, c: (c, 0)),
                  pl.BlockSpec((R_HEADS, lc, lc), lambda b, c: (0, 0, 0)),
                  pl.BlockSpec((R_HEADS, lc, R_DV), lambda b, c: (0, 0, 0)),
                  pl.BlockSpec((R_HEADS, lc, R_DK), lambda b, c: (0, 0, 0)),
                  pl.BlockSpec(memory_space=pltpu.SMEM),
                  pl.BlockSpec((1, R_VAL), lambda b, c: (0, 0))] + st_specs,
        out_specs=[pl.BlockSpec((rows, R_VAL), lambda b, c: (b * nc + c, 0)), st_out_spec],
        out_shape=[jax.ShapeDtypeStruct((batch * length, R_VAL), bf16),
                   jax.ShapeDtypeStruct((depth * batch,) + tail, f32)],
        scratch_shapes=[pltpu.VMEM(tail, f32)],
        input_output_aliases=aliases,
        compiler_params=_cparams("parallel", "arbitrary"),
        name="retention",
    )(proj, proj, proj, proj, cos2, sin2, intra, din, dup, dall, gn, *st_in)


def _s5_prep_kernel(are_ref, aim_ref, ldt_ref, bre_ref, bim_ref, ar_ref, ai_ref, lr_ref, li_ref, bbr_ref, bbi_ref):
    a_re = are_ref[...]
    a_im = aim_ref[...]
    dt = jnp.exp(ldt_ref[...])
    lr = dt * a_re
    li = dt * a_im
    mag = jnp.exp(lr)
    ar = mag * jnp.cos(li)
    ai = mag * jnp.sin(li)
    nr = ar - 1.0
    den = a_re * a_re + a_im * a_im
    cr = (nr * a_re + ai * a_im) / den
    ci = (ai * a_re - nr * a_im) / den
    ar_ref[...] = ar
    ai_ref[...] = ai
    lr_ref[...] = lr
    li_ref[...] = li
    b_re = bre_ref[...]
    b_im = bim_ref[...]
    bbr_ref[...] = cr[:, None, :] * b_re - ci[:, None, :] * b_im
    bbi_ref[...] = cr[:, None, :] * b_im + ci[:, None, :] * b_re


def s5_prep(a_re, a_im, log_dt, b_re, b_im):
    gp = jax.ShapeDtypeStruct((S5_GROUPS, S5_STATE), f32)
    gb = jax.ShapeDtypeStruct((S5_GROUPS, S5_GROUP, S5_STATE), f32)
    return pl.pallas_call(
        _s5_prep_kernel,
        out_shape=[gp, gp, gp, gp, gb, gb],
        compiler_params=pltpu.CompilerParams(vmem_limit_bytes=VMEM_LIMIT_BYTES),
        name="s5_prep",
    )(a_re, a_im, log_dt.reshape(S5_GROUPS, 1), b_re, b_im)


def _s5_project_in(u_ref, perm_ref, bre_ref, bim_ref, bur_scr, bui_scr):
    ub = u_ref[...].astype(bf16)
    up = _dot(perm_ref[...], ub).astype(bf16)
    bur_scr[...] = _dot(up, bre_ref[0])
    bui_scr[...] = _dot(up, bim_ref[0])


def _s5_scan(bur_scr, bui_scr, ar_row, ai_row, init_r, init_i, nseq, steps):
    lw = min(S5_TL, S5_LANES // nseq)
    fin_r, fin_i = [], []
    for lb in range(S5_TL // lw):
        sl = slice(lb * lw, (lb + 1) * lw)
        ar = jnp.broadcast_to(ar_row[:, sl], (nseq, lw))
        ai = jnp.broadcast_to(ai_row[:, sl], (nseq, lw))
        xr = init_r[:, sl]
        xi = init_i[:, sl]
        for t in range(steps):
            rows = slice(t * nseq, (t + 1) * nseq)
            nxr = ar * xr - ai * xi + bur_scr[rows, sl]
            nxi = ar * xi + ai * xr + bui_scr[rows, sl]
            xr, xi = nxr, nxi
            bur_scr[rows, sl] = xr
            bui_scr[rows, sl] = xi
        fin_r.append(xr)
        fin_i.append(xi)
    return fin_r, fin_i, lw


def _s5_output(u_ref, xr, xi, cre_ref, cim_ref, d_ref, permt_ref, y_ref):
    y = _dot(xr, cre_ref[0]) - _dot(xi, cim_ref[0])
    pt = permt_ref[...]
    y0, y1, y2 = _split3(y)
    y = _dot(pt, y0) + _dot(pt, y1) + _dot(pt, y2)
    y_ref[...] = jax.nn.gelu(y + d_ref[...] * u_ref[...]).astype(bf16)


def _s5_prompt_kernel(u_ref, d_ref, bre_ref, bim_ref, cre_ref, cim_ref, ar_ref, ai_ref, lr_ref, li_ref,
                      perm_ref, permt_ref, y_ref, hr_ref, hi_ref,
                      bur_scr, bui_scr, xbr_scr, xbi_scr, apr_scr, api_scr, xsr_scr, xsi_scr, car_scr, cai_scr,
                      *, nseq, steps, nsc):
    b = pl.program_id(1)
    sc = pl.program_id(2)

    @pl.when((b == 0) & (sc == 0))
    def _():
        tt = (lax.broadcasted_iota(jnp.int32, (steps, S5_TL), 0) + 1).astype(f32)
        mag = jnp.exp(tt * lr_ref[...])
        pwr = mag * jnp.cos(tt * li_ref[...])
        pwi = mag * jnp.sin(tt * li_ref[...])
        for t in range(steps):
            apr_scr[t * nseq:(t + 1) * nseq, :] = jnp.broadcast_to(pwr[t:t + 1, :], (nseq, S5_TL))
            api_scr[t * nseq:(t + 1) * nseq, :] = jnp.broadcast_to(pwi[t:t + 1, :], (nseq, S5_TL))

    @pl.when(sc == 0)
    def _():
        car_scr[...] = jnp.zeros_like(car_scr)
        cai_scr[...] = jnp.zeros_like(cai_scr)

    _s5_project_in(u_ref, perm_ref, bre_ref, bim_ref, bur_scr, bui_scr)
    zero = jnp.zeros((nseq, S5_TL), f32)
    _s5_scan(bur_scr, bui_scr, ar_ref[...], ai_ref[...], zero, zero, nseq, steps)
    pr = apr_scr[(steps - 1) * nseq:(steps - 1) * nseq + 1, :]
    pi = api_scr[(steps - 1) * nseq:(steps - 1) * nseq + 1, :]
    xr = car_scr[...]
    xi = cai_scr[...]
    for r in range(nseq):
        xsr_scr[r:r + 1, :] = xr
        xsi_scr[r:r + 1, :] = xi
        er = bur_scr[(steps - 1) * nseq + r:(steps - 1) * nseq + r + 1, :]
        ei = bui_scr[(steps - 1) * nseq + r:(steps - 1) * nseq + r + 1, :]
        xr, xi = er + pr * xr - pi * xi, ei + pr * xi + pi * xr
    car_scr[...] = xr
    cai_scr[...] = xi
    blk = 2 * nseq
    lq = S5_TL // 4
    for q in range(S5_TL // lq):
        sl = slice(q * lq, (q + 1) * lq)
        xsr = jnp.concatenate([xsr_scr[:, sl], xsr_scr[:, sl]], axis=0)
        xsi = jnp.concatenate([xsi_scr[:, sl], xsi_scr[:, sl]], axis=0)
        for t2 in range(nseq * steps // blk):
            rows = slice(t2 * blk, (t2 + 1) * blk)
            ptr = apr_scr[rows, sl]
            pti = api_scr[rows, sl]
            xbr_scr[rows, sl] = (bur_scr[rows, sl] + ptr * xsr - pti * xsi).astype(bf16)
            xbi_scr[rows, sl] = (bui_scr[rows, sl] + ptr * xsi + pti * xsr).astype(bf16)
    _s5_output(u_ref, xbr_scr[...], xbi_scr[...], cre_ref, cim_ref, d_ref, permt_ref, y_ref)

    @pl.when(sc == nsc - 1)
    def _():
        hr_ref[0] = car_scr[...]
        hi_ref[0] = cai_scr[...]


def _s5_sample_kernel(u_ref, d_ref, bre_ref, bim_ref, cre_ref, cim_ref, ar_ref, ai_ref, perm_ref, permt_ref,
                      h0r_ref, h0i_ref, y_ref, hr_ref, hi_ref, bur_scr, bui_scr, *, nseq, steps):
    _s5_project_in(u_ref, perm_ref, bre_ref, bim_ref, bur_scr, bui_scr)
    fin_r, fin_i, lw = _s5_scan(bur_scr, bui_scr, ar_ref[...], ai_ref[...], h0r_ref[...], h0i_ref[...], nseq, steps)
    for lb in range(len(fin_r)):
        hr_ref[:, lb * lw:(lb + 1) * lw] = fin_r[lb]
        hi_ref[:, lb * lw:(lb + 1) * lw] = fin_i[lb]
    _s5_output(u_ref, bur_scr[...].astype(bf16), bui_scr[...].astype(bf16), cre_ref, cim_ref, d_ref, permt_ref,
               y_ref)


def _s5_blockdiag_in(bb):
    t = bb.reshape(S5_NGT, S5_GT, S5_GROUP, S5_STATE)
    eye = jnp.eye(S5_GT, dtype=bool)
    out = jnp.where(eye[None, :, None, :, None], t[:, :, :, None, :], 0.0)
    return out.reshape(S5_NGT, S5_TU, S5_TL).astype(bf16)


def _s5_blockdiag_out(c):
    t = jnp.transpose(c.reshape(S5_NGT, S5_GT, S5_GROUP, S5_STATE), (0, 1, 3, 2))
    eye = jnp.eye(S5_GT, dtype=bool)
    out = jnp.where(eye[None, :, None, :, None], t[:, :, :, None, :], 0.0)
    return out.reshape(S5_NGT, S5_TL, S5_TU).astype(bf16)


def _s5_perm(nseq, steps):
    n = nseq * steps
    dst = jnp.arange(n)
    src = (dst % nseq) * steps + dst // nseq
    p = src[:, None] == jnp.arange(n)[None, :]
    return jnp.where(p, 1.0, 0.0).astype(bf16), jnp.where(p.T, 1.0, 0.0).astype(bf16)


def _s5_common_specs(rows, row_map):
    lane = lambda *a: (0, a[0])
    mat = lambda *a: (a[0], 0, 0)
    return [pl.BlockSpec((rows, S5_TU), row_map),
            pl.BlockSpec((1, S5_TU), lane),
            pl.BlockSpec((1, S5_TU, S5_TL), mat),
            pl.BlockSpec((1, S5_TU, S5_TL), mat),
            pl.BlockSpec((1, S5_TL, S5_TU), mat),
            pl.BlockSpec((1, S5_TL, S5_TU), mat),
            pl.BlockSpec((1, S5_TL), lane),
            pl.BlockSpec((1, S5_TL), lane)]


def s5_prompt(proj, d, mats, batch, length):
    bre, bim, cre, cim, ar, ai, lr, li = mats
    nseq, steps = 8, CHUNK
    rows = nseq * steps
    nsc = length // rows
    ucol = OFF_U // S5_TU
    kern = functools.partial(_s5_prompt_kernel, nseq=nseq, steps=steps, nsc=nsc)
    lane = lambda g, b, s: (0, g)
    const = lambda g, b, s: (0, 0)
    perm, permt = _s5_perm(nseq, steps)
    st = jax.ShapeDtypeStruct((batch, 1, S5_LANES), f32)
    y, hr, hi = pl.pallas_call(
        kern,
        grid=(S5_NGT, batch, nsc),
        in_specs=_s5_common_specs(rows, lambda g, b, s: (b * nsc + s, ucol + g))
        + [pl.BlockSpec((1, S5_TL), lane), pl.BlockSpec((1, S5_TL), lane),
           pl.BlockSpec((rows, rows), const), pl.BlockSpec((rows, rows), const)],
        out_specs=[pl.BlockSpec((rows, S5_TU), lambda g, b, s: (b * nsc + s, g)),
                   pl.BlockSpec((1, 1, S5_TL), lambda g, b, s: (b, 0, g)),
                   pl.BlockSpec((1, 1, S5_TL), lambda g, b, s: (b, 0, g))],
        out_shape=[jax.ShapeDtypeStruct((batch * length, D_MODEL), bf16), st, st],
        scratch_shapes=[pltpu.VMEM((rows, S5_TL), f32), pltpu.VMEM((rows, S5_TL), f32),
                        pltpu.VMEM((rows, S5_TL), bf16), pltpu.VMEM((rows, S5_TL), bf16),
                        pltpu.VMEM((rows, S5_TL), f32), pltpu.VMEM((rows, S5_TL), f32),
                        pltpu.VMEM((nseq, S5_TL), f32), pltpu.VMEM((nseq, S5_TL), f32),
                        pltpu.VMEM((1, S5_TL), f32), pltpu.VMEM((1, S5_TL), f32)],
        compiler_params=_cparams("arbitrary", "arbitrary", "arbitrary"),
        name="s5_prompt",
    )(proj, d, bre, bim, cre, cim, ar, ai, lr, li, perm, permt)
    return y, hr.reshape(batch, S5_GROUPS, S5_STATE), hi.reshape(batch, S5_GROUPS, S5_STATE)


def s5_sample(proj, d, mats, h0r, h0i, batch, length):
    bre, bim, cre, cim, ar, ai, _, _ = mats
    rows = batch * length
    ucol = OFF_U // S5_TU
    kern = functools.partial(_s5_sample_kernel, nseq=batch, steps=length)
    perm, permt = _s5_perm(batch, length)
    st = jax.ShapeDtypeStruct((batch, S5_LANES), f32)
    state_spec = pl.BlockSpec((batch, S5_TL), lambda g: (0, g))
    perm_spec = pl.BlockSpec((rows, rows), lambda g: (0, 0))
    y, hr, hi = pl.pallas_call(
        kern,
        grid=(S5_NGT,),
        in_specs=_s5_common_specs(rows, lambda g: (0, ucol + g)) + [perm_spec, perm_spec, state_spec, state_spec],
        out_specs=[pl.BlockSpec((rows, S5_TU), lambda g: (0, g)), state_spec, state_spec],
        out_shape=[jax.ShapeDtypeStruct((rows, D_MODEL), bf16), st, st],
        scratch_shapes=[pltpu.VMEM((rows, S5_TL), f32), pltpu.VMEM((rows, S5_TL), f32)],
        compiler_params=_cparams("arbitrary"),
        name="s5_sample",
    )(proj, d, bre, bim, cre, cim, ar, ai, perm, permt, h0r.reshape(batch, S5_LANES), h0i.reshape(batch, S5_LANES))
    return y, hr.reshape(batch, S5_GROUPS, S5_STATE), hi.reshape(batch, S5_GROUPS, S5_STATE)


_SSD_PAD = 8
_SSD_TP = 128


def _split3(v):
    p0 = v.astype(bf16)
    r0 = v - p0.astype(f32)
    p1 = r0.astype(bf16)
    p2 = (r0 - p1.astype(f32)).astype(bf16)
    return p0, p1, p2


def _ssd_conv(c, nchunks, xbc_ref, conv0_ref, cw_ref, cb_ref, convnew_ref, stage, lc):
    keep = SSD_CONV_W - 1
    lo = _SSD_PAD - keep

    @pl.when(c == 0)
    def _():
        if conv0_ref is not None:
            stage[lo:_SSD_PAD, :] = conv0_ref[0]
        else:
            stage[0:_SSD_PAD, :] = jnp.zeros((_SSD_PAD, SSD_CONV_DIM), f32)

    stage[_SSD_PAD:_SSD_PAD + lc, :] = xbc_ref[...]
    xc = cb_ref[...]
    for j in range(SSD_CONV_W):
        xc = xc + cw_ref[j:j + 1, :] * stage[lo + j:lo + j + lc, :]
    xc = xc * _sigmoid(xc)
    tail = stage[lo + lc:_SSD_PAD + lc, :]

    @pl.when(c == nchunks - 1)
    def _():
        convnew_ref[0] = tail

    stage[lo:_SSD_PAD, :] = tail
    return xc


def _ssd_dt_cumsum(dt_ref, dtb_ref, alog_ref, lc):
    dtx = dt_ref[...] + dtb_ref[...]
    dtv = jnp.maximum(dtx, 0.0) + jnp.log1p(jnp.exp(-jnp.abs(dtx)))
    da = dtv * (-jnp.exp(alog_ref[...]))
    ri = lax.broadcasted_iota(jnp.int32, (lc, lc), 0)
    ci = lax.broadcasted_iota(jnp.int32, (lc, lc), 1)
    tri = jnp.where(ri >= ci, 1.0, 0.0).astype(bf16)
    da3 = _split3(da)
    cs = _dot(tri, da3[0]) + _dot(tri, da3[1]) + _dot(tri, da3[2])
    return dtv, cs, da3


def _ssd_parse(refs, has_init, n_alias, n_fixed):
    fixed = refs[:n_fixed]
    conv0_ref, h0_ref = (refs[n_fixed], refs[n_fixed + 1]) if has_init else (None, None)
    rest = refs[n_fixed + 2 * has_init + n_alias:]
    return fixed, conv0_ref, h0_ref, rest


def _ssd_wide_kernel(*refs, nchunks, has_init, n_alias):
    lc = SSD_HEADDIM
    z_ref, xc_ref, dt_ref, dtb_ref, alog_ref, dexp_ref, ng_ref, rep_ref = refs[:8]
    h0_ref = refs[8] if has_init else None
    y_ref, hnew_ref, ht_scr, cm_scr, dx_scr = refs[8 + has_init + n_alias:]
    c = pl.program_id(1)

    @pl.when(c == 0)
    def _():
        if has_init:
            for h in range(SSD_HEADS):
                ht_scr[:, h * lc:(h + 1) * lc] = h0_ref[0, h].T
        else:
            ht_scr[...] = jnp.zeros_like(ht_scr)

    dtv, cs, da3 = _ssd_dt_cumsum(dt_ref, dtb_ref, alog_ref, lc)
    pw = 2 * SSD_HEADDIM
    upper2 = jnp.where(lax.broadcasted_iota(jnp.int32, (lc, pw), 0) <= lax.broadcasted_iota(jnp.int32, (lc, pw), 1) % lc,
                       1.0, 0.0).astype(bf16)
    cst2 = _dot_tn(da3[0], upper2) + _dot_tn(da3[1], upper2) + _dot_tn(da3[2], upper2)
    low_half = lax.broadcasted_iota(jnp.int32, (1, pw), 1) < lc
    rep = rep_ref[...]
    c0, c1, c2 = _split3(cs)
    cm_scr[...] = _dot(c0, rep) + _dot(c1, rep) + _dot(c2, rep)
    d0, d1, d2 = _split3(dtv)
    dx_scr[...] = _dot(d0, rep) + _dot(d1, rep) + _dot(d2, rep)

    gw = SSD_HPG * SSD_HEADDIM
    ti = lax.broadcasted_iota(jnp.int32, (lc, gw), 0)
    si = lax.broadcasted_iota(jnp.int32, (lc, gw), 1) % lc
    causal = ti >= si
    bdmask = (lax.broadcasted_iota(jnp.int32, (pw, pw), 0) // lc) == (lax.broadcasted_iota(jnp.int32, (pw, pw), 1) // lc)
    b_off, c_off = SSD_INNER, SSD_INNER + SSD_GROUPS * SSD_STATE
    for g in range(SSD_GROUPS):
        gs = slice(g * gw, (g + 1) * gw)
        cm = cm_scr[:, gs]
        row = jnp.concatenate(
            [jnp.where(low_half, cst2[h0:h0 + 1, :], cst2[h0 + 1:h0 + 2, :])
             for h0 in range(g * SSD_HPG, (g + 1) * SSD_HPG, 2)], axis=1)
        decay = jnp.exp(jnp.where(causal, cm - row, -jnp.inf))
        last = cm[lc - 1:lc, :]
        xg = xc_ref[:, gs]
        xdt = xg * dx_scr[:, gs]
        bg = xc_ref[:, b_off + g * SSD_STATE:b_off + (g + 1) * SSD_STATE].astype(bf16)
        cg = xc_ref[:, c_off + g * SSD_STATE:c_off + (g + 1) * SSD_STATE].astype(bf16)
        cb2 = _dot_nt(cg, jnp.concatenate([bg, bg], axis=0))
        parts = []
        for pr in range(gw // pw):
            ps = slice(pr * pw, (pr + 1) * pw)
            m = (cb2 * decay[:, ps]).astype(bf16)
            xp = xdt[:, ps]
            bd = jnp.where(bdmask, jnp.concatenate([xp, xp], axis=0), 0.0).astype(bf16)
            parts.append(_dot(m, bd))
        y = jnp.concatenate(parts, axis=1)
        ht = ht_scr[:, gs]
        y = y + _dot(cg, ht.astype(bf16)) * jnp.exp(cm)
        xw = (xdt * jnp.exp(last - cm)).astype(bf16)
        ht_scr[:, gs] = jnp.exp(last) * ht + _dot_tn(bg, xw)
        y = y + dexp_ref[:, gs] * xg
        yg = y * z_ref[:, gs]
        yg = yg * lax.rsqrt(jnp.mean(yg * yg, axis=-1, keepdims=True) + EPS) * ng_ref[:, gs]
        y_ref[:, gs] = yg.astype(bf16)

    @pl.when(c == nchunks - 1)
    def _():
        for h in range(SSD_HEADS):
            hnew_ref[0, h] = ht_scr[:, h * lc:(h + 1) * lc].T


def _ssd_kernel(*refs, lc, nchunks, has_init, n_alias):
    fixed, conv0_ref, h0_ref, rest = _ssd_parse(refs, has_init, n_alias, 9)
    z_ref, xbc_ref, dt_ref, cw_ref, cb_ref, dtb_ref, alog_ref, dexp_ref, ng_ref = fixed
    y_ref, hnew_ref, convnew_ref, stage, h_scr, pad_scr, y_scr = rest
    c = pl.program_id(1)

    @pl.when(c == 0)
    def _():
        if has_init:
            h_scr[...] = h0_ref[0]
        else:
            h_scr[...] = jnp.zeros_like(h_scr)

    xc = _ssd_conv(c, nchunks, xbc_ref, conv0_ref, cw_ref, cb_ref, convnew_ref, stage, lc)
    dtv, cs, _ = _ssd_dt_cumsum(dt_ref, dtb_ref, alog_ref, lc)
    causal = lax.broadcasted_iota(jnp.int32, (lc, lc), 0) >= lax.broadcasted_iota(jnp.int32, (lc, lc), 1)
    pad_scr[...] = jnp.zeros_like(pad_scr)
    pad_scr[0:lc, 0:SSD_HEADS] = cs
    cst = pad_scr[...].T
    ecs = jnp.exp(cs)
    cs_last = cs[lc - 1:lc, :]
    wdec = jnp.exp(cs_last - cs)
    elast = jnp.exp(cs_last)
    xs_off, b_off, c_off = 0, SSD_INNER, SSD_INNER + SSD_GROUPS * SSD_STATE
    for g in range(SSD_GROUPS):
        bg = xc[:, b_off + g * SSD_STATE:b_off + (g + 1) * SSD_STATE].astype(bf16)
        cg = xc[:, c_off + g * SSD_STATE:c_off + (g + 1) * SSD_STATE].astype(bf16)
        cb = _dot_nt(cg, bg)
        for r in range(SSD_HPG):
            h = g * SSD_HPG + r
            hs = slice(h * SSD_HEADDIM, (h + 1) * SSD_HEADDIM)
            col = cs[:, h:h + 1]
            row = cst[h:h + 1, 0:lc]
            decay = jnp.exp(jnp.where(causal, col - row, -jnp.inf))
            xh = xc[:, hs]
            xdt = xh * dtv[:, h:h + 1]
            hst = h_scr[h]
            y = _dot((cb * decay).astype(bf16), xdt.astype(bf16))
            y = y + _dot_nt(cg, hst.astype(bf16)) * ecs[:, h:h + 1]
            xw = (xdt * wdec[:, h:h + 1]).astype(bf16)
            h_scr[h] = elast[:, h:h + 1] * hst + _dot_tn(xw, bg)
            y_scr[:, hs] = y + dexp_ref[:, hs] * xh
        gs = slice(g * SSD_HPG * SSD_HEADDIM, (g + 1) * SSD_HPG * SSD_HEADDIM)
        yg = y_scr[:, gs] * z_ref[:, gs]
        yg = yg * lax.rsqrt(jnp.mean(yg * yg, axis=-1, keepdims=True) + EPS) * ng_ref[:, gs]
        y_ref[:, gs] = yg.astype(bf16)

    @pl.when(c == nchunks - 1)
    def _():
        hnew_ref[0] = h_scr[...]


def ssd(proj, dt_raw, conv_w, conv_b, dt_bias, a_log, d, norm_g, batch, length, *, h_all, conv_all, prev_h,
        prev_conv, layer, depth):
    lc = min(length, CHUNK)
    nc = length // lc
    dexp = jnp.repeat(d, SSD_HEADDIM).reshape(1, SSD_INNER)
    row = lambda b, c: (b * nc + c, 0)
    const2 = lambda b, c: (0, 0)
    h_tail = (SSD_HEADS, SSD_HEADDIM, SSD_STATE)
    c_tail = (SSD_CONV_W - 1, SSD_CONV_DIM)
    has_init = h_all is not None
    has_prev = prev_h is not None
    h_in, h_specs, h_out_spec = _layered_state_specs(layer * batch, h_tail, h_all, None)
    c_in, c_specs, c_out_spec = _layered_state_specs(layer * batch, c_tail, conv_all, None)
    ins = [proj, proj, dt_raw, conv_w, conv_b.reshape(1, -1), dt_bias.reshape(1, -1), a_log.reshape(1, -1), dexp,
           norm_g.reshape(1, -1)]
    specs = [pl.BlockSpec((lc, SSD_INNER), lambda b, c: (b * nc + c, OFF_Z // SSD_INNER)),
             pl.BlockSpec((lc, SSD_CONV_DIM), lambda b, c: (b * nc + c, OFF_XBC // SSD_CONV_DIM)),
             pl.BlockSpec((lc, SSD_HEADS), row),
             pl.BlockSpec((SSD_CONV_W, SSD_CONV_DIM), const2),
             pl.BlockSpec((1, SSD_CONV_DIM), const2),
             pl.BlockSpec((1, SSD_HEADS), const2),
             pl.BlockSpec((1, SSD_HEADS), const2),
             pl.BlockSpec((1, SSD_INNER), const2),
             pl.BlockSpec((1, SSD_INNER), const2)]
    kern = functools.partial(_ssd_kernel, lc=lc, nchunks=nc, has_init=has_init, n_alias=2 * has_prev)
    scratch = [pltpu.VMEM((_SSD_PAD + lc, SSD_CONV_DIM), f32),
               pltpu.VMEM(h_tail, f32),
               pltpu.VMEM((_SSD_TP, _SSD_TP), f32),
               pltpu.VMEM((lc, SSD_INNER), f32)]
    ins += c_in + h_in
    specs += c_specs + h_specs
    aliases = {}
    if has_prev:
        aliases = {len(ins): 1, len(ins) + 1: 2}
        ins += [prev_h, prev_conv]
        specs += [pl.BlockSpec(memory_space=pl.ANY), pl.BlockSpec(memory_space=pl.ANY)]
    return pl.pallas_call(
        kern,
        grid=(batch, nc),
        in_specs=specs,
        out_specs=[pl.BlockSpec((lc, SSD_INNER), row), h_out_spec, c_out_spec],
        out_shape=[jax.ShapeDtypeStruct((batch * length, SSD_INNER), bf16),
                   jax.ShapeDtypeStruct((depth * batch,) + h_tail, f32),
                   jax.ShapeDtypeStruct((depth * batch,) + c_tail, f32)],
        scratch_shapes=scratch,
        input_output_aliases=aliases,
        compiler_params=_cparams("parallel", "arbitrary"),
        name="ssd",
    )(*ins)


def ssd_wide(proj, dt_raw, dt_bias, a_log, d, norm_g, batch, length, *, h_all, prev_h, layer, depth):
    lc = SSD_HEADDIM
    nc = length // lc
    dexp = jnp.repeat(d, SSD_HEADDIM).reshape(1, SSD_INNER)
    rep = jnp.repeat(jnp.eye(SSD_HEADS, dtype=bf16), SSD_HEADDIM, axis=1)
    row = lambda b, c: (b * nc + c, 0)
    const2 = lambda b, c: (0, 0)
    h_tail = (SSD_HEADS, SSD_HEADDIM, SSD_STATE)
    h_in, h_specs, h_out_spec = _layered_state_specs(layer * batch, h_tail, h_all, prev_h)
    ins = [proj, proj, dt_raw, dt_bias.reshape(1, -1), a_log.reshape(1, -1), dexp, norm_g.reshape(1, -1), rep]
    specs = [pl.BlockSpec((lc, SSD_INNER), lambda b, c: (b * nc + c, OFF_Z // SSD_INNER)),
             pl.BlockSpec((lc, SSD_CONV_DIM), lambda b, c: (b * nc + c, OFF_XBC // SSD_CONV_DIM)),
             pl.BlockSpec((lc, SSD_HEADS), row),
             pl.BlockSpec((1, SSD_HEADS), const2),
             pl.BlockSpec((1, SSD_HEADS), const2),
             pl.BlockSpec((1, SSD_INNER), const2),
             pl.BlockSpec((1, SSD_INNER), const2),
             pl.BlockSpec((SSD_HEADS, SSD_INNER), const2)]
    aliases = {len(ins) + len(h_in) - 1: 1} if prev_h is not None else {}
    kern = functools.partial(_ssd_wide_kernel, nchunks=nc, has_init=h_all is not None,
                             n_alias=int(prev_h is not None))
    return pl.pallas_call(
        kern,
        grid=(batch, nc),
        in_specs=specs + h_specs,
        out_specs=[pl.BlockSpec((lc, SSD_INNER), row), h_out_spec],
        out_shape=[jax.ShapeDtypeStruct((batch * length, SSD_INNER), bf16),
                   jax.ShapeDtypeStruct((depth * batch,) + h_tail, f32)],
        scratch_shapes=[pltpu.VMEM((SSD_STATE, SSD_INNER), f32),
                        pltpu.VMEM((lc, SSD_INNER), f32),
                        pltpu.VMEM((lc, SSD_INNER), f32)],
        input_output_aliases=aliases,
        compiler_params=_cparams("parallel", "arbitrary"),
        name="ssd_wide",
    )(*ins, *h_in)


def _merge_kernel(ra_ref, sa_ref, da_ref, gr_ref, gs_ref, gd_ref, wr_ref, wv_ref, wg_ref, wd_ref, o_ref):
    sa = sa_ref[...]
    ret = _dot(ra_ref[...], wr_ref[...])
    s5 = _dot(sa, wv_ref[...]) * _sigmoid(_dot(sa, wg_ref[...]))
    sd = _dot(da_ref[...], wd_ref[...])
    merged = gr_ref[...] * ret + gs_ref[...] * s5 + gd_ref[...] * sd
    o_ref[...] = merged.astype(bf16)


def merge(ret_act, s5_act, ssd_act, gates, w_ret_o, w_glu, w_ssd_out, *, tm, tn):
    n = ret_act.shape[0]
    nj = D_MODEL // tn
    act = lambda i, j: (i, 0)
    return pl.pallas_call(
        _merge_kernel,
        grid=(n // tm, nj),
        in_specs=[pl.BlockSpec((tm, R_VAL), act),
                  pl.BlockSpec((tm, D_MODEL), act),
                  pl.BlockSpec((tm, SSD_INNER), act),
                  pl.BlockSpec((tm, tn), lambda i, j: (i, j)),
                  pl.BlockSpec((tm, tn), lambda i, j: (i, nj + j)),
                  pl.BlockSpec((tm, tn), lambda i, j: (i, 2 * nj + j)),
                  pl.BlockSpec((R_VAL, tn), lambda i, j: (0, j)),
                  pl.BlockSpec((D_MODEL, tn), lambda i, j: (0, j)),
                  pl.BlockSpec((D_MODEL, tn), lambda i, j: (0, nj + j)),
                  pl.BlockSpec((SSD_INNER, tn), lambda i, j: (0, j))],
        out_specs=pl.BlockSpec((tm, tn), lambda i, j: (i, j)),
        out_shape=jax.ShapeDtypeStruct((n, D_MODEL), bf16),
        compiler_params=_cparams("parallel", "arbitrary"),
        name="merge",
    )(ret_act, s5_act, ssd_act, gates, gates, gates, w_ret_o, w_glu, w_glu, w_ssd_out)


def _attn_kernel(q_ref, k_ref, v_ref, o_ref):
    scale = X_HEAD_DIM ** -0.5
    for h in range(X_HEADS):
        hs = slice(h * X_HEAD_DIM, (h + 1) * X_HEAD_DIM)
        s = _dot_nt(q_ref[:, hs], k_ref[0, :, hs].astype(bf16)) * scale
        e = jnp.exp(s - jnp.max(s, axis=-1, keepdims=True))
        p = e / jnp.sum(e, axis=-1, keepdims=True)
        o_ref[:, hs] = _dot(p.astype(bf16), v_ref[0, :, hs].astype(bf16)).astype(bf16)


def attention(q, mem_k, mem_v, batch, length, *, tq, kv_base):
    nt = length // tq
    kv = pl.BlockSpec((1, MEM_LEN, D_MODEL), lambda b, i: (kv_base + b, 0, 0))
    return pl.pallas_call(
        _attn_kernel,
        grid=(batch, nt),
        in_specs=[pl.BlockSpec((tq, D_MODEL), lambda b, i: (b * nt + i, 0)), kv, kv],
        out_specs=pl.BlockSpec((tq, D_MODEL), lambda b, i: (b * nt + i, 0)),
        out_shape=jax.ShapeDtypeStruct((batch * length, D_MODEL), bf16),
        compiler_params=_cparams("parallel", "arbitrary"),
        name="attention",
    )(q, mem_k, mem_v)


def _mlp_kernel(x_ref, g_in_ref, g_out_ref, wu_ref, wd_ref, o_ref, hn_scr, acc_scr, *, nff):
    j = pl.program_id(1)

    @pl.when(j == 0)
    def _():
        hn_scr[...] = _rms(x_ref[...], g_in_ref[...]).astype(bf16)
        acc_scr[...] = jnp.zeros_like(acc_scr)

    h = jnp.maximum(_dot(hn_scr[...], wu_ref[...]), 0.0)
    acc_scr[...] += _dot((h * h).astype(bf16), wd_ref[...])

    @pl.when(j == nff - 1)
    def _():
        o_ref[...] = x_ref[...] + _rms(acc_scr[...], g_out_ref[...])


def mlp(x, g_in, g_out, w_up, w_down, *, tm, tf):
    n, d = x.shape
    nff = D_FF // tf
    kern = functools.partial(_mlp_kernel, nff=nff)
    return pl.pallas_call(
        kern,
        grid=(n // tm, nff),
        in_specs=[pl.BlockSpec((tm, d), lambda i, j: (i, 0)),
                  pl.BlockSpec((1, d), lambda i, j: (0, 0)),
                  pl.BlockSpec((1, d), lambda i, j: (0, 0)),
                  pl.BlockSpec((d, tf), lambda i, j: (0, j)),
                  pl.BlockSpec((tf, d), lambda i, j: (j, 0))],
        out_specs=pl.BlockSpec((tm, d), lambda i, j: (i, 0)),
        out_shape=jax.ShapeDtypeStruct((n, d), f32),
        scratch_shapes=[pltpu.VMEM((tm, d), bf16), pltpu.VMEM((tm, d), f32)],
        compiler_params=_cparams("parallel", "arbitrary"),
        name="mlp",
    )(x, g_in, g_out, w_up, w_down)


def _run_layer(x, batch, length, pos0, mem_k, mem_v, kv_base, carried, prev, layer, depth, lw, *, prompt):
    gains = lw["gains"]
    n = x.shape[0]
    tm = min(512, n)
    tm_in = min(1024, n)
    if prompt:
        proj, tails = inproj(x, gains[0:1], lw["w_in_all"], layer, tm=tm_in, conv_w=lw["ssd_conv_w"],
                             conv_b=lw["ssd_conv_b"], seq_len=length)
    else:
        proj, tails = inproj(x, gains[0:1], lw["w_in_all"], layer, tm=tm_in)
    dt_raw = norm_matmul(x, gains[0:1], lw["w_dt"], tm=tm_in, tn=SSD_HEADS, out_dtype=f32)
    gates = norm_matmul(x, gains[0:1], lw["w_gate"], tm=tm_in, tn=1024, out_dtype=f32, sigmoid=True)
    ret_act, ret_all = retention(proj, lw["ret_gn"], batch, length, pos0, state_all=carried["ret"],
                                 prev_out=prev["ret"], layer=layer, depth=depth)
    if prompt:
        s5_act, s5_r_new, s5_i_new = s5_prompt(proj, lw["s5_d"], lw["s5_mats"], batch, length)
    else:
        s5_act, s5_r_new, s5_i_new = s5_sample(proj, lw["s5_d"], lw["s5_mats"], carried["s5_r"], carried["s5_i"],
                                               batch, length)
    if prompt:
        ssd_act, ssd_all = ssd_wide(proj, dt_raw, lw["ssd_dt_bias"], lw["ssd_a_log"], lw["ssd_d"], lw["ssd_norm"],
                                    batch, length, h_all=carried["ssd"], prev_h=prev["ssd"], layer=layer, depth=depth)
        tiles_per_seq = length // tm_in
        seq_tails = tails.reshape(batch, tiles_per_seq, _CONV_PAD, SSD_CONV_DIM)[:, -1, _CONV_PAD - (SSD_CONV_W - 1):]
        conv_all = seq_tails if prev["conv"] is None else jnp.concatenate([prev["conv"], seq_tails], axis=0)
    else:
        ssd_act, ssd_all, conv_all = ssd(proj, dt_raw, lw["ssd_conv_w"], lw["ssd_conv_b"], lw["ssd_dt_bias"],
                                         lw["ssd_a_log"], lw["ssd_d"], lw["ssd_norm"], batch, length,
                                         h_all=carried["ssd"], conv_all=carried["conv"], prev_h=prev["ssd"],
                                         prev_conv=prev["conv"], layer=layer, depth=depth)
    merged = merge(ret_act, s5_act, ssd_act, gates, lw["w_ret_o"], lw["w_s5_glu"], lw["w_ssd_out"], tm=tm, tn=512)
    x = matmul_norm_res(merged, lw["w_mix_out"], gains[1:2], x, tm=tm)
    q = norm_matmul(x, gains[2:3], lw["w_xq"], tm=tm_in, tn=512, out_dtype=bf16)
    att = attention(q, mem_k, mem_v, batch, length, tq=min(length, 512), kv_base=kv_base)
    x = matmul_norm_res(att, lw["w_xo"], gains[3:4], x, tm=tm)
    x = mlp(x, gains[4:5], gains[5:6], lw["w_up"], lw["w_down"], tm=tm, tf=1024)
    return x, dict(ret=ret_all, ssd=ssd_all, conv=conv_all), (s5_r_new, s5_i_new)


def kernel(x_prompt, x_sample, mem_prompt, state_ret, state_s5_re, state_s5_im, state_ssd, cache_ssd_conv,
           cache_mem_k, cache_mem_v, norm_gains, w_in, ret_gn, w_ret_o, s5_a_re, s5_a_im, s5_b_re, s5_b_im,
           s5_c_re, s5_c_im, s5_d, s5_log_dt, w_s5_glu, ssd_conv_w, ssd_conv_b, ssd_dt_bias, ssd_a_log, ssd_d,
           ssd_norm, w_ssd_out, w_mix_out, w_xq, w_xkv, w_xo, w_up, w_down):
    b, l, _ = x_prompt.shape
    db, dl, _ = x_sample.shape
    depth = w_in.shape[0]
    yp = x_prompt.reshape(b * l, D_MODEL)
    ys = x_sample.reshape(db * dl, D_MODEL)
    mem2d = mem_prompt.reshape(b * MEM_LEN, D_MODEL)
    cache_k = cache_mem_k.reshape(depth * db, MEM_LEN, D_MODEL)
    cache_v = cache_mem_v.reshape(depth * db, MEM_LEN, D_MODEL)
    s_carried = dict(ret=state_ret.reshape(depth * db, R_HEADS, R_DK, R_DV),
                     ssd=state_ssd.reshape(depth * db, SSD_HEADS, SSD_HEADDIM, SSD_STATE),
                     conv=cache_ssd_conv.reshape(depth * db, SSD_CONV_W - 1, SSD_CONV_DIM))
    p_carried = dict(ret=None, ssd=None, conv=None)
    p_prev = dict(ret=None, ssd=None, conv=None)
    s_prev = dict(ret=None, ssd=None, conv=None)
    p_s5, s_s5, p_mk, p_mv = [], [], [], []
    w_in_bf = w_in.astype(bf16)
    for i in range(depth):
        ar, ai, lr, li, bbr, bbi = s5_prep(s5_a_re[i], s5_a_im[i], s5_log_dt[i],
                                           jnp.swapaxes(s5_b_re[i], 1, 2), jnp.swapaxes(s5_b_im[i], 1, 2))
        lane = lambda t: t.reshape(1, S5_LANES)
        lw = dict(
            gains=norm_gains[i],
            w_in_all=w_in_bf,
            w_dt=w_in_bf[i, :, OFF_DT:OFF_GATE],
            w_gate=w_in_bf[i, :, OFF_GATE:],
            ret_gn=ret_gn[i].reshape(1, R_VAL),
            w_ret_o=w_ret_o[i].astype(bf16),
            s5_d=s5_d[i].reshape(1, D_MODEL),
            s5_mats=(_s5_blockdiag_in(bbr), _s5_blockdiag_in(bbi),
                     _s5_blockdiag_out(s5_c_re[i]), _s5_blockdiag_out(s5_c_im[i]),
                     lane(ar), lane(ai), lane(lr), lane(li)),
            w_s5_glu=w_s5_glu[i].astype(bf16),
            ssd_conv_w=ssd_conv_w[i], ssd_conv_b=ssd_conv_b[i], ssd_dt_bias=ssd_dt_bias[i],
            ssd_a_log=ssd_a_log[i], ssd_d=ssd_d[i], ssd_norm=ssd_norm[i],
            w_ssd_out=w_ssd_out[i].astype(bf16),
            w_mix_out=w_mix_out[i].astype(bf16),
            w_xq=w_xq[i].astype(bf16),
            w_xo=w_xo[i].astype(bf16),
            w_up=w_up[i].astype(bf16),
            w_down=w_down[i].astype(bf16),
        )
        kv = norm_matmul(mem2d, norm_gains[i, 6:7], w_xkv[i].astype(bf16), tm=b * MEM_LEN, tn=512, out_dtype=f32)
        mk = kv[:, :D_MODEL].reshape(b, MEM_LEN, D_MODEL)
        mv = kv[:, D_MODEL:].reshape(b, MEM_LEN, D_MODEL)
        p_mk.append(mk.reshape(b, MEM_LEN, X_HEADS, X_HEAD_DIM))
        p_mv.append(mv.reshape(b, MEM_LEN, X_HEADS, X_HEAD_DIM))
        yp, p_prev, s5p = _run_layer(yp, b, l, 0, mk, mv, 0, p_carried, p_prev, i, depth, lw, prompt=True)
        p_s5.append(s5p)
        s_carried_i = dict(s_carried, s5_r=state_s5_re[i], s5_i=state_s5_im[i])
        ys, s_prev, s5s = _run_layer(ys, db, dl, PAST_LEN, cache_k, cache_v, i * db, s_carried_i, s_prev, i, depth,
                                     lw, prompt=False)
        s_s5.append(s5s)
    p_s5_re, p_s5_im = [jnp.stack(t) for t in zip(*p_s5)]
    s_s5_re, s_s5_im = [jnp.stack(t) for t in zip(*s_s5)]
    unstack = lambda t, bsz: t.reshape((depth, bsz) + t.shape[1:])
    return (yp.reshape(b, l, D_MODEL), ys.reshape(db, dl, D_MODEL),
            unstack(p_prev["ret"], b), p_s5_re, p_s5_im, unstack(p_prev["ssd"], b), unstack(p_prev["conv"], b),
            jnp.stack(p_mk), jnp.stack(p_mv),
            unstack(s_prev["ret"], db), s_s5_re, s_s5_im, unstack(s_prev["ssd"], db), unstack(s_prev["conv"], db))
```

```python
import functools
import math

import jax
import jax.numpy as jnp
from jax import lax
from jax.experimental import pallas as pl
from jax.experimental.pallas import tpu as pltpu

f32 = jnp.float32
bf16 = jnp.bfloat16

D_MODEL = 2048
PAST_LEN = 1024
CHUNK = 64
EPS = 1e-6
R_HEADS = 8
R_DK = 128
R_DV = 256
R_QK = R_HEADS * R_DK
R_VAL = R_HEADS * R_DV
ROPE_BASE = 10000.0
S5_GROUP = 16
S5_GROUPS = D_MODEL // S5_GROUP
S5_STATE = 64
S5_LANES = S5_GROUPS * S5_STATE
S5_GT = 16
S5_NGT = S5_GROUPS // S5_GT
S5_TU = S5_GT * S5_GROUP
S5_TL = S5_GT * S5_STATE
SSD_INNER = 2 * D_MODEL
SSD_HEADDIM = 64
SSD_HEADS = SSD_INNER // SSD_HEADDIM
SSD_GROUPS = 8
SSD_HPG = SSD_HEADS // SSD_GROUPS
SSD_STATE = 128
SSD_CONV_W = 4
SSD_CONV_DIM = SSD_INNER + 2 * SSD_GROUPS * SSD_STATE
MEM_LEN = 256
X_HEADS = 4
X_HEAD_DIM = D_MODEL // X_HEADS
D_FF = 4 * D_MODEL
OFF_Q = 0
OFF_K = OFF_Q + R_QK
OFF_V = OFF_K + R_QK
OFF_G = OFF_V + R_VAL
OFF_U = OFF_G + R_VAL
OFF_Z = OFF_U + D_MODEL
OFF_XBC = OFF_Z + SSD_INNER
OFF_DT = OFF_XBC + SSD_CONV_DIM
OFF_GATE = OFF_DT + SSD_HEADS
MAIN_COLS = OFF_DT

VMEM_LIMIT_BYTES = 56 * 1024 * 1024


def _cparams(*sem):
    return pltpu.CompilerParams(dimension_semantics=sem, vmem_limit_bytes=VMEM_LIMIT_BYTES)


def _rms(x, g):
    return x * lax.rsqrt(jnp.mean(x * x, axis=-1, keepdims=True) + EPS) * g


def _dot(a, b):
    return jnp.dot(a, b, preferred_element_type=f32)


def _dot_nt(a, b):
    return lax.dot_general(a, b, (((1,), (1,)), ((), ())), preferred_element_type=f32)


def _dot_tn(a, b):
    return lax.dot_general(a, b, (((0,), (0,)), ((), ())), preferred_element_type=f32)


def _sigmoid(x):
    return 1.0 / (1.0 + jnp.exp(-x))


def _norm_matmul_kernel(x_ref, g_ref, w_ref, o_ref, hn_ref, *, sigmoid):
    @pl.when(pl.program_id(1) == 0)
    def _():
        hn_ref[...] = _rms(x_ref[...], g_ref[...]).astype(bf16)

    y = _dot(hn_ref[...], w_ref[...])
    if sigmoid:
        y = _sigmoid(y)
    o_ref[...] = y.astype(o_ref.dtype)


def norm_matmul(x, g, w, *, tm, tn, out_dtype, sigmoid=False):
    n, k = x.shape
    m = w.shape[1]
    return pl.pallas_call(
        functools.partial(_norm_matmul_kernel, sigmoid=sigmoid),
        grid=(n // tm, m // tn),
        in_specs=[pl.BlockSpec((tm, k), lambda i, j: (i, 0)),
                  pl.BlockSpec((1, k), lambda i, j: (0, 0)),
                  pl.BlockSpec((k, tn), lambda i, j: (0, j))],
        out_specs=pl.BlockSpec((tm, tn), lambda i, j: (i, j)),
        out_shape=jax.ShapeDtypeStruct((n, m), out_dtype),
        scratch_shapes=[pltpu.VMEM((tm, k), bf16)],
        compiler_params=_cparams("parallel", "arbitrary"),
        name="norm_matmul",
    )(x, g, w)


IN_TN = 1024
_J_G = (OFF_G // IN_TN, OFF_U // IN_TN)
_J_Z = (OFF_Z // IN_TN, OFF_XBC // IN_TN)
_J_XBC = OFF_XBC // IN_TN
_NJ_MAIN = MAIN_COLS // IN_TN
_GATE_TN = 512
_CONV_PAD = 8
_IN_SUB = 256


def _inproj_kernel(*refs, tm, conv, tiles_per_seq):
    if conv:
        (x_ref, g_ref, w_ref, wg_ref, wdt_ref, cw_ref, cb_ref, o_ref, gates_ref, dt_ref, tails_ref, hn_ref,
         tail_scr) = refs
    else:
        x_ref, g_ref, w_ref, wg_ref, wdt_ref, o_ref, gates_ref, dt_ref, hn_ref = refs
    i = pl.program_id(0)
    j = pl.program_id(1)

    @pl.when(j == 0)
    def _():
        hn_ref[...] = _rms(x_ref[...], g_ref[...]).astype(bf16)

    is_main = j < _NJ_MAIN
    is_silu = ((j >= _J_G[0]) & (j < _J_G[1])) | ((j >= _J_Z[0]) & (j < _J_Z[1]))
    is_plain = is_main & jnp.logical_not(is_silu)
    if conv:
        is_plain = is_plain & (j < _J_XBC)
    subs = [slice(s * _IN_SUB, (s + 1) * _IN_SUB) for s in range(IN_TN // _IN_SUB)]

    @pl.when(is_silu)
    def _():
        for cs in subs:
            acc = _dot(hn_ref[...], w_ref[:, cs])
            o_ref[:, cs] = acc * _sigmoid(acc)

    @pl.when(is_plain)
    def _():
        o_ref[...] = _dot(hn_ref[...], w_ref[...])

    @pl.when(j >= _NJ_MAIN)
    def _():
        for cs in subs[:_GATE_TN // _IN_SUB]:
            gates_ref[:, cs] = _sigmoid(_dot(hn_ref[...], wg_ref[:, cs]))

    @pl.when(j == _NJ_MAIN)
    def _():
        dt_ref[...] = _dot(hn_ref[...], wdt_ref[...])

    if conv:
        @pl.when((j >= _J_XBC) & is_main)
        def _():
            jj = j - _J_XBC

            @pl.when(i % tiles_per_seq == 0)
            def _():
                tail_scr[jj] = jnp.zeros((_CONV_PAD, IN_TN), f32)

            for cs in subs:
                acc = _dot(hn_ref[...], w_ref[:, cs])
                ext = jnp.concatenate([tail_scr[jj, :, cs], acc], axis=0)
                h = cw_ref[0:1, cs] * ext
                for t in range(1, SSD_CONV_W):
                    h = pltpu.roll(h, 1, 0) + cw_ref[t:t + 1, cs] * ext
                xc = h[_CONV_PAD:, :] + cb_ref[:, cs]
                o_ref[:, cs] = xc * _sigmoid(xc)
                last = acc[tm - _CONV_PAD:, :]
                tail_scr[jj, :, cs] = last
                tails_ref[0, :, cs] = last


def inproj(x, g, w_all, w_gate, w_dt, layer, *, tm, conv_w=None, conv_b=None, seq_len=None):
    n, k = x.shape
    conv = conv_w is not None
    nxbc = SSD_CONV_DIM // IN_TN
    ngate = w_gate.shape[1] // _GATE_TN
    gate_col = lambda j: jnp.maximum(j - _NJ_MAIN, 0)
    main_col = lambda j: jnp.minimum(j, _NJ_MAIN - 1)
    in_specs = [pl.BlockSpec((tm, k), lambda i, j: (i, 0)),
                pl.BlockSpec((1, k), lambda i, j: (0, 0)),
                pl.BlockSpec((None, k, IN_TN), lambda i, j: (layer, 0, main_col(j))),
                pl.BlockSpec((k, _GATE_TN), lambda i, j: (0, gate_col(j))),
                pl.BlockSpec((k, SSD_HEADS), lambda i, j: (0, 0))]
    out_specs = [pl.BlockSpec((tm, IN_TN), lambda i, j: (i, main_col(j))),
                 pl.BlockSpec((tm, _GATE_TN), lambda i, j: (i, gate_col(j))),
                 pl.BlockSpec((tm, SSD_HEADS), lambda i, j: (i, 0))]
    out_shape = [jax.ShapeDtypeStruct((n, MAIN_COLS), f32),
                 jax.ShapeDtypeStruct((n, w_gate.shape[1]), f32),
                 jax.ShapeDtypeStruct((n, SSD_HEADS), f32)]
    scratch = [pltpu.VMEM((tm, k), bf16)]
    args = [x, g, w_all, w_gate, w_dt]
    tiles_per_seq = None
    if conv:
        xcol = lambda j: jnp.clip(j - _J_XBC, 0, nxbc - 1)
        in_specs += [pl.BlockSpec((SSD_CONV_W, IN_TN), lambda i, j: (0, xcol(j))),
                     pl.BlockSpec((1, IN_TN), lambda i, j: (0, xcol(j)))]
        out_specs.append(pl.BlockSpec((1, _CONV_PAD, IN_TN), lambda i, j: (i, 0, xcol(j))))
        out_shape.append(jax.ShapeDtypeStruct((n // tm, _CONV_PAD, SSD_CONV_DIM), f32))
        scratch += [pltpu.VMEM((nxbc, _CONV_PAD, IN_TN), f32)]
        args += [conv_w, conv_b.reshape(1, -1)]
        tiles_per_seq = seq_len // tm
    kern = functools.partial(_inproj_kernel, tm=tm, conv=conv, tiles_per_seq=tiles_per_seq)
    out = pl.pallas_call(
        kern,
        grid=(n // tm, _NJ_MAIN + ngate),
        in_specs=in_specs,
        out_specs=out_specs,
        out_shape=out_shape,
        scratch_shapes=scratch,
        compiler_params=_cparams("arbitrary", "arbitrary"),
        name="inproj",
    )(*args)
    return tuple(out) if conv else (*out, None)


def _matmul_norm_res_kernel(a_ref, w_ref, g_ref, x_ref, o_ref):
    y = _dot(a_ref[...], w_ref[...])
    o_ref[...] = x_ref[...] + _rms(y, g_ref[...])


def matmul_norm_res(a, w, g, x, *, tm):
    n, k = a.shape
    d = w.shape[1]
    return pl.pallas_call(
        _matmul_norm_res_kernel,
        grid=(n // tm,),
        in_specs=[pl.BlockSpec((tm, k), lambda i: (i, 0)),
                  pl.BlockSpec((k, d), lambda i: (0, 0)),
                  pl.BlockSpec((1, d), lambda i: (0, 0)),
                  pl.BlockSpec((tm, d), lambda i: (i, 0))],
        out_specs=pl.BlockSpec((tm, d), lambda i: (i, 0)),
        out_shape=jax.ShapeDtypeStruct((n, d), f32),
        compiler_params=_cparams("parallel"),
        name="matmul_norm_res",
    )(a, w, g, x)


def _layered_state_specs(layer_rows, tail, state_all, prev_out):
    zeros = (0,) * len(tail)
    spec = pl.BlockSpec((1,) + tail, lambda b, c: (layer_rows + b,) + zeros)
    ins, specs = [], []
    if state_all is not None:
        ins.append(state_all)
        specs.append(spec)
    if prev_out is not None:
        ins.append(prev_out)
        specs.append(pl.BlockSpec(memory_space=pl.ANY))
    return ins, specs, spec


def _retention_kernel(*refs, lc, per_step, nsteps, has_init, n_alias):
    q_ref, k_ref, v_ref, g_ref, cos_ref, sin_ref, intra_ref, din_ref, dup_ref, dall_ref, gn_ref = refs[:11]
    s0_ref = refs[11] if has_init else None
    o_ref, snew_ref, s_scr = refs[11 + has_init + n_alias:]
    c = pl.program_id(1)

    @pl.when(c == 0)
    def _():
        if has_init:
            s_scr[...] = s0_ref[0]
        else:
            s_scr[...] = jnp.zeros_like(s_scr)

    for cc, h in [(cc, h) for cc in range(per_step) for h in range(R_HEADS)]:
        rows = slice(cc * lc, (cc + 1) * lc)
        cos = cos_ref[rows, :]
        sin = sin_ref[rows, :]
        q = q_ref[rows, h * R_DK:(h + 1) * R_DK]
        k = k_ref[rows, h * R_DK:(h + 1) * R_DK]
        v = v_ref[rows, h * R_DV:(h + 1) * R_DV].astype(bf16)
        qr = (q * cos + pltpu.roll(q, R_DK // 2, 1) * sin)
        kr = (k * cos + pltpu.roll(k, R_DK // 2, 1) * sin) * (R_DK ** -0.5)
        qb = qr.astype(bf16)
        s = s_scr[h]
        scores = _dot_nt(qb, kr.astype(bf16)) * intra_ref[h]
        out = _dot(scores.astype(bf16), v) + _dot(qb, s.astype(bf16)) * din_ref[h]
        s_scr[h] = dall_ref[h] * s + _dot_tn((kr * dup_ref[h]).astype(bf16), v)
        mu = jnp.mean(out, axis=-1, keepdims=True)
        cen = out - mu
        var = jnp.mean(cen * cen, axis=-1, keepdims=True)
        o = cen * lax.rsqrt(var + EPS) * gn_ref[:, h * R_DV:(h + 1) * R_DV]
        o_ref[rows, h * R_DV:(h + 1) * R_DV] = (g_ref[rows, h * R_DV:(h + 1) * R_DV] * o).astype(bf16)

    @pl.when(c == nsteps - 1)
    def _():
        snew_ref[0] = s_scr[...]


def retention(proj, gn, batch, length, pos0, *, state_all, prev_out, layer, depth):
    lc = min(length, CHUNK)
    per_step = 2 if length % (2 * lc) == 0 else 1
    rows = per_step * lc
    nc = length // rows
    half = R_DK // 2
    inv_freq = jnp.exp(-math.log(ROPE_BASE) * jnp.arange(half, dtype=f32) / half)
    pos = (pos0 + jnp.arange(length, dtype=jnp.int32)).astype(f32)
    ang = pos[:, None] * inv_freq[None]
    cos2 = jnp.concatenate([jnp.cos(ang), jnp.cos(ang)], axis=-1)
    sin2 = jnp.concatenate([-jnp.sin(ang), jnp.sin(ang)], axis=-1)
    log_gamma = jnp.log1p(-jnp.exp2(-5.0 - jnp.arange(R_HEADS, dtype=f32)))
    idx = jnp.arange(lc, dtype=f32)
    lg = log_gamma[:, None]
    intra = jnp.exp(jnp.abs(idx[:, None] - idx[None, :])[None] * lg[:, :, None])
    din = jnp.broadcast_to(jnp.exp((idx + 1.0)[None] * lg)[:, :, None], (R_HEADS, lc, R_DV))
    dup = jnp.broadcast_to(jnp.exp((lc - 1.0 - idx)[None] * lg)[:, :, None], (R_HEADS, lc, R_DK))
    dall = jnp.exp(lc * log_gamma)
    tail = (R_HEADS, R_DK, R_DV)
    st_in, st_specs, st_out_spec = _layered_state_specs(layer * batch, tail, state_all, prev_out)
    n_fixed = 11
    kern = functools.partial(_retention_kernel, lc=lc, per_step=per_step, nsteps=nc,
                             has_init=state_all is not None, n_alias=int(prev_out is not None))
    aliases = {n_fixed + len(st_in) - 1: 1} if prev_out is not None else {}
    return pl.pallas_call(
        kern,
        grid=(batch, nc),
        in_specs=[pl.BlockSpec((rows, R_QK), lambda b, c: (b * nc + c, OFF_Q // R_QK)),
                  pl.BlockSpec((rows, R_QK), lambda b, c: (b * nc + c, OFF_K // R_QK)),
                  pl.BlockSpec((rows, R_VAL), lambda b, c: (b * nc + c, OFF_V // R_VAL)),
                  pl.BlockSpec((rows, R_VAL), lambda b, c: (b * nc + c, OFF_G // R_VAL)),
                  pl.BlockSpec((rows, R_DK), lambda b, c: (c, 0)),
                  pl.BlockSpec((rows, R_DK), lambda b, c: (c, 0)),
                  pl.BlockSpec((R_HEADS, lc, lc), lambda b, c: (0, 0, 0)),
                  pl.BlockSpec((R_HEADS, lc, R_DV), lambda b, c: (0, 0, 0)),
                  pl.BlockSpec((R_HEADS, lc, R_DK), lambda b, c: (0, 0, 0)),
                  pl.BlockSpec(memory_space=pltpu.SMEM),
                  pl.BlockSpec((1, R_VAL), lambda b, c: (0, 0))] + st_specs,
        out_specs=[pl.BlockSpec((rows, R_VAL), lambda b, c: (b * nc + c, 0)), st_out_spec],
        out_shape=[jax.ShapeDtypeStruct((batch * length, R_VAL), bf16),
                   jax.ShapeDtypeStruct((depth * batch,) + tail, f32)],
        scratch_shapes=[pltpu.VMEM(tail, f32)],
        input_output_aliases=aliases,
        compiler_params=_cparams("parallel", "arbitrary"),
        name="retention",
    )(proj, proj, proj, proj, cos2, sin2, intra, din, dup, dall, gn, *st_in)


def _s5_prep_kernel(are_ref, aim_ref, ldt_ref, bre_ref, bim_ref, ar_ref, ai_ref, lr_ref, li_ref, bbr_ref, bbi_ref):
    a_re = are_ref[...]
    a_im = aim_ref[...]
    dt = jnp.exp(ldt_ref[...])
    lr = dt * a_re
    li = dt * a_im
    mag = jnp.exp(lr)
    ar = mag * jnp.cos(li)
    ai = mag * jnp.sin(li)
    nr = ar - 1.0
    den = a_re * a_re + a_im * a_im
    cr = (nr * a_re + ai * a_im) / den
    ci = (ai * a_re - nr * a_im) / den
    ar_ref[...] = ar
    ai_ref[...] = ai
    lr_ref[...] = lr
    li_ref[...] = li
    b_re = bre_ref[...]
    b_im = bim_ref[...]
    bbr_ref[...] = cr[:, None, :] * b_re - ci[:, None, :] * b_im
    bbi_ref[...] = cr[:, None, :] * b_im + ci[:, None, :] * b_re


def s5_prep(a_re, a_im, log_dt, b_re, b_im):
    gp = jax.ShapeDtypeStruct((S5_GROUPS, S5_STATE), f32)
    gb = jax.ShapeDtypeStruct((S5_GROUPS, S5_GROUP, S5_STATE), f32)
    return pl.pallas_call(
        _s5_prep_kernel,
        out_shape=[gp, gp, gp, gp, gb, gb],
        compiler_params=pltpu.CompilerParams(vmem_limit_bytes=VMEM_LIMIT_BYTES),
        name="s5_prep",
    )(a_re, a_im, log_dt.reshape(S5_GROUPS, 1), b_re, b_im)


def _s5_project_in(u_ref, perm_ref, bre_ref, bim_ref, bur_scr, bui_scr):
    ub = u_ref[...].astype(bf16)
    up = _dot(perm_ref[...], ub).astype(bf16)
    bur_scr[...] = _dot(up, bre_ref[0])
    bui_scr[...] = _dot(up, bim_ref[0])


def _s5_scan(bur_scr, bui_scr, ar_row, ai_row, init_r, init_i, nseq, steps):
    lw = min(S5_TL, S5_LANES // nseq)
    fin_r, fin_i = [], []
    for lb in range(S5_TL // lw):
        sl = slice(lb * lw, (lb + 1) * lw)
        ar = jnp.broadcast_to(ar_row[:, sl], (nseq, lw))
        ai = jnp.broadcast_to(ai_row[:, sl], (nseq, lw))
        xr = init_r[:, sl]
        xi = init_i[:, sl]
        for t in range(steps):
            rows = slice(t * nseq, (t + 1) * nseq)
            nxr = ar * xr - ai * xi + bur_scr[rows, sl]
            nxi = ar * xi + ai * xr + bui_scr[rows, sl]
            xr, xi = nxr, nxi
            bur_scr[rows, sl] = xr
            bui_scr[rows, sl] = xi
        fin_r.append(xr)
        fin_i.append(xi)
    return fin_r, fin_i, lw


def _s5_output(u_ref, xr, xi, cre_ref, cim_ref, d_ref, permt_ref, y_ref):
    y = _dot(xr, cre_ref[0]) - _dot(xi, cim_ref[0])
    pt = permt_ref[...]
    y0, y1, y2 = _split3(y)
    y = _dot(pt, y0) + _dot(pt, y1) + _dot(pt, y2)
    y_ref[...] = jax.nn.gelu(y + d_ref[...] * u_ref[...]).astype(bf16)


def _s5_prompt_kernel(u_ref, d_ref, bre_ref, bim_ref, cre_ref, cim_ref, ar_ref, ai_ref, lr_ref, li_ref,
                      perm_ref, permt_ref, y_ref, hr_ref, hi_ref,
                      bur_scr, bui_scr, xbr_scr, xbi_scr, apr_scr, api_scr, xsr_scr, xsi_scr, car_scr, cai_scr,
                      *, nseq, steps, nsc):
    b = pl.program_id(1)
    sc = pl.program_id(2)

    @pl.when((b == 0) & (sc == 0))
    def _():
        tt = (lax.broadcasted_iota(jnp.int32, (steps, S5_TL), 0) + 1).astype(f32)
        mag = jnp.exp(tt * lr_ref[...])
        pwr = mag * jnp.cos(tt * li_ref[...])
        pwi = mag * jnp.sin(tt * li_ref[...])
        for t in range(steps):
            apr_scr[t * nseq:(t + 1) * nseq, :] = jnp.broadcast_to(pwr[t:t + 1, :], (nseq, S5_TL))
            api_scr[t * nseq:(t + 1) * nseq, :] = jnp.broadcast_to(pwi[t:t + 1, :], (nseq, S5_TL))

    @pl.when(sc == 0)
    def _():
        car_scr[...] = jnp.zeros_like(car_scr)
        cai_scr[...] = jnp.zeros_like(cai_scr)

    _s5_project_in(u_ref, perm_ref, bre_ref, bim_ref, bur_scr, bui_scr)
    zero = jnp.zeros((nseq, S5_TL), f32)
    _s5_scan(bur_scr, bui_scr, ar_ref[...], ai_ref[...], zero, zero, nseq, steps)
    pr = apr_scr[(steps - 1) * nseq:(steps - 1) * nseq + 1, :]
    pi = api_scr[(steps - 1) * nseq:(steps - 1) * nseq + 1, :]
    xr = car_scr[...]
    xi = cai_scr[...]
    for r in range(nseq):
        xsr_scr[r:r + 1, :] = xr
        xsi_scr[r:r + 1, :] = xi
        er = bur_scr[(steps - 1) * nseq + r:(steps - 1) * nseq + r + 1, :]
        ei = bui_scr[(steps - 1) * nseq + r:(steps - 1) * nseq + r + 1, :]
        xr, xi = er + pr * xr - pi * xi, ei + pr * xi + pi * xr
    car_scr[...] = xr
    cai_scr[...] = xi
    blk = 2 * nseq
    lq = S5_TL // 4
    for q in range(S5_TL // lq):
        sl = slice(q * lq, (q + 1) * lq)
        xsr = jnp.concatenate([xsr_scr[:, sl], xsr_scr[:, sl]], axis=0)
        xsi = jnp.concatenate([xsi_scr[:, sl], xsi_scr[:, sl]], axis=0)
        for t2 in range(nseq * steps // blk):
            rows = slice(t2 * blk, (t2 + 1) * blk)
            ptr = apr_scr[rows, sl]
            pti = api_scr[rows, sl]
            xbr_scr[rows, sl] = (bur_scr[rows, sl] + ptr * xsr - pti * xsi).astype(bf16)
            xbi_scr[rows, sl] = (bui_scr[rows, sl] + ptr * xsi + pti * xsr).astype(bf16)
    _s5_output(u_ref, xbr_scr[...], xbi_scr[...], cre_ref, cim_ref, d_ref, permt_ref, y_ref)

    @pl.when(sc == nsc - 1)
    def _():
        hr_ref[0] = car_scr[...]
        hi_ref[0] = cai_scr[...]


def _s5_sample_kernel(u_ref, d_ref, bre_ref, bim_ref, cre_ref, cim_ref, ar_ref, ai_ref, perm_ref, permt_ref,
                      h0r_ref, h0i_ref, y_ref, hr_ref, hi_ref, bur_scr, bui_scr, *, nseq, steps):
    _s5_project_in(u_ref, perm_ref, bre_ref, bim_ref, bur_scr, bui_scr)
    fin_r, fin_i, lw = _s5_scan(bur_scr, bui_scr, ar_ref[...], ai_ref[...], h0r_ref[...], h0i_ref[...], nseq, steps)
    for lb in range(len(fin_r)):
        hr_ref[:, lb * lw:(lb + 1) * lw] = fin_r[lb]
        hi_ref[:, lb * lw:(lb + 1) * lw] = fin_i[lb]
    _s5_output(u_ref, bur_scr[...].astype(bf16), bui_scr[...].astype(bf16), cre_ref, cim_ref, d_ref, permt_ref,
               y_ref)


def _s5_blockdiag_in(bb):
    t = bb.reshape(S5_NGT, S5_GT, S5_GROUP, S5_STATE)
    eye = jnp.eye(S5_GT, dtype=bool)
    out = jnp.where(eye[None, :, None, :, None], t[:, :, :, None, :], 0.0)
    return out.reshape(S5_NGT, S5_TU, S5_TL).astype(bf16)


def _s5_blockdiag_out(c):
    t = jnp.transpose(c.reshape(S5_NGT, S5_GT, S5_GROUP, S5_STATE), (0, 1, 3, 2))
    eye = jnp.eye(S5_GT, dtype=bool)
    out = jnp.where(eye[None, :, None, :, None], t[:, :, :, None, :], 0.0)
    return out.reshape(S5_NGT, S5_TL, S5_TU).astype(bf16)


def _s5_perm(nseq, steps):
    n = nseq * steps
    dst = jnp.arange(n)
    src = (dst % nseq) * steps + dst // nseq
    p = src[:, None] == jnp.arange(n)[None, :]
    return jnp.where(p, 1.0, 0.0).astype(bf16), jnp.where(p.T, 1.0, 0.0).astype(bf16)


def _s5_common_specs(rows, row_map):
    lane = lambda *a: (0, a[0])
    mat = lambda *a: (a[0], 0, 0)
    return [pl.BlockSpec((rows, S5_TU), row_map),
            pl.BlockSpec((1, S5_TU), lane),
            pl.BlockSpec((1, S5_TU, S5_TL), mat),
            pl.BlockSpec((1, S5_TU, S5_TL), mat),
            pl.BlockSpec((1, S5_TL, S5_TU), mat),
            pl.BlockSpec((1, S5_TL, S5_TU), mat),
            pl.BlockSpec((1, S5_TL), lane),
            pl.BlockSpec((1, S5_TL), lane)]


def s5_prompt(proj, d, mats, batch, length):
    bre, bim, cre, cim, ar, ai, lr, li = mats
    nseq, steps = 8, CHUNK
    rows = nseq * steps
    nsc = length // rows
    ucol = OFF_U // S5_TU
    kern = functools.partial(_s5_prompt_kernel, nseq=nseq, steps=steps, nsc=nsc)
    lane = lambda g, b, s: (0, g)
    const = lambda g, b, s: (0, 0)
    perm, permt = _s5_perm(nseq, steps)
    st = jax.ShapeDtypeStruct((batch, 1, S5_LANES), f32)
    y, hr, hi = pl.pallas_call(
        kern,
        grid=(S5_NGT, batch, nsc),
        in_specs=_s5_common_specs(rows, lambda g, b, s: (b * nsc + s, ucol + g))
        + [pl.BlockSpec((1, S5_TL), lane), pl.BlockSpec((1, S5_TL), lane),
           pl.BlockSpec((rows, rows), const), pl.BlockSpec((rows, rows), const)],
        out_specs=[pl.BlockSpec((rows, S5_TU), lambda g, b, s: (b * nsc + s, g)),
                   pl.BlockSpec((1, 1, S5_TL), lambda g, b, s: (b, 0, g)),
                   pl.BlockSpec((1, 1, S5_TL), lambda g, b, s: (b, 0, g))],
        out_shape=[jax.ShapeDtypeStruct((batch * length, D_MODEL), bf16), st, st],
        scratch_shapes=[pltpu.VMEM((rows, S5_TL), f32), pltpu.VMEM((rows, S5_TL), f32),
                        pltpu.VMEM((rows, S5_TL), bf16), pltpu.VMEM((rows, S5_TL), bf16),
                        pltpu.VMEM((rows, S5_TL), f32), pltpu.VMEM((rows, S5_TL), f32),
                        pltpu.VMEM((nseq, S5_TL), f32), pltpu.VMEM((nseq, S5_TL), f32),
                        pltpu.VMEM((1, S5_TL), f32), pltpu.VMEM((1, S5_TL), f32)],
        compiler_params=_cparams("arbitrary", "arbitrary", "arbitrary"),
        name="s5_prompt",
    )(proj, d, bre, bim, cre, cim, ar, ai, lr, li, perm, permt)
    return y, hr.reshape(batch, S5_GROUPS, S5_STATE), hi.reshape(batch, S5_GROUPS, S5_STATE)


def s5_sample(proj, d, mats, h0r, h0i, batch, length):
    bre, bim, cre, cim, ar, ai, _, _ = mats
    rows = batch * length
    ucol = OFF_U // S5_TU
    kern = functools.partial(_s5_sample_kernel, nseq=batch, steps=length)
    perm, permt = _s5_perm(batch, length)
    st = jax.ShapeDtypeStruct((batch, S5_LANES), f32)
    state_spec = pl.BlockSpec((batch, S5_TL), lambda g: (0, g))
    perm_spec = pl.BlockSpec((rows, rows), lambda g: (0, 0))
    y, hr, hi = pl.pallas_call(
        kern,
        grid=(S5_NGT,),
        in_specs=_s5_common_specs(rows, lambda g: (0, ucol + g)) + [perm_spec, perm_spec, state_spec, state_spec],
        out_specs=[pl.BlockSpec((rows, S5_TU), lambda g: (0, g)), state_spec, state_spec],
        out_shape=[jax.ShapeDtypeStruct((rows, D_MODEL), bf16), st, st],
        scratch_shapes=[pltpu.VMEM((rows, S5_TL), f32), pltpu.VMEM((rows, S5_TL), f32)],
        compiler_params=_cparams("arbitrary"),
        name="s5_sample",
    )(proj, d, bre, bim, cre, cim, ar, ai, perm, permt, h0r.reshape(batch, S5_LANES), h0i.reshape(batch, S5_LANES))
    return y, hr.reshape(batch, S5_GROUPS, S5_STATE), hi.reshape(batch, S5_GROUPS, S5_STATE)


_SSD_PAD = 8
_SSD_TP = 128


def _split3(v):
    p0 = v.astype(bf16)
    r0 = v - p0.astype(f32)
    p1 = r0.astype(bf16)
    p2 = (r0 - p1.astype(f32)).astype(bf16)
    return p0, p1, p2


def _ssd_conv(c, nchunks, xbc_ref, conv0_ref, cw_ref, cb_ref, convnew_ref, stage, lc):
    keep = SSD_CONV_W - 1
    lo = _SSD_PAD - keep

    @pl.when(c == 0)
    def _():
        if conv0_ref is not None:
            stage[lo:_SSD_PAD, :] = conv0_ref[0]
        else:
            stage[0:_SSD_PAD, :] = jnp.zeros((_SSD_PAD, SSD_CONV_DIM), f32)

    stage[_SSD_PAD:_SSD_PAD + lc, :] = xbc_ref[...]
    xc = cb_ref[...]
    for j in range(SSD_CONV_W):
        xc = xc + cw_ref[j:j + 1, :] * stage[lo + j:lo + j + lc, :]
    xc = xc * _sigmoid(xc)
    tail = stage[lo + lc:_SSD_PAD + lc, :]

    @pl.when(c == nchunks - 1)
    def _():
        convnew_ref[0] = tail

    stage[lo:_SSD_PAD, :] = tail
    return xc


def _ssd_dt_cumsum(dt_raw, dtb_ref, alog_ref, lc):
    dtx = dt_raw + dtb_ref[...]
    dtv = jnp.maximum(dtx, 0.0) + jnp.log1p(jnp.exp(-jnp.abs(dtx)))
    da = dtv * (-jnp.exp(alog_ref[...]))
    ri = lax.broadcasted_iota(jnp.int32, (lc, lc), 0)
    ci = lax.broadcasted_iota(jnp.int32, (lc, lc), 1)
    tri = jnp.where(ri >= ci, 1.0, 0.0).astype(bf16)
    da3 = _split3(da)
    cs = _dot(tri, da3[0]) + _dot(tri, da3[1]) + _dot(tri, da3[2])
    return dtv, cs, da3


def _ssd_parse(refs, has_init, n_alias, n_fixed):
    fixed = refs[:n_fixed]
    conv0_ref, h0_ref = (refs[n_fixed], refs[n_fixed + 1]) if has_init else (None, None)
    rest = refs[n_fixed + 2 * has_init + n_alias:]
    return fixed, conv0_ref, h0_ref, rest


def _ssd_wide_kernel(*refs, nsteps, per_step, has_init, n_alias):
    lc = SSD_HEADDIM
    z_ref, xc_ref, dt_ref, dtb_ref, alog_ref, dexp_ref, ng_ref, rep_ref = refs[:8]
    h0_ref = refs[8] if has_init else None
    y_ref, hnew_ref, ht_scr, cm_scr, dx_scr = refs[8 + has_init + n_alias:]
    c = pl.program_id(1)

    @pl.when(c == 0)
    def _():
        if has_init:
            for h in range(SSD_HEADS):
                ht_scr[:, h * lc:(h + 1) * lc] = h0_ref[0, h].T
        else:
            ht_scr[...] = jnp.zeros_like(ht_scr)

    pw = 2 * SSD_HEADDIM
    gw = SSD_HPG * SSD_HEADDIM
    ti = lax.broadcasted_iota(jnp.int32, (lc, gw), 0)
    si = lax.broadcasted_iota(jnp.int32, (lc, gw), 1) % lc
    causal = ti >= si
    bdmask = (lax.broadcasted_iota(jnp.int32, (pw, pw), 0) // lc) == (lax.broadcasted_iota(jnp.int32, (pw, pw), 1) // lc)
    low_half = lax.broadcasted_iota(jnp.int32, (1, pw), 1) < lc
    for cc in range(per_step):
        _ssd_wide_chunk(slice(cc * lc, (cc + 1) * lc), z_ref, xc_ref, dt_ref, dtb_ref, alog_ref, dexp_ref, ng_ref,
                        rep_ref, y_ref, ht_scr, cm_scr, dx_scr, causal, bdmask, low_half)

    @pl.when(c == nsteps - 1)
    def _():
        for h in range(SSD_HEADS):
            hnew_ref[0, h] = ht_scr[:, h * lc:(h + 1) * lc].T


def _ssd_wide_chunk(rows, z_ref, xc_ref, dt_ref, dtb_ref, alog_ref, dexp_ref, ng_ref, rep_ref, y_ref, ht_scr, cm_scr,
                    dx_scr, causal, bdmask, low_half):
    lc = SSD_HEADDIM
    pw = 2 * SSD_HEADDIM
    gw = SSD_HPG * SSD_HEADDIM
    dtv, cs, da3 = _ssd_dt_cumsum(dt_ref[rows, :], dtb_ref, alog_ref, lc)
    upper2 = jnp.where(lax.broadcasted_iota(jnp.int32, (lc, pw), 0) <= lax.broadcasted_iota(jnp.int32, (lc, pw), 1) % lc,
                       1.0, 0.0).astype(bf16)
    cst2 = _dot_tn(da3[0], upper2) + _dot_tn(da3[1], upper2) + _dot_tn(da3[2], upper2)
    rep = rep_ref[...]
    c0, c1, c2 = _split3(cs)
    cm_scr[rows, :] = _dot(c0, rep) + _dot(c1, rep) + _dot(c2, rep)
    d0, d1, d2 = _split3(dtv)
    dx_scr[rows, :] = _dot(d0, rep) + _dot(d1, rep) + _dot(d2, rep)

    b_off, c_off = SSD_INNER, SSD_INNER + SSD_GROUPS * SSD_STATE
    for g in range(SSD_GROUPS):
        gs = slice(g * gw, (g + 1) * gw)
        cm = cm_scr[rows, gs]
        row = jnp.concatenate(
            [jnp.where(low_half, cst2[h0:h0 + 1, :], cst2[h0 + 1:h0 + 2, :])
             for h0 in range(g * SSD_HPG, (g + 1) * SSD_HPG, 2)], axis=1)
        decay = jnp.exp(jnp.where(causal, cm - row, -jnp.inf))
        last = cm[lc - 1:lc, :]
        xg = xc_ref[rows, gs]
        xdt = xg * dx_scr[rows, gs]
        bg = xc_ref[rows, b_off + g * SSD_STATE:b_off + (g + 1) * SSD_STATE].astype(bf16)
        cg = xc_ref[rows, c_off + g * SSD_STATE:c_off + (g + 1) * SSD_STATE].astype(bf16)
        cb2 = _dot_nt(cg, jnp.concatenate([bg, bg], axis=0))
        parts = []
        for pr in range(gw // pw):
            ps = slice(pr * pw, (pr + 1) * pw)
            m = (cb2 * decay[:, ps]).astype(bf16)
            xp = xdt[:, ps]
            bd = jnp.where(bdmask, jnp.concatenate([xp, xp], axis=0), 0.0).astype(bf16)
            parts.append(_dot(m, bd))
        y = jnp.concatenate(parts, axis=1)
        ht = ht_scr[:, gs]
        y = y + _dot(cg, ht.astype(bf16)) * jnp.exp(cm)
        xw = (xdt * jnp.exp(last - cm)).astype(bf16)
        ht_scr[:, gs] = jnp.exp(last) * ht + _dot_tn(bg, xw)
        y = y + dexp_ref[:, gs] * xg
        yg = y * z_ref[rows, gs]
        yg = yg * lax.rsqrt(jnp.mean(yg * yg, axis=-1, keepdims=True) + EPS) * ng_ref[:, gs]
        y_ref[rows, gs] = yg.astype(bf16)


def _ssd_kernel(*refs, lc, nchunks, has_init, n_alias):
    fixed, conv0_ref, h0_ref, rest = _ssd_parse(refs, has_init, n_alias, 9)
    z_ref, xbc_ref, dt_ref, cw_ref, cb_ref, dtb_ref, alog_ref, dexp_ref, ng_ref = fixed
    y_ref, hnew_ref, convnew_ref, stage, h_scr, pad_scr, y_scr = rest
    c = pl.program_id(1)

    @pl.when(c == 0)
    def _():
        if has_init:
            h_scr[...] = h0_ref[0]
        else:
            h_scr[...] = jnp.zeros_like(h_scr)

    xc = _ssd_conv(c, nchunks, xbc_ref, conv0_ref, cw_ref, cb_ref, convnew_ref, stage, lc)
    dtv, cs, _ = _ssd_dt_cumsum(dt_ref[...], dtb_ref, alog_ref, lc)
    causal = lax.broadcasted_iota(jnp.int32, (lc, lc), 0) >= lax.broadcasted_iota(jnp.int32, (lc, lc), 1)
    pad_scr[...] = jnp.zeros_like(pad_scr)
    pad_scr[0:lc, 0:SSD_HEADS] = cs
    cst = pad_scr[...].T
    ecs = jnp.exp(cs)
    cs_last = cs[lc - 1:lc, :]
    wdec = jnp.exp(cs_last - cs)
    elast = jnp.exp(cs_last)
    xs_off, b_off, c_off = 0, SSD_INNER, SSD_INNER + SSD_GROUPS * SSD_STATE
    for g in range(SSD_GROUPS):
        bg = xc[:, b_off + g * SSD_STATE:b_off + (g + 1) * SSD_STATE].astype(bf16)
        cg = xc[:, c_off + g * SSD_STATE:c_off + (g + 1) * SSD_STATE].astype(bf16)
        cb = _dot_nt(cg, bg)
        for r in range(SSD_HPG):
            h = g * SSD_HPG + r
            hs = slice(h * SSD_HEADDIM, (h + 1) * SSD_HEADDIM)
            col = cs[:, h:h + 1]
            row = cst[h:h + 1, 0:lc]
            decay = jnp.exp(jnp.where(causal, col - row, -jnp.inf))
            xh = xc[:, hs]
            xdt = xh * dtv[:, h:h + 1]
            hst = h_scr[h]
            y = _dot((cb * decay).astype(bf16), xdt.astype(bf16))
            y = y + _dot_nt(cg, hst.astype(bf16)) * ecs[:, h:h + 1]
            xw = (xdt * wdec[:, h:h + 1]).astype(bf16)
            h_scr[h] = elast[:, h:h + 1] * hst + _dot_tn(xw, bg)
            y_scr[:, hs] = y + dexp_ref[:, hs] * xh
        gs = slice(g * SSD_HPG * SSD_HEADDIM, (g + 1) * SSD_HPG * SSD_HEADDIM)
        yg = y_scr[:, gs] * z_ref[:, gs]
        yg = yg * lax.rsqrt(jnp.mean(yg * yg, axis=-1, keepdims=True) + EPS) * ng_ref[:, gs]
        y_ref[:, gs] = yg.astype(bf16)

    @pl.when(c == nchunks - 1)
    def _():
        hnew_ref[0] = h_scr[...]


def ssd(proj, dt_raw, conv_w, conv_b, dt_bias, a_log, d, norm_g, batch, length, *, h_all, conv_all, prev_h,
        prev_conv, layer, depth):
    lc = min(length, CHUNK)
    nc = length // lc
    dexp = jnp.repeat(d, SSD_HEADDIM).reshape(1, SSD_INNER)
    row = lambda b, c: (b * nc + c, 0)
    const2 = lambda b, c: (0, 0)
    h_tail = (SSD_HEADS, SSD_HEADDIM, SSD_STATE)
    c_tail = (SSD_CONV_W - 1, SSD_CONV_DIM)
    has_init = h_all is not None
    has_prev = prev_h is not None
    h_in, h_specs, h_out_spec = _layered_state_specs(layer * batch, h_tail, h_all, None)
    c_in, c_specs, c_out_spec = _layered_state_specs(layer * batch, c_tail, conv_all, None)
    ins = [proj, proj, dt_raw, conv_w, conv_b.reshape(1, -1), dt_bias.reshape(1, -1), a_log.reshape(1, -1), dexp,
           norm_g.reshape(1, -1)]
    specs = [pl.BlockSpec((lc, SSD_INNER), lambda b, c: (b * nc + c, OFF_Z // SSD_INNER)),
             pl.BlockSpec((lc, SSD_CONV_DIM), lambda b, c: (b * nc + c, OFF_XBC // SSD_CONV_DIM)),
             pl.BlockSpec((lc, SSD_HEADS), row),
             pl.BlockSpec((SSD_CONV_W, SSD_CONV_DIM), const2),
             pl.BlockSpec((1, SSD_CONV_DIM), const2),
             pl.BlockSpec((1, SSD_HEADS), const2),
             pl.BlockSpec((1, SSD_HEADS), const2),
             pl.BlockSpec((1, SSD_INNER), const2),
             pl.BlockSpec((1, SSD_INNER), const2)]
    kern = functools.partial(_ssd_kernel, lc=lc, nchunks=nc, has_init=has_init, n_alias=2 * has_prev)
    scratch = [pltpu.VMEM((_SSD_PAD + lc, SSD_CONV_DIM), f32),
               pltpu.VMEM(h_tail, f32),
               pltpu.VMEM((_SSD_TP, _SSD_TP), f32),
               pltpu.VMEM((lc, SSD_INNER), f32)]
    ins += c_in + h_in
    specs += c_specs + h_specs
    aliases = {}
    if has_prev:
        aliases = {len(ins): 1, len(ins) + 1: 2}
        ins += [prev_h, prev_conv]
        specs += [pl.BlockSpec(memory_space=pl.ANY), pl.BlockSpec(memory_space=pl.ANY)]
    return pl.pallas_call(
        kern,
        grid=(batch, nc),
        in_specs=specs,
        out_specs=[pl.BlockSpec((lc, SSD_INNER), row), h_out_spec, c_out_spec],
        out_shape=[jax.ShapeDtypeStruct((batch * length, SSD_INNER), bf16),
                   jax.ShapeDtypeStruct((depth * batch,) + h_tail, f32),
                   jax.ShapeDtypeStruct((depth * batch,) + c_tail, f32)],
        scratch_shapes=scratch,
        input_output_aliases=aliases,
        compiler_params=_cparams("parallel", "arbitrary"),
        name="ssd",
    )(*ins)


def ssd_wide(proj, dt_raw, dt_bias, a_log, d, norm_g, batch, length, *, h_all, prev_h, layer, depth):
    lc = SSD_HEADDIM
    per_step = 2 if length % (2 * lc) == 0 else 1
    rows = per_step * lc
    nc = length // rows
    dexp = jnp.repeat(d, SSD_HEADDIM).reshape(1, SSD_INNER)
    rep = jnp.repeat(jnp.eye(SSD_HEADS, dtype=bf16), SSD_HEADDIM, axis=1)
    row = lambda b, c: (b * nc + c, 0)
    const2 = lambda b, c: (0, 0)
    h_tail = (SSD_HEADS, SSD_HEADDIM, SSD_STATE)
    h_in, h_specs, h_out_spec = _layered_state_specs(layer * batch, h_tail, h_all, prev_h)
    ins = [proj, proj, dt_raw, dt_bias.reshape(1, -1), a_log.reshape(1, -1), dexp, norm_g.reshape(1, -1), rep]
    specs = [pl.BlockSpec((rows, SSD_INNER), lambda b, c: (b * nc + c, OFF_Z // SSD_INNER)),
             pl.BlockSpec((rows, SSD_CONV_DIM), lambda b, c: (b * nc + c, OFF_XBC // SSD_CONV_DIM)),
             pl.BlockSpec((rows, SSD_HEADS), row),
             pl.BlockSpec((1, SSD_HEADS), const2),
             pl.BlockSpec((1, SSD_HEADS), const2),
             pl.BlockSpec((1, SSD_INNER), const2),
             pl.BlockSpec((1, SSD_INNER), const2),
             pl.BlockSpec((SSD_HEADS, SSD_INNER), const2)]
    aliases = {len(ins) + len(h_in) - 1: 1} if prev_h is not None else {}
    kern = functools.partial(_ssd_wide_kernel, nsteps=nc, per_step=per_step, has_init=h_all is not None,
                             n_alias=int(prev_h is not None))
    return pl.pallas_call(
        kern,
        grid=(batch, nc),
        in_specs=specs + h_specs,
        out_specs=[pl.BlockSpec((rows, SSD_INNER), row), h_out_spec],
        out_shape=[jax.ShapeDtypeStruct((batch * length, SSD_INNER), bf16),
                   jax.ShapeDtypeStruct((depth * batch,) + h_tail, f32)],
        scratch_shapes=[pltpu.VMEM((SSD_STATE, SSD_INNER), f32),
                        pltpu.VMEM((rows, SSD_INNER), f32),
                        pltpu.VMEM((rows, SSD_INNER), f32)],
        input_output_aliases=aliases,
        compiler_params=_cparams("parallel", "arbitrary"),
        name="ssd_wide",
    )(*ins, *h_in)


def _merge_kernel(ra_ref, sa_ref, da_ref, gr_ref, gs_ref, gd_ref, wr_ref, wv_ref, wg_ref, wd_ref, o_ref):
    sa = sa_ref[...]
    ret = _dot(ra_ref[...], wr_ref[...])
    s5 = _dot(sa, wv_ref[...]) * _sigmoid(_dot(sa, wg_ref[...]))
    sd = _dot(da_ref[...], wd_ref[...])
    merged = gr_ref[...] * ret + gs_ref[...] * s5 + gd_ref[...] * sd
    o_ref[...] = merged.astype(bf16)


def merge(ret_act, s5_act, ssd_act, gates, w_ret_o, w_glu, w_ssd_out, *, tm, tn):
    n = ret_act.shape[0]
    nj = D_MODEL // tn
    act = lambda i, j: (i, 0)
    return pl.pallas_call(
        _merge_kernel,
        grid=(n // tm, nj),
        in_specs=[pl.BlockSpec((tm, R_VAL), act),
                  pl.BlockSpec((tm, D_MODEL), act),
                  pl.BlockSpec((tm, SSD_INNER), act),
                  pl.BlockSpec((tm, tn), lambda i, j: (i, j)),
                  pl.BlockSpec((tm, tn), lambda i, j: (i, nj + j)),
                  pl.BlockSpec((tm, tn), lambda i, j: (i, 2 * nj + j)),
                  pl.BlockSpec((R_VAL, tn), lambda i, j: (0, j)),
                  pl.BlockSpec((D_MODEL, tn), lambda i, j: (0, j)),
                  pl.BlockSpec((D_MODEL, tn), lambda i, j: (0, nj + j)),
                  pl.BlockSpec((SSD_INNER, tn), lambda i, j: (0, j))],
        out_specs=pl.BlockSpec((tm, tn), lambda i, j: (i, j)),
        out_shape=jax.ShapeDtypeStruct((n, D_MODEL), bf16),
        compiler_params=_cparams("parallel", "arbitrary"),
        name="merge",
    )(ret_act, s5_act, ssd_act, gates, gates, gates, w_ret_o, w_glu, w_glu, w_ssd_out)


def _attn_kernel(q_ref, k_ref, v_ref, o_ref):
    scale = X_HEAD_DIM ** -0.5
    for h in range(X_HEADS):
        hs = slice(h * X_HEAD_DIM, (h + 1) * X_HEAD_DIM)
        s = _dot_nt(q_ref[:, hs], k_ref[0, :, hs].astype(bf16)) * scale
        e = jnp.exp(s - jnp.max(s, axis=-1, keepdims=True))
        p = e / jnp.sum(e, axis=-1, keepdims=True)
        o_ref[:, hs] = _dot(p.astype(bf16), v_ref[0, :, hs].astype(bf16)).astype(bf16)


def attention(q, mem_k, mem_v, batch, length, *, tq, kv_base):
    nt = length // tq
    kv = pl.BlockSpec((1, MEM_LEN, D_MODEL), lambda b, i: (kv_base + b, 0, 0))
    return pl.pallas_call(
        _attn_kernel,
        grid=(batch, nt),
        in_specs=[pl.BlockSpec((tq, D_MODEL), lambda b, i: (b * nt + i, 0)), kv, kv],
        out_specs=pl.BlockSpec((tq, D_MODEL), lambda b, i: (b * nt + i, 0)),
        out_shape=jax.ShapeDtypeStruct((batch * length, D_MODEL), bf16),
        compiler_params=_cparams("parallel", "arbitrary"),
        name="attention",
    )(q, mem_k, mem_v)


def _mlp_kernel(x_ref, g_in_ref, g_out_ref, wu_ref, wd_ref, o_ref, hn_scr, acc_scr, *, nff):
    j = pl.program_id(1)

    @pl.when(j == 0)
    def _():
        hn_scr[...] = _rms(x_ref[...], g_in_ref[...]).astype(bf16)
        acc_scr[...] = jnp.zeros_like(acc_scr)

    h = jnp.maximum(_dot(hn_scr[...], wu_ref[...]), 0.0)
    acc_scr[...] += _dot((h * h).astype(bf16), wd_ref[...])

    @pl.when(j == nff - 1)
    def _():
        o_ref[...] = x_ref[...] + _rms(acc_scr[...], g_out_ref[...])


def mlp(x, g_in, g_out, w_up, w_down, *, tm, tf):
    n, d = x.shape
    nff = D_FF // tf
    kern = functools.partial(_mlp_kernel, nff=nff)
    return pl.pallas_call(
        kern,
        grid=(n // tm, nff),
        in_specs=[pl.BlockSpec((tm, d), lambda i, j: (i, 0)),
                  pl.BlockSpec((1, d), lambda i, j: (0, 0)),
                  pl.BlockSpec((1, d), lambda i, j: (0, 0)),
                  pl.BlockSpec((d, tf), lambda i, j: (0, j)),
                  pl.BlockSpec((tf, d), lambda i, j: (j, 0))],
        out_specs=pl.BlockSpec((tm, d), lambda i, j: (i, 0)),
        out_shape=jax.ShapeDtypeStruct((n, d), f32),
        scratch_shapes=[pltpu.VMEM((tm, d), bf16), pltpu.VMEM((tm, d), f32)],
        compiler_params=_cparams("parallel", "arbitrary"),
        name="mlp",
    )(x, g_in, g_out, w_up, w_down)


def _run_layer(x, batch, length, pos0, mem_k, mem_v, kv_base, carried, prev, layer, depth, lw, *, prompt):
    gains = lw["gains"]
    n = x.shape[0]
    tm = min(512, n)
    tm_in = min(1024, n)
    conv_args = {}
    if prompt:
        conv_args = dict(conv_w=lw["ssd_conv_w"], conv_b=lw["ssd_conv_b"], seq_len=length)
    proj, gates, dt_raw, tails = inproj(x, gains[0:1], lw["w_in_all"], lw["w_gate"], lw["w_dt"], layer, tm=tm_in,
                                        **conv_args)
    ret_act, ret_all = retention(proj, lw["ret_gn"], batch, length, pos0, state_all=carried["ret"],
                                 prev_out=prev["ret"], layer=layer, depth=depth)
    if prompt:
        s5_act, s5_r_new, s5_i_new = s5_prompt(proj, lw["s5_d"], lw["s5_mats"], batch, length)
    else:
        s5_act, s5_r_new, s5_i_new = s5_sample(proj, lw["s5_d"], lw["s5_mats"], carried["s5_r"], carried["s5_i"],
                                               batch, length)
    if prompt:
        ssd_act, ssd_all = ssd_wide(proj, dt_raw, lw["ssd_dt_bias"], lw["ssd_a_log"], lw["ssd_d"], lw["ssd_norm"],
                                    batch, length, h_all=carried["ssd"], prev_h=prev["ssd"], layer=layer, depth=depth)
        tiles_per_seq = length // tm_in
        seq_tails = tails.reshape(batch, tiles_per_seq, _CONV_PAD, SSD_CONV_DIM)[:, -1, _CONV_PAD - (SSD_CONV_W - 1):]
        conv_all = seq_tails if prev["conv"] is None else jnp.concatenate([prev["conv"], seq_tails], axis=0)
    else:
        ssd_act, ssd_all, conv_all = ssd(proj, dt_raw, lw["ssd_conv_w"], lw["ssd_conv_b"], lw["ssd_dt_bias"],
                                         lw["ssd_a_log"], lw["ssd_d"], lw["ssd_norm"], batch, length,
                                         h_all=carried["ssd"], conv_all=carried["conv"], prev_h=prev["ssd"],
                                         prev_conv=prev["conv"], layer=layer, depth=depth)
    merged = merge(ret_act, s5_act, ssd_act, gates, lw["w_ret_o"], lw["w_s5_glu"], lw["w_ssd_out"], tm=tm, tn=512)
    x = matmul_norm_res(merged, lw["w_mix_out"], gains[1:2], x, tm=tm)
    q = norm_matmul(x, gains[2:3], lw["w_xq"], tm=tm_in, tn=512, out_dtype=bf16)
    att = attention(q, mem_k, mem_v, batch, length, tq=min(length, 512), kv_base=kv_base)
    x = matmul_norm_res(att, lw["w_xo"], gains[3:4], x, tm=tm)
    x = mlp(x, gains[4:5], gains[5:6], lw["w_up"], lw["w_down"], tm=tm, tf=1024)
    return x, dict(ret=ret_all, ssd=ssd_all, conv=conv_all), (s5_r_new, s5_i_new)


def kernel(x_prompt, x_sample, mem_prompt, state_ret, state_s5_re, state_s5_im, state_ssd, cache_ssd_conv,
           cache_mem_k, cache_mem_v, norm_gains, w_in, ret_gn, w_ret_o, s5_a_re, s5_a_im, s5_b_re, s5_b_im,
           s5_c_re, s5_c_im, s5_d, s5_log_dt, w_s5_glu, ssd_conv_w, ssd_conv_b, ssd_dt_bias, ssd_a_log, ssd_d,
           ssd_norm, w_ssd_out, w_mix_out, w_xq, w_xkv, w_xo, w_up, w_down):
    b, l, _ = x_prompt.shape
    db, dl, _ = x_sample.shape
    depth = w_in.shape[0]
    yp = x_prompt.reshape(b * l, D_MODEL)
    ys = x_sample.reshape(db * dl, D_MODEL)
    mem2d = mem_prompt.reshape(b * MEM_LEN, D_MODEL)
    cache_k = cache_mem_k.reshape(depth * db, MEM_LEN, D_MODEL)
    cache_v = cache_mem_v.reshape(depth * db, MEM_LEN, D_MODEL)
    s_carried = dict(ret=state_ret.reshape(depth * db, R_HEADS, R_DK, R_DV),
                     ssd=state_ssd.reshape(depth * db, SSD_HEADS, SSD_HEADDIM, SSD_STATE),
                     conv=cache_ssd_conv.reshape(depth * db, SSD_CONV_W - 1, SSD_CONV_DIM))
    p_carried = dict(ret=None, ssd=None, conv=None)
    p_prev = dict(ret=None, ssd=None, conv=None)
    s_prev = dict(ret=None, ssd=None, conv=None)
    p_s5, s_s5, p_mk, p_mv = [], [], [], []
    w_in_bf = w_in.astype(bf16)
    for i in range(depth):
        ar, ai, lr, li, bbr, bbi = s5_prep(s5_a_re[i], s5_a_im[i], s5_log_dt[i],
                                           jnp.swapaxes(s5_b_re[i], 1, 2), jnp.swapaxes(s5_b_im[i], 1, 2))
        lane = lambda t: t.reshape(1, S5_LANES)
        lw = dict(
            gains=norm_gains[i],
            w_in_all=w_in_bf,
            w_dt=w_in_bf[i, :, OFF_DT:OFF_GATE],
            w_gate=w_in_bf[i, :, OFF_GATE:],
            ret_gn=ret_gn[i].reshape(1, R_VAL),
            w_ret_o=w_ret_o[i].astype(bf16),
            s5_d=s5_d[i].reshape(1, D_MODEL),
            s5_mats=(_s5_blockdiag_in(bbr), _s5_blockdiag_in(bbi),
                     _s5_blockdiag_out(s5_c_re[i]), _s5_blockdiag_out(s5_c_im[i]),
                     lane(ar), lane(ai), lane(lr), lane(li)),
            w_s5_glu=w_s5_glu[i].astype(bf16),
            ssd_conv_w=ssd_conv_w[i], ssd_conv_b=ssd_conv_b[i], ssd_dt_bias=ssd_dt_bias[i],
            ssd_a_log=ssd_a_log[i], ssd_d=ssd_d[i], ssd_norm=ssd_norm[i],
            w_ssd_out=w_ssd_out[i].astype(bf16),
            w_mix_out=w_mix_out[i].astype(bf16),
            w_xq=w_xq[i].astype(bf16),
            w_xo=w_xo[i].astype(bf16),
            w_up=w_up[i].astype(bf16),
            w_down=w_down[i].astype(bf16),
        )
        kv = norm_matmul(mem2d, norm_gains[i, 6:7], w_xkv[i].astype(bf16), tm=b * MEM_LEN, tn=512, out_dtype=f32)
        mk = kv[:, :D_MODEL].reshape(b, MEM_LEN, D_MODEL)
        mv = kv[:, D_MODEL:].reshape(b, MEM_LEN, D_MODEL)
        p_mk.append(mk.reshape(b, MEM_LEN, X_HEADS, X_HEAD_DIM))
        p_mv.append(mv.reshape(b, MEM_LEN, X_HEADS, X_HEAD_DIM))
        yp, p_prev, s5p = _run_layer(yp, b, l, 0, mk, mv, 0, p_carried, p_prev, i, depth, lw, prompt=True)
        p_s5.append(s5p)
        s_carried_i = dict(s_carried, s5_r=state_s5_re[i], s5_i=state_s5_im[i])
        ys, s_prev, s5s = _run_layer(ys, db, dl, PAST_LEN, cache_k, cache_v, i * db, s_carried_i, s_prev, i, depth,
                                     lw, prompt=False)
        s_s5.append(s5s)
    p_s5_re, p_s5_im = [jnp.stack(t) for t in zip(*p_s5)]
    s_s5_re, s_s5_im = [jnp.stack(t) for t in zip(*s_s5)]
    unstack = lambda t, bsz: t.reshape((depth, bsz) + t.shape[1:])
    return (yp.reshape(b, l, D_MODEL), ys.reshape(db, dl, D_MODEL),
            unstack(p_prev["ret"], b), p_s5_re, p_s5_im, unstack(p_prev["ssd"], b), unstack(p_prev["conv"], b),
            jnp.stack(p_mk), jnp.stack(p_mv),
            unstack(s_prev["ret"], db), s_s5_re, s_s5_im, unstack(s_prev["ssd"], db), unstack(s_prev["conv"], db))
```

```python
import functools
import math

import jax
import jax.numpy as jnp
from jax import lax
from jax.experimental import pallas as pl
from jax.experimental.pallas import tpu as pltpu

f32 = jnp.float32
bf16 = jnp.bfloat16

D_MODEL = 2048
PAST_LEN = 1024
CHUNK = 64
EPS = 1e-6
R_HEADS = 8
R_DK = 128
R_DV = 256
R_QK = R_HEADS * R_DK
R_VAL = R_HEADS * R_DV
ROPE_BASE = 10000.0
S5_GROUP = 16
S5_GROUPS = D_MODEL // S5_GROUP
S5_STATE = 64
S5_LANES = S5_GROUPS * S5_STATE
S5_GT = 16
S5_NGT = S5_GROUPS // S5_GT
S5_TU = S5_GT * S5_GROUP
S5_TL = S5_GT * S5_STATE
SSD_INNER = 2 * D_MODEL
SSD_HEADDIM = 64
SSD_HEADS = SSD_INNER // SSD_HEADDIM
SSD_GROUPS = 8
SSD_HPG = SSD_HEADS // SSD_GROUPS
SSD_STATE = 128
SSD_CONV_W = 4
SSD_CONV_DIM = SSD_INNER + 2 * SSD_GROUPS * SSD_STATE
MEM_LEN = 256
X_HEADS = 4
X_HEAD_DIM = D_MODEL // X_HEADS
D_FF = 4 * D_MODEL
OFF_Q = 0
OFF_K = OFF_Q + R_QK
OFF_V = OFF_K + R_QK
OFF_G = OFF_V + R_VAL
OFF_U = OFF_G + R_VAL
OFF_Z = OFF_U + D_MODEL
OFF_XBC = OFF_Z + SSD_INNER
OFF_DT = OFF_XBC + SSD_CONV_DIM
OFF_GATE = OFF_DT + SSD_HEADS
MAIN_COLS = OFF_DT

VMEM_LIMIT_BYTES = 52 * 1024 * 1024


def _cparams(*sem):
    return pltpu.CompilerParams(dimension_semantics=sem, vmem_limit_bytes=VMEM_LIMIT_BYTES)


def _rms(x, g):
    return x * lax.rsqrt(jnp.mean(x * x, axis=-1, keepdims=True) + EPS) * g


def _dot(a, b):
    return jnp.dot(a, b, preferred_element_type=f32)


def _dot_nt(a, b):
    return lax.dot_general(a, b, (((1,), (1,)), ((), ())), preferred_element_type=f32)


def _dot_tn(a, b):
    return lax.dot_general(a, b, (((0,), (0,)), ((), ())), preferred_element_type=f32)


def _sigmoid(x):
    return 1.0 / (1.0 + jnp.exp(-x))


def _norm_matmul_kernel(x_ref, g_ref, w_ref, o_ref, hn_ref, *, sigmoid):
    @pl.when(pl.program_id(1) == 0)
    def _():
        hn_ref[...] = _rms(x_ref[...], g_ref[...]).astype(bf16)

    y = _dot(hn_ref[...], w_ref[...])
    if sigmoid:
        y = _sigmoid(y)
    o_ref[...] = y.astype(o_ref.dtype)


def norm_matmul(x, g, w, *, tm, tn, out_dtype, sigmoid=False):
    n, k = x.shape
    m = w.shape[1]
    return pl.pallas_call(
        functools.partial(_norm_matmul_kernel, sigmoid=sigmoid),
        grid=(n // tm, m // tn),
        in_specs=[pl.BlockSpec((tm, k), lambda i, j: (i, 0)),
                  pl.BlockSpec((1, k), lambda i, j: (0, 0)),
                  pl.BlockSpec((k, tn), lambda i, j: (0, j))],
        out_specs=pl.BlockSpec((tm, tn), lambda i, j: (i, j)),
        out_shape=jax.ShapeDtypeStruct((n, m), out_dtype),
        scratch_shapes=[pltpu.VMEM((tm, k), bf16)],
        compiler_params=_cparams("parallel", "arbitrary"),
        name="norm_matmul",
    )(x, g, w)


IN_TN = 1024
_J_G = (OFF_G // IN_TN, OFF_U // IN_TN)
_J_Z = (OFF_Z // IN_TN, OFF_XBC // IN_TN)
_J_XBC = OFF_XBC // IN_TN
_NJ_MAIN = MAIN_COLS // IN_TN
_CONV_PAD = 8
_IN_SUB = 256


def _inproj_kernel(*refs, tm, conv, tiles_per_seq):
    if conv:
        x_ref, g_ref, w_ref, cw_ref, cb_ref, o_ref, tails_ref, hn_ref, tail_scr = refs
    else:
        x_ref, g_ref, w_ref, o_ref, hn_ref = refs
    i = pl.program_id(0)
    j = pl.program_id(1)

    @pl.when(j == 0)
    def _():
        hn_ref[...] = _rms(x_ref[...], g_ref[...]).astype(bf16)

    is_silu = ((j >= _J_G[0]) & (j < _J_G[1])) | ((j >= _J_Z[0]) & (j < _J_Z[1]))
    is_plain = jnp.logical_not(is_silu)
    if conv:
        is_plain = is_plain & (j < _J_XBC)
    subs = [slice(s * _IN_SUB, (s + 1) * _IN_SUB) for s in range(IN_TN // _IN_SUB)]

    @pl.when(is_silu)
    def _():
        for cs in subs:
            acc = _dot(hn_ref[...], w_ref[:, cs])
            o_ref[:, cs] = acc * _sigmoid(acc)

    @pl.when(is_plain)
    def _():
        o_ref[...] = _dot(hn_ref[...], w_ref[...])

    if conv:
        @pl.when(j >= _J_XBC)
        def _():
            jj = j - _J_XBC

            @pl.when(i % tiles_per_seq == 0)
            def _():
                tail_scr[jj] = jnp.zeros((_CONV_PAD, IN_TN), f32)

            for cs in subs:
                acc = _dot(hn_ref[...], w_ref[:, cs])
                ext = jnp.concatenate([tail_scr[jj, :, cs], acc], axis=0)
                h = cw_ref[0:1, cs] * ext
                for t in range(1, SSD_CONV_W):
                    h = pltpu.roll(h, 1, 0) + cw_ref[t:t + 1, cs] * ext
                xc = h[_CONV_PAD:, :] + cb_ref[:, cs]
                o_ref[:, cs] = xc * _sigmoid(xc)
                last = acc[tm - _CONV_PAD:, :]
                tail_scr[jj, :, cs] = last
                tails_ref[0, :, cs] = last


def inproj(x, g, w_all, layer, *, tm, conv_w=None, conv_b=None, seq_len=None):
    n, k = x.shape
    conv = conv_w is not None
    nxbc = SSD_CONV_DIM // IN_TN
    in_specs = [pl.BlockSpec((tm, k), lambda i, j: (i, 0)),
                pl.BlockSpec((1, k), lambda i, j: (0, 0)),
                pl.BlockSpec((None, k, IN_TN), lambda i, j: (layer, 0, j))]
    out_specs = pl.BlockSpec((tm, IN_TN), lambda i, j: (i, j))
    out_shape = jax.ShapeDtypeStruct((n, MAIN_COLS), f32)
    scratch = [pltpu.VMEM((tm, k), bf16)]
    args = [x, g, w_all]
    tiles_per_seq = None
    if conv:
        xcol = lambda i, j: (0, jnp.maximum(j - _J_XBC, 0))
        in_specs += [pl.BlockSpec((SSD_CONV_W, IN_TN), xcol), pl.BlockSpec((1, IN_TN), xcol)]
        out_specs = [out_specs, pl.BlockSpec((1, _CONV_PAD, IN_TN), lambda i, j: (i, 0, jnp.maximum(j - _J_XBC, 0)))]
        out_shape = [out_shape, jax.ShapeDtypeStruct((n // tm, _CONV_PAD, SSD_CONV_DIM), f32)]
        scratch += [pltpu.VMEM((nxbc, _CONV_PAD, IN_TN), f32)]
        args += [conv_w, conv_b.reshape(1, -1)]
        tiles_per_seq = seq_len // tm
    kern = functools.partial(_inproj_kernel, tm=tm, conv=conv, tiles_per_seq=tiles_per_seq)
    out = pl.pallas_call(
        kern,
        grid=(n // tm, _NJ_MAIN),
        in_specs=in_specs,
        out_specs=out_specs,
        out_shape=out_shape,
        scratch_shapes=scratch,
        compiler_params=_cparams("arbitrary", "arbitrary"),
        name="inproj",
    )(*args)
    return out if conv else (out, None)


def _matmul_norm_res_kernel(a_ref, w_ref, g_ref, x_ref, o_ref):
    y = _dot(a_ref[...], w_ref[...])
    o_ref[...] = x_ref[...] + _rms(y, g_ref[...])


def matmul_norm_res(a, w, g, x, *, tm):
    n, k = a.shape
    d = w.shape[1]
    return pl.pallas_call(
        _matmul_norm_res_kernel,
        grid=(n // tm,),
        in_specs=[pl.BlockSpec((tm, k), lambda i: (i, 0)),
                  pl.BlockSpec((k, d), lambda i: (0, 0)),
                  pl.BlockSpec((1, d), lambda i: (0, 0)),
                  pl.BlockSpec((tm, d), lambda i: (i, 0))],
        out_specs=pl.BlockSpec((tm, d), lambda i: (i, 0)),
        out_shape=jax.ShapeDtypeStruct((n, d), f32),
        compiler_params=_cparams("parallel"),
        name="matmul_norm_res",
    )(a, w, g, x)


def _layered_state_specs(layer_rows, tail, state_all, prev_out):
    zeros = (0,) * len(tail)
    spec = pl.BlockSpec((1,) + tail, lambda b, c: (layer_rows + b,) + zeros)
    ins, specs = [], []
    if state_all is not None:
        ins.append(state_all)
        specs.append(spec)
    if prev_out is not None:
        ins.append(prev_out)
        specs.append(pl.BlockSpec(memory_space=pl.ANY))
    return ins, specs, spec


def _retention_kernel(*refs, lc, per_step, nsteps, has_init, n_alias):
    q_ref, k_ref, v_ref, g_ref, cos_ref, sin_ref, intra_ref, din_ref, dup_ref, dall_ref, gn_ref = refs[:11]
    s0_ref = refs[11] if has_init else None
    o_ref, snew_ref, s_scr = refs[11 + has_init + n_alias:]
    c = pl.program_id(1)

    @pl.when(c == 0)
    def _():
        if has_init:
            s_scr[...] = s0_ref[0]
        else:
            s_scr[...] = jnp.zeros_like(s_scr)

    for cc, h in [(cc, h) for cc in range(per_step) for h in range(R_HEADS)]:
        rows = slice(cc * lc, (cc + 1) * lc)
        cos = cos_ref[rows, :]
        sin = sin_ref[rows, :]
        q = q_ref[rows, h * R_DK:(h + 1) * R_DK]
        k = k_ref[rows, h * R_DK:(h + 1) * R_DK]
        v = v_ref[rows, h * R_DV:(h + 1) * R_DV].astype(bf16)
        qr = (q * cos + pltpu.roll(q, R_DK // 2, 1) * sin)
        kr = (k * cos + pltpu.roll(k, R_DK // 2, 1) * sin) * (R_DK ** -0.5)
        qb = qr.astype(bf16)
        s = s_scr[h]
        scores = _dot_nt(qb, kr.astype(bf16)) * intra_ref[h]
        out = _dot(scores.astype(bf16), v) + _dot(qb, s.astype(bf16)) * din_ref[h]
        s_scr[h] = dall_ref[h] * s + _dot_tn((kr * dup_ref[h]).astype(bf16), v)
        mu = jnp.mean(out, axis=-1, keepdims=True)
        cen = out - mu
        var = jnp.mean(cen * cen, axis=-1, keepdims=True)
        o = cen * lax.rsqrt(var + EPS) * gn_ref[:, h * R_DV:(h + 1) * R_DV]
        o_ref[rows, h * R_DV:(h + 1) * R_DV] = (g_ref[rows, h * R_DV:(h + 1) * R_DV] * o).astype(bf16)

    @pl.when(c == nsteps - 1)
    def _():
        snew_ref[0] = s_scr[...]


def retention(proj, gn, batch, length, pos0, *, state_all, prev_out, layer, depth):
    lc = min(length, CHUNK)
    per_step = 2 if length % (2 * lc) == 0 else 1
    rows = per_step * lc
    nc = length // rows
    half = R_DK // 2
    inv_freq = jnp.exp(-math.log(ROPE_BASE) * jnp.arange(half, dtype=f32) / half)
    pos = (pos0 + jnp.arange(length, dtype=jnp.int32)).astype(f32)
    ang = pos[:, None] * inv_freq[None]
    cos2 = jnp.concatenate([jnp.cos(ang), jnp.cos(ang)], axis=-1)
    sin2 = jnp.concatenate([-jnp.sin(ang), jnp.sin(ang)], axis=-1)
    log_gamma = jnp.log1p(-jnp.exp2(-5.0 - jnp.arange(R_HEADS, dtype=f32)))
    idx = jnp.arange(lc, dtype=f32)
    lg = log_gamma[:, None]
    intra = jnp.exp(jnp.abs(idx[:, None] - idx[None, :])[None] * lg[:, :, None])
    din = jnp.broadcast_to(jnp.exp((idx + 1.0)[None] * lg)[:, :, None], (R_HEADS, lc, R_DV))
    dup = jnp.broadcast_to(jnp.exp((lc - 1.0 - idx)[None] * lg)[:, :, None], (R_HEADS, lc, R_DK))
    dall = jnp.exp(lc * log_gamma)
    tail = (R_HEADS, R_DK, R_DV)
    st_in, st_specs, st_out_spec = _layered_state_specs(layer * batch, tail, state_all, prev_out)
    n_fixed = 11
    kern = functools.partial(_retention_kernel, lc=lc, per_step=per_step, nsteps=nc,
                             has_init=state_all is not None, n_alias=int(prev_out is not None))
    aliases = {n_fixed + len(st_in) - 1: 1} if prev_out is not None else {}
    return pl.pallas_call(
        kern,
        grid=(batch, nc),
        in_specs=[pl.BlockSpec((rows, R_QK), lambda b, c: (b * nc + c, OFF_Q // R_QK)),
                  pl.BlockSpec((rows, R_QK), lambda b, c: (b * nc + c, OFF_K // R_QK)),
                  pl.BlockSpec((rows, R_VAL), lambda b, c: (b * nc + c, OFF_V // R_VAL)),
                  pl.BlockSpec((rows, R_VAL), lambda b, c: (b * nc + c, OFF_G // R_VAL)),
                  pl.BlockSpec((rows, R_DK), lambda b, c: (c, 0)),
                  pl.BlockSpec((rows, R_DK), lambda b, c: (c, 0)),
                  pl.BlockSpec((R_HEADS, lc, lc), lambda b, c: (0, 0, 0)),
                  pl.BlockSpec((R_HEADS, lc, R_DV), lambda b, c: (0, 0, 0)),
                  pl.BlockSpec((R_HEADS, lc, R_DK), lambda b, c: (0, 0, 0)),
                  pl.BlockSpec(memory_space=pltpu.SMEM),
                  pl.BlockSpec((1, R_VAL), lambda b, c: (0, 0))] + st_specs,
        out_specs=[pl.BlockSpec((rows, R_VAL), lambda b, c: (b * nc + c, 0)), st_out_spec],
        out_shape=[jax.ShapeDtypeStruct((batch * length, R_VAL), bf16),
                   jax.ShapeDtypeStruct((depth * batch,) + tail, f32)],
        scratch_shapes=[pltpu.VMEM(tail, f32)],
        input_output_aliases=aliases,
        compiler_params=_cparams("parallel", "arbitrary"),
        name="retention",
    )(proj, proj, proj, proj, cos2, sin2, intra, din, dup, dall, gn, *st_in)


def _s5_prep_kernel(are_ref, aim_ref, ldt_ref, bre_ref, bim_ref, ar_ref, ai_ref, lr_ref, li_ref, bbr_ref, bbi_ref):
    a_re = are_ref[...]
    a_im = aim_ref[...]
    dt = jnp.exp(ldt_ref[...])
    lr = dt * a_re
    li = dt * a_im
    mag = jnp.exp(lr)
    ar = mag * jnp.cos(li)
    ai = mag * jnp.sin(li)
    nr = ar - 1.0
    den = a_re * a_re + a_im * a_im
    cr = (nr * a_re + ai * a_im) / den
    ci = (ai * a_re - nr * a_im) / den
    ar_ref[...] = ar
    ai_ref[...] = ai
    lr_ref[...] = lr
    li_ref[...] = li
    b_re = bre_ref[...]
    b_im = bim_ref[...]
    bbr_ref[...] = cr[:, None, :] * b_re - ci[:, None, :] * b_im
    bbi_ref[...] = cr[:, None, :] * b_im + ci[:, None, :] * b_re


def s5_prep(a_re, a_im, log_dt, b_re, b_im):
    gp = jax.ShapeDtypeStruct((S5_GROUPS, S5_STATE), f32)
    gb = jax.ShapeDtypeStruct((S5_GROUPS, S5_GROUP, S5_STATE), f32)
    return pl.pallas_call(
        _s5_prep_kernel,
        out_shape=[gp, gp, gp, gp, gb, gb],
        compiler_params=pltpu.CompilerParams(vmem_limit_bytes=VMEM_LIMIT_BYTES),
        name="s5_prep",
    )(a_re, a_im, log_dt.reshape(S5_GROUPS, 1), b_re, b_im)


def _s5_project_in(u_ref, perm_ref, bre_ref, bim_ref, bur_scr, bui_scr):
    ub = u_ref[...].astype(bf16)
    up = _dot(perm_ref[...], ub).astype(bf16)
    bur_scr[...] = _dot(up, bre_ref[0])
    bui_scr[...] = _dot(up, bim_ref[0])


def _s5_scan(bur_scr, bui_scr, ar_row, ai_row, init_r, init_i, nseq, steps):
    lw = min(S5_TL, S5_LANES // nseq)
    fin_r, fin_i = [], []
    for lb in range(S5_TL // lw):
        sl = slice(lb * lw, (lb + 1) * lw)
        ar = jnp.broadcast_to(ar_row[:, sl], (nseq, lw))
        ai = jnp.broadcast_to(ai_row[:, sl], (nseq, lw))
        xr = init_r[:, sl]
        xi = init_i[:, sl]
        for t in range(steps):
            rows = slice(t * nseq, (t + 1) * nseq)
            nxr = ar * xr - ai * xi + bur_scr[rows, sl]
            nxi = ar * xi + ai * xr + bui_scr[rows, sl]
            xr, xi = nxr, nxi
            bur_scr[rows, sl] = xr
            bui_scr[rows, sl] = xi
        fin_r.append(xr)
        fin_i.append(xi)
    return fin_r, fin_i, lw


def _s5_output(u_ref, xr, xi, cre_ref, cim_ref, d_ref, permt_ref, y_ref):
    y = _dot(xr, cre_ref[0]) - _dot(xi, cim_ref[0])
    pt = permt_ref[...]
    y0, y1, y2 = _split3(y)
    y = _dot(pt, y0) + _dot(pt, y1) + _dot(pt, y2)
    y_ref[...] = jax.nn.gelu(y + d_ref[...] * u_ref[...]).astype(bf16)


def _s5_prompt_kernel(u_ref, d_ref, bre_ref, bim_ref, cre_ref, cim_ref, ar_ref, ai_ref, lr_ref, li_ref,
                      perm_ref, permt_ref, y_ref, hr_ref, hi_ref,
                      bur_scr, bui_scr, xbr_scr, xbi_scr, apr_scr, api_scr, xsr_scr, xsi_scr, car_scr, cai_scr,
                      *, nseq, steps, nsc):
    b = pl.program_id(1)
    sc = pl.program_id(2)

    @pl.when((b == 0) & (sc == 0))
    def _():
        tt = (lax.broadcasted_iota(jnp.int32, (steps, S5_TL), 0) + 1).astype(f32)
        mag = jnp.exp(tt * lr_ref[...])
        pwr = mag * jnp.cos(tt * li_ref[...])
        pwi = mag * jnp.sin(tt * li_ref[...])
        for t in range(steps):
            apr_scr[t * nseq:(t + 1) * nseq, :] = jnp.broadcast_to(pwr[t:t + 1, :], (nseq, S5_TL))
            api_scr[t * nseq:(t + 1) * nseq, :] = jnp.broadcast_to(pwi[t:t + 1, :], (nseq, S5_TL))

    @pl.when(sc == 0)
    def _():
        car_scr[...] = jnp.zeros_like(car_scr)
        cai_scr[...] = jnp.zeros_like(cai_scr)

    _s5_project_in(u_ref, perm_ref, bre_ref, bim_ref, bur_scr, bui_scr)
    zero = jnp.zeros((nseq, S5_TL), f32)
    _s5_scan(bur_scr, bui_scr, ar_ref[...], ai_ref[...], zero, zero, nseq, steps)
    pr = apr_scr[(steps - 1) * nseq:(steps - 1) * nseq + 1, :]
    pi = api_scr[(steps - 1) * nseq:(steps - 1) * nseq + 1, :]
    xr = car_scr[...]
    xi = cai_scr[...]
    for r in range(nseq):
        xsr_scr[r:r + 1, :] = xr
        xsi_scr[r:r + 1, :] = xi
        er = bur_scr[(steps - 1) * nseq + r:(steps - 1) * nseq + r + 1, :]
        ei = bui_scr[(steps - 1) * nseq + r:(steps - 1) * nseq + r + 1, :]
        xr, xi = er + pr * xr - pi * xi, ei + pr * xi + pi * xr
    car_scr[...] = xr
    cai_scr[...] = xi
    blk = 2 * nseq
    lq = S5_TL // 4
    for q in range(S5_TL // lq):
        sl = slice(q * lq, (q + 1) * lq)
        xsr = jnp.concatenate([xsr_scr[:, sl], xsr_scr[:, sl]], axis=0)
        xsi = jnp.concatenate([xsi_scr[:, sl], xsi_scr[:, sl]], axis=0)
        for t2 in range(nseq * steps // blk):
            rows = slice(t2 * blk, (t2 + 1) * blk)
            ptr = apr_scr[rows, sl]
            pti = api_scr[rows, sl]
            xbr_scr[rows, sl] = (bur_scr[rows, sl] + ptr * xsr - pti * xsi).astype(bf16)
            xbi_scr[rows, sl] = (bui_scr[rows, sl] + ptr * xsi + pti * xsr).astype(bf16)
    _s5_output(u_ref, xbr_scr[...], xbi_scr[...], cre_ref, cim_ref, d_ref, permt_ref, y_ref)

    @pl.when(sc == nsc - 1)
    def _():
        hr_ref[0] = car_scr[...]
        hi_ref[0] = cai_scr[...]


def _s5_sample_kernel(u_ref, d_ref, bre_ref, bim_ref, cre_ref, cim_ref, ar_ref, ai_ref, perm_ref, permt_ref,
                      h0r_ref, h0i_ref, y_ref, hr_ref, hi_ref, bur_scr, bui_scr, *, nseq, steps):
    _s5_project_in(u_ref, perm_ref, bre_ref, bim_ref, bur_scr, bui_scr)
    fin_r, fin_i, lw = _s5_scan(bur_scr, bui_scr, ar_ref[...], ai_ref[...], h0r_ref[...], h0i_ref[...], nseq, steps)
    for lb in range(len(fin_r)):
        hr_ref[:, lb * lw:(lb + 1) * lw] = fin_r[lb]
        hi_ref[:, lb * lw:(lb + 1) * lw] = fin_i[lb]
    _s5_output(u_ref, bur_scr[...].astype(bf16), bui_scr[...].astype(bf16), cre_ref, cim_ref, d_ref, permt_ref,
               y_ref)


def _s5_blockdiag_in(bb):
    t = bb.reshape(S5_NGT, S5_GT, S5_GROUP, S5_STATE)
    eye = jnp.eye(S5_GT, dtype=bool)
    out = jnp.where(eye[None, :, None, :, None], t[:, :, :, None, :], 0.0)
    return out.reshape(S5_NGT, S5_TU, S5_TL).astype(bf16)


def _s5_blockdiag_out(c):
    t = jnp.transpose(c.reshape(S5_NGT, S5_GT, S5_GROUP, S5_STATE), (0, 1, 3, 2))
    eye = jnp.eye(S5_GT, dtype=bool)
    out = jnp.where(eye[None, :, None, :, None], t[:, :, :, None, :], 0.0)
    return out.reshape(S5_NGT, S5_TL, S5_TU).astype(bf16)


def _s5_perm(nseq, steps):
    n = nseq * steps
    dst = jnp.arange(n)
    src = (dst % nseq) * steps + dst // nseq
    p = src[:, None] == jnp.arange(n)[None, :]
    return jnp.where(p, 1.0, 0.0).astype(bf16), jnp.where(p.T, 1.0, 0.0).astype(bf16)


def _s5_common_specs(rows, row_map):
    lane = lambda *a: (0, a[0])
    mat = lambda *a: (a[0], 0, 0)
    return [pl.BlockSpec((rows, S5_TU), row_map),
            pl.BlockSpec((1, S5_TU), lane),
            pl.BlockSpec((1, S5_TU, S5_TL), mat),
            pl.BlockSpec((1, S5_TU, S5_TL), mat),
            pl.BlockSpec((1, S5_TL, S5_TU), mat),
            pl.BlockSpec((1, S5_TL, S5_TU), mat),
            pl.BlockSpec((1, S5_TL), lane),
            pl.BlockSpec((1, S5_TL), lane)]


def s5_prompt(proj, d, mats, batch, length):
    bre, bim, cre, cim, ar, ai, lr, li = mats
    nseq, steps = 8, CHUNK
    rows = nseq * steps
    nsc = length // rows
    ucol = OFF_U // S5_TU
    kern = functools.partial(_s5_prompt_kernel, nseq=nseq, steps=steps, nsc=nsc)
    lane = lambda g, b, s: (0, g)
    const = lambda g, b, s: (0, 0)
    perm, permt = _s5_perm(nseq, steps)
    st = jax.ShapeDtypeStruct((batch, 1, S5_LANES), f32)
    y, hr, hi = pl.pallas_call(
        kern,
        grid=(S5_NGT, batch, nsc),
        in_specs=_s5_common_specs(rows, lambda g, b, s: (b * nsc + s, ucol + g))
        + [pl.BlockSpec((1, S5_TL), lane), pl.BlockSpec((1, S5_TL), lane),
           pl.BlockSpec((rows, rows), const), pl.BlockSpec((rows, rows), const)],
        out_specs=[pl.BlockSpec((rows, S5_TU), lambda g, b, s: (b * nsc + s, g)),
                   pl.BlockSpec((1, 1, S5_TL), lambda g, b, s: (b, 0, g)),
                   pl.BlockSpec((1, 1, S5_TL), lambda g, b, s: (b, 0, g))],
        out_shape=[jax.ShapeDtypeStruct((batch * length, D_MODEL), bf16), st, st],
        scratch_shapes=[pltpu.VMEM((rows, S5_TL), f32), pltpu.VMEM((rows, S5_TL), f32),
                        pltpu.VMEM((rows, S5_TL), bf16), pltpu.VMEM((rows, S5_TL), bf16),
                        pltpu.VMEM((rows, S5_TL), f32), pltpu.VMEM((rows, S5_TL), f32),
                        pltpu.VMEM((nseq, S5_TL), f32), pltpu.VMEM((nseq, S5_TL), f32),
                        pltpu.VMEM((1, S5_TL), f32), pltpu.VMEM((1, S5_TL), f32)],
        compiler_params=_cparams("arbitrary", "arbitrary", "arbitrary"),
        name="s5_prompt",
    )(proj, d, bre, bim, cre, cim, ar, ai, lr, li, perm, permt)
    return y, hr.reshape(batch, S5_GROUPS, S5_STATE), hi.reshape(batch, S5_GROUPS, S5_STATE)


def s5_sample(proj, d, mats, h0r, h0i, batch, length):
    bre, bim, cre, cim, ar, ai, _, _ = mats
    rows = batch * length
    ucol = OFF_U // S5_TU
    kern = functools.partial(_s5_sample_kernel, nseq=batch, steps=length)
    perm, permt = _s5_perm(batch, length)
    st = jax.ShapeDtypeStruct((batch, S5_LANES), f32)
    state_spec = pl.BlockSpec((batch, S5_TL), lambda g: (0, g))
    perm_spec = pl.BlockSpec((rows, rows), lambda g: (0, 0))
    y, hr, hi = pl.pallas_call(
        kern,
        grid=(S5_NGT,),
        in_specs=_s5_common_specs(rows, lambda g: (0, ucol + g)) + [perm_spec, perm_spec, state_spec, state_spec],
        out_specs=[pl.BlockSpec((rows, S5_TU), lambda g: (0, g)), state_spec, state_spec],
        out_shape=[jax.ShapeDtypeStruct((rows, D_MODEL), bf16), st, st],
        scratch_shapes=[pltpu.VMEM((rows, S5_TL), f32), pltpu.VMEM((rows, S5_TL), f32)],
        compiler_params=_cparams("arbitrary"),
        name="s5_sample",
    )(proj, d, bre, bim, cre, cim, ar, ai, perm, permt, h0r.reshape(batch, S5_LANES), h0i.reshape(batch, S5_LANES))
    return y, hr.reshape(batch, S5_GROUPS, S5_STATE), hi.reshape(batch, S5_GROUPS, S5_STATE)


_SSD_PAD = 8
_SSD_TP = 128


def _split3(v):
    p0 = v.astype(bf16)
    r0 = v - p0.astype(f32)
    p1 = r0.astype(bf16)
    p2 = (r0 - p1.astype(f32)).astype(bf16)
    return p0, p1, p2


def _ssd_conv(c, nchunks, xbc_ref, conv0_ref, cw_ref, cb_ref, convnew_ref, stage, lc):
    keep = SSD_CONV_W - 1
    lo = _SSD_PAD - keep

    @pl.when(c == 0)
    def _():
        if conv0_ref is not None:
            stage[lo:_SSD_PAD, :] = conv0_ref[0]
        else:
            stage[0:_SSD_PAD, :] = jnp.zeros((_SSD_PAD, SSD_CONV_DIM), f32)

    stage[_SSD_PAD:_SSD_PAD + lc, :] = xbc_ref[...]
    xc = cb_ref[...]
    for j in range(SSD_CONV_W):
        xc = xc + cw_ref[j:j + 1, :] * stage[lo + j:lo + j + lc, :]
    xc = xc * _sigmoid(xc)
    tail = stage[lo + lc:_SSD_PAD + lc, :]

    @pl.when(c == nchunks - 1)
    def _():
        convnew_ref[0] = tail

    stage[lo:_SSD_PAD, :] = tail
    return xc


def _ssd_dt_cumsum(dt_raw, dtb_ref, alog_ref, lc):
    dtx = dt_raw + dtb_ref[...]
    dtv = jnp.maximum(dtx, 0.0) + jnp.log1p(jnp.exp(-jnp.abs(dtx)))
    da = dtv * (-jnp.exp(alog_ref[...]))
    ri = lax.broadcasted_iota(jnp.int32, (lc, lc), 0)
    ci = lax.broadcasted_iota(jnp.int32, (lc, lc), 1)
    tri = jnp.where(ri >= ci, 1.0, 0.0).astype(bf16)
    da3 = _split3(da)
    cs = _dot(tri, da3[0]) + _dot(tri, da3[1]) + _dot(tri, da3[2])
    return dtv, cs, da3


def _ssd_parse(refs, has_init, n_alias, n_fixed):
    fixed = refs[:n_fixed]
    conv0_ref, h0_ref = (refs[n_fixed], refs[n_fixed + 1]) if has_init else (None, None)
    rest = refs[n_fixed + 2 * has_init + n_alias:]
    return fixed, conv0_ref, h0_ref, rest


def _ssd_wide_kernel(*refs, nsteps, per_step, has_init, n_alias):
    lc = SSD_HEADDIM
    z_ref, xc_ref, dt_ref, dtb_ref, alog_ref, dexp_ref, ng_ref, rep_ref = refs[:8]
    h0_ref = refs[8] if has_init else None
    y_ref, hnew_ref, ht_scr, cm_scr, dx_scr = refs[8 + has_init + n_alias:]
    c = pl.program_id(1)

    @pl.when(c == 0)
    def _():
        if has_init:
            for h in range(SSD_HEADS):
                ht_scr[:, h * lc:(h + 1) * lc] = h0_ref[0, h].T
        else:
            ht_scr[...] = jnp.zeros_like(ht_scr)

    pw = 2 * SSD_HEADDIM
    gw = SSD_HPG * SSD_HEADDIM
    ti = lax.broadcasted_iota(jnp.int32, (lc, gw), 0)
    si = lax.broadcasted_iota(jnp.int32, (lc, gw), 1) % lc
    causal = ti >= si
    bdmask = (lax.broadcasted_iota(jnp.int32, (pw, pw), 0) // lc) == (lax.broadcasted_iota(jnp.int32, (pw, pw), 1) // lc)
    low_half = lax.broadcasted_iota(jnp.int32, (1, pw), 1) < lc
    for cc in range(per_step):
        _ssd_wide_chunk(slice(cc * lc, (cc + 1) * lc), z_ref, xc_ref, dt_ref, dtb_ref, alog_ref, dexp_ref, ng_ref,
                        rep_ref, y_ref, ht_scr, cm_scr, dx_scr, causal, bdmask, low_half)

    @pl.when(c == nsteps - 1)
    def _():
        for h in range(SSD_HEADS):
            hnew_ref[0, h] = ht_scr[:, h * lc:(h + 1) * lc].T


def _ssd_wide_chunk(rows, z_ref, xc_ref, dt_ref, dtb_ref, alog_ref, dexp_ref, ng_ref, rep_ref, y_ref, ht_scr, cm_scr,
                    dx_scr, causal, bdmask, low_half):
    lc = SSD_HEADDIM
    pw = 2 * SSD_HEADDIM
    gw = SSD_HPG * SSD_HEADDIM
    dtv, cs, da3 = _ssd_dt_cumsum(dt_ref[rows, :], dtb_ref, alog_ref, lc)
    upper2 = jnp.where(lax.broadcasted_iota(jnp.int32, (lc, pw), 0) <= lax.broadcasted_iota(jnp.int32, (lc, pw), 1) % lc,
                       1.0, 0.0).astype(bf16)
    cst2 = _dot_tn(da3[0], upper2) + _dot_tn(da3[1], upper2) + _dot_tn(da3[2], upper2)
    rep = rep_ref[...]
    c0, c1, c2 = _split3(cs)
    cm_scr[rows, :] = _dot(c0, rep) + _dot(c1, rep) + _dot(c2, rep)
    d0, d1, d2 = _split3(dtv)
    dx_scr[rows, :] = _dot(d0, rep) + _dot(d1, rep) + _dot(d2, rep)

    b_off, c_off = SSD_INNER, SSD_INNER + SSD_GROUPS * SSD_STATE
    for g in range(SSD_GROUPS):
        gs = slice(g * gw, (g + 1) * gw)
        cm = cm_scr[rows, gs]
        row = jnp.concatenate(
            [jnp.where(low_half, cst2[h0:h0 + 1, :], cst2[h0 + 1:h0 + 2, :])
             for h0 in range(g * SSD_HPG, (g + 1) * SSD_HPG, 2)], axis=1)
        decay = jnp.exp(jnp.where(causal, cm - row, -jnp.inf))
        last = cm[lc - 1:lc, :]
        xg = xc_ref[rows, gs]
        xdt = xg * dx_scr[rows, gs]
        bg = xc_ref[rows, b_off + g * SSD_STATE:b_off + (g + 1) * SSD_STATE].astype(bf16)
        cg = xc_ref[rows, c_off + g * SSD_STATE:c_off + (g + 1) * SSD_STATE].astype(bf16)
        cb2 = _dot_nt(cg, jnp.concatenate([bg, bg], axis=0))
        parts = []
        for pr in range(gw // pw):
            ps = slice(pr * pw, (pr + 1) * pw)
            m = (cb2 * decay[:, ps]).astype(bf16)
            xp = xdt[:, ps]
            bd = jnp.where(bdmask, jnp.concatenate([xp, xp], axis=0), 0.0).astype(bf16)
            parts.append(_dot(m, bd))
        y = jnp.concatenate(parts, axis=1)
        ht = ht_scr[:, gs]
        y = y + _dot(cg, ht.astype(bf16)) * jnp.exp(cm)
        xw = (xdt * jnp.exp(last - cm)).astype(bf16)
        ht_scr[:, gs] = jnp.exp(last) * ht + _dot_tn(bg, xw)
        y = y + dexp_ref[:, gs] * xg
        yg = y * z_ref[rows, gs]
        yg = yg * lax.rsqrt(jnp.mean(yg * yg, axis=-1, keepdims=True) + EPS) * ng_ref[:, gs]
        y_ref[rows, gs] = yg.astype(bf16)


def _ssd_kernel(*refs, lc, nchunks, has_init, n_alias):
    fixed, conv0_ref, h0_ref, rest = _ssd_parse(refs, has_init, n_alias, 9)
    z_ref, xbc_ref, dt_ref, cw_ref, cb_ref, dtb_ref, alog_ref, dexp_ref, ng_ref = fixed
    y_ref, hnew_ref, convnew_ref, stage, h_scr, pad_scr, y_scr = rest
    c = pl.program_id(1)

    @pl.when(c == 0)
    def _():
        if has_init:
            h_scr[...] = h0_ref[0]
        else:
            h_scr[...] = jnp.zeros_like(h_scr)

    xc = _ssd_conv(c, nchunks, xbc_ref, conv0_ref, cw_ref, cb_ref, convnew_ref, stage, lc)
    dtv, cs, _ = _ssd_dt_cumsum(dt_ref[...], dtb_ref, alog_ref, lc)
    causal = lax.broadcasted_iota(jnp.int32, (lc, lc), 0) >= lax.broadcasted_iota(jnp.int32, (lc, lc), 1)
    pad_scr[...] = jnp.zeros_like(pad_scr)
    pad_scr[0:lc, 0:SSD_HEADS] = cs
    cst = pad_scr[...].T
    ecs = jnp.exp(cs)
    cs_last = cs[lc - 1:lc, :]
    wdec = jnp.exp(cs_last - cs)
    elast = jnp.exp(cs_last)
    xs_off, b_off, c_off = 0, SSD_INNER, SSD_INNER + SSD_GROUPS * SSD_STATE
    for g in range(SSD_GROUPS):
        bg = xc[:, b_off + g * SSD_STATE:b_off + (g + 1) * SSD_STATE].astype(bf16)
        cg = xc[:, c_off + g * SSD_STATE:c_off + (g + 1) * SSD_STATE].astype(bf16)
        cb = _dot_nt(cg, bg)
        for r in range(SSD_HPG):
            h = g * SSD_HPG + r
            hs = slice(h * SSD_HEADDIM, (h + 1) * SSD_HEADDIM)
            col = cs[:, h:h + 1]
            row = cst[h:h + 1, 0:lc]
            decay = jnp.exp(jnp.where(causal, col - row, -jnp.inf))
            xh = xc[:, hs]
            xdt = xh * dtv[:, h:h + 1]
            hst = h_scr[h]
            y = _dot((cb * decay).astype(bf16), xdt.astype(bf16))
            y = y + _dot_nt(cg, hst.astype(bf16)) * ecs[:, h:h + 1]
            xw = (xdt * wdec[:, h:h + 1]).astype(bf16)
            h_scr[h] = elast[:, h:h + 1] * hst + _dot_tn(xw, bg)
            y_scr[:, hs] = y + dexp_ref[:, hs] * xh
        gs = slice(g * SSD_HPG * SSD_HEADDIM, (g + 1) * SSD_HPG * SSD_HEADDIM)
        yg = y_scr[:, gs] * z_ref[:, gs]
        yg = yg * lax.rsqrt(jnp.mean(yg * yg, axis=-1, keepdims=True) + EPS) * ng_ref[:, gs]
        y_ref[:, gs] = yg.astype(bf16)

    @pl.when(c == nchunks - 1)
    def _():
        hnew_ref[0] = h_scr[...]


def ssd(proj, dt_raw, conv_w, conv_b, dt_bias, a_log, d, norm_g, batch, length, *, h_all, conv_all, prev_h,
        prev_conv, layer, depth):
    lc = min(length, CHUNK)
    nc = length // lc
    dexp = jnp.repeat(d, SSD_HEADDIM).reshape(1, SSD_INNER)
    row = lambda b, c: (b * nc + c, 0)
    const2 = lambda b, c: (0, 0)
    h_tail = (SSD_HEADS, SSD_HEADDIM, SSD_STATE)
    c_tail = (SSD_CONV_W - 1, SSD_CONV_DIM)
    has_init = h_all is not None
    has_prev = prev_h is not None
    h_in, h_specs, h_out_spec = _layered_state_specs(layer * batch, h_tail, h_all, None)
    c_in, c_specs, c_out_spec = _layered_state_specs(layer * batch, c_tail, conv_all, None)
    ins = [proj, proj, dt_raw, conv_w, conv_b.reshape(1, -1), dt_bias.reshape(1, -1), a_log.reshape(1, -1), dexp,
           norm_g.reshape(1, -1)]
    specs = [pl.BlockSpec((lc, SSD_INNER), lambda b, c: (b * nc + c, OFF_Z // SSD_INNER)),
             pl.BlockSpec((lc, SSD_CONV_DIM), lambda b, c: (b * nc + c, OFF_XBC // SSD_CONV_DIM)),
             pl.BlockSpec((lc, SSD_HEADS), row),
             pl.BlockSpec((SSD_CONV_W, SSD_CONV_DIM), const2),
             pl.BlockSpec((1, SSD_CONV_DIM), const2),
             pl.BlockSpec((1, SSD_HEADS), const2),
             pl.BlockSpec((1, SSD_HEADS), const2),
             pl.BlockSpec((1, SSD_INNER), const2),
             pl.BlockSpec((1, SSD_INNER), const2)]
    kern = functools.partial(_ssd_kernel, lc=lc, nchunks=nc, has_init=has_init, n_alias=2 * has_prev)
    scratch = [pltpu.VMEM((_SSD_PAD + lc, SSD_CONV_DIM), f32),
               pltpu.VMEM(h_tail, f32),
               pltpu.VMEM((_SSD_TP, _SSD_TP), f32),
               pltpu.VMEM((lc, SSD_INNER), f32)]
    ins += c_in + h_in
    specs += c_specs + h_specs
    aliases = {}
    if has_prev:
        aliases = {len(ins): 1, len(ins) + 1: 2}
        ins += [prev_h, prev_conv]
        specs += [pl.BlockSpec(memory_space=pl.ANY), pl.BlockSpec(memory_space=pl.ANY)]
    return pl.pallas_call(
        kern,
        grid=(batch, nc),
        in_specs=specs,
        out_specs=[pl.BlockSpec((lc, SSD_INNER), row), h_out_spec, c_out_spec],
        out_shape=[jax.ShapeDtypeStruct((batch * length, SSD_INNER), bf16),
                   jax.ShapeDtypeStruct((depth * batch,) + h_tail, f32),
                   jax.ShapeDtypeStruct((depth * batch,) + c_tail, f32)],
        scratch_shapes=scratch,
        input_output_aliases=aliases,
        compiler_params=_cparams("parallel", "arbitrary"),
        name="ssd",
    )(*ins)


def ssd_wide(proj, dt_raw, dt_bias, a_log, d, norm_g, batch, length, *, h_all, prev_h, layer, depth):
    lc = SSD_HEADDIM
    per_step = 2 if length % (2 * lc) == 0 else 1
    rows = per_step * lc
    nc = length // rows
    dexp = jnp.repeat(d, SSD_HEADDIM).reshape(1, SSD_INNER)
    rep = jnp.repeat(jnp.eye(SSD_HEADS, dtype=bf16), SSD_HEADDIM, axis=1)
    row = lambda b, c: (b * nc + c, 0)
    const2 = lambda b, c: (0, 0)
    h_tail = (SSD_HEADS, SSD_HEADDIM, SSD_STATE)
    h_in, h_specs, h_out_spec = _layered_state_specs(layer * batch, h_tail, h_all, prev_h)
    ins = [proj, proj, dt_raw, dt_bias.reshape(1, -1), a_log.reshape(1, -1), dexp, norm_g.reshape(1, -1), rep]
    specs = [pl.BlockSpec((rows, SSD_INNER), lambda b, c: (b * nc + c, OFF_Z // SSD_INNER)),
             pl.BlockSpec((rows, SSD_CONV_DIM), lambda b, c: (b * nc + c, OFF_XBC // SSD_CONV_DIM)),
             pl.BlockSpec((rows, SSD_HEADS), row),
             pl.BlockSpec((1, SSD_HEADS), const2),
             pl.BlockSpec((1, SSD_HEADS), const2),
             pl.BlockSpec((1, SSD_INNER), const2),
             pl.BlockSpec((1, SSD_INNER), const2),
             pl.BlockSpec((SSD_HEADS, SSD_INNER), const2)]
    aliases = {len(ins) + len(h_in) - 1: 1} if prev_h is not None else {}
    kern = functools.partial(_ssd_wide_kernel, nsteps=nc, per_step=per_step, has_init=h_all is not None,
                             n_alias=int(prev_h is not None))
    return pl.pallas_call(
        kern,
        grid=(batch, nc),
        in_specs=specs + h_specs,
        out_specs=[pl.BlockSpec((rows, SSD_INNER), row), h_out_spec],
        out_shape=[jax.ShapeDtypeStruct((batch * length, SSD_INNER), bf16),
                   jax.ShapeDtypeStruct((depth * batch,) + h_tail, f32)],
        scratch_shapes=[pltpu.VMEM((SSD_STATE, SSD_INNER), f32),
                        pltpu.VMEM((rows, SSD_INNER), f32),
                        pltpu.VMEM((rows, SSD_INNER), f32)],
        input_output_aliases=aliases,
        compiler_params=_cparams("parallel", "arbitrary"),
        name="ssd_wide",
    )(*ins, *h_in)


def _merge_kernel(ra_ref, sa_ref, da_ref, gr_ref, gs_ref, gd_ref, wr_ref, wv_ref, wg_ref, wd_ref, o_ref):
    sa = sa_ref[...]
    ret = _dot(ra_ref[...], wr_ref[...])
    s5 = _dot(sa, wv_ref[...]) * _sigmoid(_dot(sa, wg_ref[...]))
    sd = _dot(da_ref[...], wd_ref[...])
    merged = gr_ref[...] * ret + gs_ref[...] * s5 + gd_ref[...] * sd
    o_ref[...] = merged.astype(bf16)


def merge(ret_act, s5_act, ssd_act, gates, w_ret_o, w_glu, w_ssd_out, *, tm, tn):
    n = ret_act.shape[0]
    nj = D_MODEL // tn
    act = lambda i, j: (i, 0)
    return pl.pallas_call(
        _merge_kernel,
        grid=(n // tm, nj),
        in_specs=[pl.BlockSpec((tm, R_VAL), act),
                  pl.BlockSpec((tm, D_MODEL), act),
                  pl.BlockSpec((tm, SSD_INNER), act),
                  pl.BlockSpec((tm, tn), lambda i, j: (i, j)),
                  pl.BlockSpec((tm, tn), lambda i, j: (i, nj + j)),
                  pl.BlockSpec((tm, tn), lambda i, j: (i, 2 * nj + j)),
                  pl.BlockSpec((R_VAL, tn), lambda i, j: (0, j)),
                  pl.BlockSpec((D_MODEL, tn), lambda i, j: (0, j)),
                  pl.BlockSpec((D_MODEL, tn), lambda i, j: (0, nj + j)),
                  pl.BlockSpec((SSD_INNER, tn), lambda i, j: (0, j))],
        out_specs=pl.BlockSpec((tm, tn), lambda i, j: (i, j)),
        out_shape=jax.ShapeDtypeStruct((n, D_MODEL), bf16),
        compiler_params=_cparams("parallel", "arbitrary"),
        name="merge",
    )(ret_act, s5_act, ssd_act, gates, gates, gates, w_ret_o, w_glu, w_glu, w_ssd_out)


def _attn_kernel(q_ref, k_ref, v_ref, o_ref):
    scale = X_HEAD_DIM ** -0.5
    for h in range(X_HEADS):
        hs = slice(h * X_HEAD_DIM, (h + 1) * X_HEAD_DIM)
        s = _dot_nt(q_ref[:, hs], k_ref[0, :, hs].astype(bf16)) * scale
        e = jnp.exp(s - jnp.max(s, axis=-1, keepdims=True))
        p = e / jnp.sum(e, axis=-1, keepdims=True)
        o_ref[:, hs] = _dot(p.astype(bf16), v_ref[0, :, hs].astype(bf16)).astype(bf16)


def attention(q, mem_k, mem_v, batch, length, *, tq, kv_base):
    nt = length // tq
    kv = pl.BlockSpec((1, MEM_LEN, D_MODEL), lambda b, i: (kv_base + b, 0, 0))
    return pl.pallas_call(
        _attn_kernel,
        grid=(batch, nt),
        in_specs=[pl.BlockSpec((tq, D_MODEL), lambda b, i: (b * nt + i, 0)), kv, kv],
        out_specs=pl.BlockSpec((tq, D_MODEL), lambda b, i: (b * nt + i, 0)),
        out_shape=jax.ShapeDtypeStruct((batch * length, D_MODEL), bf16),
        compiler_params=_cparams("parallel", "arbitrary"),
        name="attention",
    )(q, mem_k, mem_v)


def _mlp_kernel(x_ref, g_in_ref, g_out_ref, wu_ref, wd_ref, o_ref, hn_scr, acc_scr, *, nff):
    j = pl.program_id(1)

    @pl.when(j == 0)
    def _():
        hn_scr[...] = _rms(x_ref[...], g_in_ref[...]).astype(bf16)
        acc_scr[...] = jnp.zeros_like(acc_scr)

    h = jnp.maximum(_dot(hn_scr[...], wu_ref[...]), 0.0)
    acc_scr[...] += _dot((h * h).astype(bf16), wd_ref[...])

    @pl.when(j == nff - 1)
    def _():
        o_ref[...] = x_ref[...] + _rms(acc_scr[...], g_out_ref[...])


def mlp(x, g_in, g_out, w_up, w_down, *, tm, tf):
    n, d = x.shape
    nff = D_FF // tf
    kern = functools.partial(_mlp_kernel, nff=nff)
    return pl.pallas_call(
        kern,
        grid=(n // tm, nff),
        in_specs=[pl.BlockSpec((tm, d), lambda i, j: (i, 0)),
                  pl.BlockSpec((1, d), lambda i, j: (0, 0)),
                  pl.BlockSpec((1, d), lambda i, j: (0, 0)),
                  pl.BlockSpec((d, tf), lambda i, j: (0, j)),
                  pl.BlockSpec((tf, d), lambda i, j: (j, 0))],
        out_specs=pl.BlockSpec((tm, d), lambda i, j: (i, 0)),
        out_shape=jax.ShapeDtypeStruct((n, d), f32),
        scratch_shapes=[pltpu.VMEM((tm, d), bf16), pltpu.VMEM((tm, d), f32)],
        compiler_params=_cparams("parallel", "arbitrary"),
        name="mlp",
    )(x, g_in, g_out, w_up, w_down)


def _run_layer(x, batch, length, pos0, mem_k, mem_v, kv_base, carried, prev, layer, depth, lw, *, prompt):
    gains = lw["gains"]
    n = x.shape[0]
    tm = min(512, n)
    tm_in = min(1024, n)
    conv_args = {}
    if prompt:
        conv_args = dict(conv_w=lw["ssd_conv_w"], conv_b=lw["ssd_conv_b"], seq_len=length)
    proj, tails = inproj(x, gains[0:1], lw["w_in_all"], layer, tm=tm_in, **conv_args)
    dt_raw = norm_matmul(x, gains[0:1], lw["w_dt"], tm=tm_in, tn=SSD_HEADS, out_dtype=f32)
    gates = norm_matmul(x, gains[0:1], lw["w_gate"], tm=tm_in, tn=1024, out_dtype=f32, sigmoid=True)
    ret_act, ret_all = retention(proj, lw["ret_gn"], batch, length, pos0, state_all=carried["ret"],
                                 prev_out=prev["ret"], layer=layer, depth=depth)
    if prompt:
        s5_act, s5_r_new, s5_i_new = s5_prompt(proj, lw["s5_d"], lw["s5_mats"], batch, length)
    else:
        s5_act, s5_r_new, s5_i_new = s5_sample(proj, lw["s5_d"], lw["s5_mats"], carried["s5_r"], carried["s5_i"],
                                               batch, length)
    if prompt:
        ssd_act, ssd_all = ssd_wide(proj, dt_raw, lw["ssd_dt_bias"], lw["ssd_a_log"], lw["ssd_d"], lw["ssd_norm"],
                                    batch, length, h_all=carried["ssd"], prev_h=prev["ssd"], layer=layer, depth=depth)
        tiles_per_seq = length // tm_in
        seq_tails = tails.reshape(batch, tiles_per_seq, _CONV_PAD, SSD_CONV_DIM)[:, -1, _CONV_PAD - (SSD_CONV_W - 1):]
        conv_all = seq_tails if prev["conv"] is None else jnp.concatenate([prev["conv"], seq_tails], axis=0)
    else:
        ssd_act, ssd_all, conv_all = ssd(proj, dt_raw, lw["ssd_conv_w"], lw["ssd_conv_b"], lw["ssd_dt_bias"],
                                         lw["ssd_a_log"], lw["ssd_d"], lw["ssd_norm"], batch, length,
                                         h_all=carried["ssd"], conv_all=carried["conv"], prev_h=prev["ssd"],
                                         prev_conv=prev["conv"], layer=layer, depth=depth)
    merged = merge(ret_act, s5_act, ssd_act, gates, lw["w_ret_o"], lw["w_s5_glu"], lw["w_ssd_out"], tm=tm, tn=512)
    x = matmul_norm_res(merged, lw["w_mix_out"], gains[1:2], x, tm=tm)
    q = norm_matmul(x, gains[2:3], lw["w_xq"], tm=tm_in, tn=1024, out_dtype=bf16)
    att = attention(q, mem_k, mem_v, batch, length, tq=min(length, 512), kv_base=kv_base)
    x = matmul_norm_res(att, lw["w_xo"], gains[3:4], x, tm=tm)
    x = mlp(x, gains[4:5], gains[5:6], lw["w_up"], lw["w_down"], tm=tm, tf=1024)
    return x, dict(ret=ret_all, ssd=ssd_all, conv=conv_all), (s5_r_new, s5_i_new)


def kernel(x_prompt, x_sample, mem_prompt, state_ret, state_s5_re, state_s5_im, state_ssd, cache_ssd_conv,
           cache_mem_k, cache_mem_v, norm_gains, w_in, ret_gn, w_ret_o, s5_a_re, s5_a_im, s5_b_re, s5_b_im,
           s5_c_re, s5_c_im, s5_d, s5_log_dt, w_s5_glu, ssd_conv_w, ssd_conv_b, ssd_dt_bias, ssd_a_log, ssd_d,
           ssd_norm, w_ssd_out, w_mix_out, w_xq, w_xkv, w_xo, w_up, w_down):
    b, l, _ = x_prompt.shape
    db, dl, _ = x_sample.shape
    depth = w_in.shape[0]
    yp = x_prompt.reshape(b * l, D_MODEL)
    ys = x_sample.reshape(db * dl, D_MODEL)
    mem2d = mem_prompt.reshape(b * MEM_LEN, D_MODEL)
    cache_k = cache_mem_k.reshape(depth * db, MEM_LEN, D_MODEL)
    cache_v = cache_mem_v.reshape(depth * db, MEM_LEN, D_MODEL)
    s_carried = dict(ret=state_ret.reshape(depth * db, R_HEADS, R_DK, R_DV),
                     ssd=state_ssd.reshape(depth * db, SSD_HEADS, SSD_HEADDIM, SSD_STATE),
                     conv=cache_ssd_conv.reshape(depth * db, SSD_CONV_W - 1, SSD_CONV_DIM))
    p_carried = dict(ret=None, ssd=None, conv=None)
    p_prev = dict(ret=None, ssd=None, conv=None)
    s_prev = dict(ret=None, ssd=None, conv=None)
    p_s5, s_s5, p_mk, p_mv = [], [], [], []
    w_in_bf = w_in.astype(bf16)
    for i in range(depth):
        ar, ai, lr, li, bbr, bbi = s5_prep(s5_a_re[i], s5_a_im[i], s5_log_dt[i],
                                           jnp.swapaxes(s5_b_re[i], 1, 2), jnp.swapaxes(s5_b_im[i], 1, 2))
        lane = lambda t: t.reshape(1, S5_LANES)
        lw = dict(
            gains=norm_gains[i],
            w_in_all=w_in_bf,
            w_dt=w_in_bf[i, :, OFF_DT:OFF_GATE],
            w_gate=w_in_bf[i, :, OFF_GATE:],
            ret_gn=ret_gn[i].reshape(1, R_VAL),
            w_ret_o=w_ret_o[i].astype(bf16),
            s5_d=s5_d[i].reshape(1, D_MODEL),
            s5_mats=(_s5_blockdiag_in(bbr), _s5_blockdiag_in(bbi),
                     _s5_blockdiag_out(s5_c_re[i]), _s5_blockdiag_out(s5_c_im[i]),
                     lane(ar), lane(ai), lane(lr), lane(li)),
            w_s5_glu=w_s5_glu[i].astype(bf16),
            ssd_conv_w=ssd_conv_w[i], ssd_conv_b=ssd_conv_b[i], ssd_dt_bias=ssd_dt_bias[i],
            ssd_a_log=ssd_a_log[i], ssd_d=ssd_d[i], ssd_norm=ssd_norm[i],
            w_ssd_out=w_ssd_out[i].astype(bf16),
            w_mix_out=w_mix_out[i].astype(bf16),
            w_xq=w_xq[i].astype(bf16),
            w_xo=w_xo[i].astype(bf16),
            w_up=w_up[i].astype(bf16),
            w_down=w_down[i].astype(bf16),
        )
        kv = norm_matmul(mem2d, norm_gains[i, 6:7], w_xkv[i].astype(bf16), tm=b * MEM_LEN, tn=512, out_dtype=f32)
        mk = kv[:, :D_MODEL].reshape(b, MEM_LEN, D_MODEL)
        mv = kv[:, D_MODEL:].reshape(b, MEM_LEN, D_MODEL)
        p_mk.append(mk.reshape(b, MEM_LEN, X_HEADS, X_HEAD_DIM))
        p_mv.append(mv.reshape(b, MEM_LEN, X_HEADS, X_HEAD_DIM))
        yp, p_prev, s5p = _run_layer(yp, b, l, 0, mk, mv, 0, p_carried, p_prev, i, depth, lw, prompt=True)
        p_s5.append(s5p)
        s_carried_i = dict(s_carried, s5_r=state_s5_re[i], s5_i=state_s5_im[i])
        ys, s_prev, s5s = _run_layer(ys, db, dl, PAST_LEN, cache_k, cache_v, i * db, s_carried_i, s_prev, i, depth,
                                     lw, prompt=False)
        s_s5.append(s5s)
    p_s5_re, p_s5_im = [jnp.stack(t) for t in zip(*p_s5)]
    s_s5_re, s_s5_im = [jnp.stack(t) for t in zip(*s_s5)]
    unstack = lambda t, bsz: t.reshape((depth, bsz) + t.shape[1:])
    return (yp.reshape(b, l, D_MODEL), ys.reshape(db, dl, D_MODEL),
            unstack(p_prev["ret"], b), p_s5_re, p_s5_im, unstack(p_prev["ssd"], b), unstack(p_prev["conv"], b),
            jnp.stack(p_mk), jnp.stack(p_mv),
            unstack(s_prev["ret"], db), s_s5_re, s_s5_im, unstack(s_prev["ssd"], db), unstack(s_prev["conv"], db))
```

```python
import functools
import math

import jax
import jax.numpy as jnp
from jax import lax
from jax.experimental import pallas as pl
from jax.experimental.pallas import tpu as pltpu

f32 = jnp.float32
bf16 = jnp.bfloat16

D_MODEL = 2048
PAST_LEN = 1024
CHUNK = 64
EPS = 1e-6
R_HEADS = 8
R_DK = 128
R_DV = 256
R_QK = R_HEADS * R_DK
R_VAL = R_HEADS * R_DV
ROPE_BASE = 10000.0
S5_GROUP = 16
S5_GROUPS = D_MODEL // S5_GROUP
S5_STATE = 64
S5_LANES = S5_GROUPS * S5_STATE
S5_GT = 16
S5_NGT = S5_GROUPS // S5_GT
S5_TU = S5_GT * S5_GROUP
S5_TL = S5_GT * S5_STATE
SSD_INNER = 2 * D_MODEL
SSD_HEADDIM = 64
SSD_HEADS = SSD_INNER // SSD_HEADDIM
SSD_GROUPS = 8
SSD_HPG = SSD_HEADS // SSD_GROUPS
SSD_STATE = 128
SSD_CONV_W = 4
SSD_CONV_DIM = SSD_INNER + 2 * SSD_GROUPS * SSD_STATE
MEM_LEN = 256
X_HEADS = 4
X_HEAD_DIM = D_MODEL // X_HEADS
D_FF = 4 * D_MODEL
OFF_Q = 0
OFF_K = OFF_Q + R_QK
OFF_V = OFF_K + R_QK
OFF_G = OFF_V + R_VAL
OFF_U = OFF_G + R_VAL
OFF_Z = OFF_U + D_MODEL
OFF_XBC = OFF_Z + SSD_INNER
OFF_DT = OFF_XBC + SSD_CONV_DIM
OFF_GATE = OFF_DT + SSD_HEADS
MAIN_COLS = OFF_DT

VMEM_LIMIT_BYTES = 52 * 1024 * 1024


def _cparams(*sem):
    return pltpu.CompilerParams(dimension_semantics=sem, vmem_limit_bytes=VMEM_LIMIT_BYTES)


def _rms(x, g):
    return x * lax.rsqrt(jnp.mean(x * x, axis=-1, keepdims=True) + EPS) * g


def _dot(a, b):
    return jnp.dot(a, b, preferred_element_type=f32)


def _dot_nt(a, b):
    return lax.dot_general(a, b, (((1,), (1,)), ((), ())), preferred_element_type=f32)


def _dot_tn(a, b):
    return lax.dot_general(a, b, (((0,), (0,)), ((), ())), preferred_element_type=f32)


def _sigmoid(x):
    return 1.0 / (1.0 + jnp.exp(-x))


def _norm_matmul_kernel(x_ref, g_ref, w_ref, o_ref, hn_ref, *, sigmoid):
    @pl.when(pl.program_id(1) == 0)
    def _():
        hn_ref[...] = _rms(x_ref[...], g_ref[...]).astype(bf16)

    y = _dot(hn_ref[...], w_ref[...])
    if sigmoid:
        y = _sigmoid(y)
    o_ref[...] = y.astype(o_ref.dtype)


def norm_matmul(x, g, w, *, tm, tn, out_dtype, sigmoid=False):
    n, k = x.shape
    m = w.shape[1]
    return pl.pallas_call(
        functools.partial(_norm_matmul_kernel, sigmoid=sigmoid),
        grid=(n // tm, m // tn),
        in_specs=[pl.BlockSpec((tm, k), lambda i, j: (i, 0)),
                  pl.BlockSpec((1, k), lambda i, j: (0, 0)),
                  pl.BlockSpec((k, tn), lambda i, j: (0, j))],
        out_specs=pl.BlockSpec((tm, tn), lambda i, j: (i, j)),
        out_shape=jax.ShapeDtypeStruct((n, m), out_dtype),
        scratch_shapes=[pltpu.VMEM((tm, k), bf16)],
        compiler_params=_cparams("parallel", "arbitrary"),
        name="norm_matmul",
    )(x, g, w)


IN_TN = 1024
_J_G = (OFF_G // IN_TN, OFF_U // IN_TN)
_J_Z = (OFF_Z // IN_TN, OFF_XBC // IN_TN)
_J_XBC = OFF_XBC // IN_TN
_NJ_MAIN = MAIN_COLS // IN_TN
_CONV_PAD = 8
_IN_SUB = 256


def _inproj_kernel(*refs, tm, conv, tiles_per_seq):
    if conv:
        x_ref, g_ref, w_ref, cw_ref, cb_ref, o_ref, tails_ref, hn_ref, tail_scr = refs
    else:
        x_ref, g_ref, w_ref, o_ref, hn_ref = refs
    i = pl.program_id(0)
    j = pl.program_id(1)

    @pl.when(j == 0)
    def _():
        hn_ref[...] = _rms(x_ref[...], g_ref[...]).astype(bf16)

    is_silu = ((j >= _J_G[0]) & (j < _J_G[1])) | ((j >= _J_Z[0]) & (j < _J_Z[1]))
    is_plain = jnp.logical_not(is_silu)
    if conv:
        is_plain = is_plain & (j < _J_XBC)
    subs = [slice(s * _IN_SUB, (s + 1) * _IN_SUB) for s in range(IN_TN // _IN_SUB)]

    @pl.when(is_silu)
    def _():
        for cs in subs:
            acc = _dot(hn_ref[...], w_ref[:, cs])
            o_ref[:, cs] = acc * _sigmoid(acc)

    @pl.when(is_plain)
    def _():
        o_ref[...] = _dot(hn_ref[...], w_ref[...])

    if conv:
        @pl.when(j >= _J_XBC)
        def _():
            jj = j - _J_XBC

            @pl.when(i % tiles_per_seq == 0)
            def _():
                tail_scr[jj] = jnp.zeros((_CONV_PAD, IN_TN), f32)

            for cs in subs:
                acc = _dot(hn_ref[...], w_ref[:, cs])
                ext = jnp.concatenate([tail_scr[jj, :, cs], acc], axis=0)
                h = cw_ref[0:1, cs] * ext
                for t in range(1, SSD_CONV_W):
                    h = pltpu.roll(h, 1, 0) + cw_ref[t:t + 1, cs] * ext
                xc = h[_CONV_PAD:, :] + cb_ref[:, cs]
                o_ref[:, cs] = xc * _sigmoid(xc)
                last = acc[tm - _CONV_PAD:, :]
                tail_scr[jj, :, cs] = last
                tails_ref[0, :, cs] = last


def inproj(x, g, w_all, layer, *, tm, conv_w=None, conv_b=None, seq_len=None):
    n, k = x.shape
    conv = conv_w is not None
    nxbc = SSD_CONV_DIM // IN_TN
    in_specs = [pl.BlockSpec((tm, k), lambda i, j: (i, 0)),
                pl.BlockSpec((1, k), lambda i, j: (0, 0)),
                pl.BlockSpec((None, k, IN_TN), lambda i, j: (layer, 0, j))]
    out_specs = pl.BlockSpec((tm, IN_TN), lambda i, j: (i, j))
    out_shape = jax.ShapeDtypeStruct((n, MAIN_COLS), f32)
    scratch = [pltpu.VMEM((tm, k), bf16)]
    args = [x, g, w_all]
    tiles_per_seq = None
    if conv:
        xcol = lambda i, j: (0, jnp.maximum(j - _J_XBC, 0))
        in_specs += [pl.BlockSpec((SSD_CONV_W, IN_TN), xcol), pl.BlockSpec((1, IN_TN), xcol)]
        out_specs = [out_specs, pl.BlockSpec((1, _CONV_PAD, IN_TN), lambda i, j: (i, 0, jnp.maximum(j - _J_XBC, 0)))]
        out_shape = [out_shape, jax.ShapeDtypeStruct((n // tm, _CONV_PAD, SSD_CONV_DIM), f32)]
        scratch += [pltpu.VMEM((nxbc, _CONV_PAD, IN_TN), f32)]
        args += [conv_w, conv_b.reshape(1, -1)]
        tiles_per_seq = seq_len // tm
    kern = functools.partial(_inproj_kernel, tm=tm, conv=conv, tiles_per_seq=tiles_per_seq)
    out = pl.pallas_call(
        kern,
        grid=(n // tm, _NJ_MAIN),
        in_specs=in_specs,
        out_specs=out_specs,
        out_shape=out_shape,
        scratch_shapes=scratch,
        compiler_params=_cparams("arbitrary", "arbitrary"),
        name="inproj",
    )(*args)
    return out if conv else (out, None)


def _matmul_norm_res_kernel(a_ref, w_ref, g_ref, x_ref, o_ref):
    y = _dot(a_ref[...], w_ref[...])
    o_ref[...] = x_ref[...] + _rms(y, g_ref[...])


def matmul_norm_res(a, w, g, x, *, tm):
    n, k = a.shape
    d = w.shape[1]
    return pl.pallas_call(
        _matmul_norm_res_kernel,
        grid=(n // tm,),
        in_specs=[pl.BlockSpec((tm, k), lambda i: (i, 0)),
                  pl.BlockSpec((k, d), lambda i: (0, 0)),
                  pl.BlockSpec((1, d), lambda i: (0, 0)),
                  pl.BlockSpec((tm, d), lambda i: (i, 0))],
        out_specs=pl.BlockSpec((tm, d), lambda i: (i, 0)),
        out_shape=jax.ShapeDtypeStruct((n, d), f32),
        compiler_params=_cparams("parallel"),
        name="matmul_norm_res",
    )(a, w, g, x)


def _layered_state_specs(layer_rows, tail, state_all, prev_out):
    zeros = (0,) * len(tail)
    spec = pl.BlockSpec((1,) + tail, lambda b, c: (layer_rows + b,) + zeros)
    ins, specs = [], []
    if state_all is not None:
        ins.append(state_all)
        specs.append(spec)
    if prev_out is not None:
        ins.append(prev_out)
        specs.append(pl.BlockSpec(memory_space=pl.ANY))
    return ins, specs, spec


def _retention_kernel(*refs, lc, per_step, nsteps, has_init, n_alias):
    q_ref, k_ref, v_ref, g_ref, cos_ref, sin_ref, intra_ref, din_ref, dup_ref, dall_ref, gn_ref = refs[:11]
    s0_ref = refs[11] if has_init else None
    o_ref, snew_ref, s_scr = refs[11 + has_init + n_alias:]
    c = pl.program_id(1)

    @pl.when(c == 0)
    def _():
        if has_init:
            s_scr[...] = s0_ref[0]
        else:
            s_scr[...] = jnp.zeros_like(s_scr)

    for cc, h in [(cc, h) for cc in range(per_step) for h in range(R_HEADS)]:
        rows = slice(cc * lc, (cc + 1) * lc)
        cos = cos_ref[rows, :]
        sin = sin_ref[rows, :]
        q = q_ref[rows, h * R_DK:(h + 1) * R_DK]
        k = k_ref[rows, h * R_DK:(h + 1) * R_DK]
        v = v_ref[rows, h * R_DV:(h + 1) * R_DV].astype(bf16)
        qr = (q * cos + pltpu.roll(q, R_DK // 2, 1) * sin)
        kr = (k * cos + pltpu.roll(k, R_DK // 2, 1) * sin) * (R_DK ** -0.5)
        qb = qr.astype(bf16)
        s = s_scr[h]
        scores = _dot_nt(qb, kr.astype(bf16)) * intra_ref[h]
        out = _dot(scores.astype(bf16), v) + _dot(qb, s.astype(bf16)) * din_ref[h]
        s_scr[h] = dall_ref[h] * s + _dot_tn((kr * dup_ref[h]).astype(bf16), v)
        mu = jnp.mean(out, axis=-1, keepdims=True)
        cen = out - mu
        var = jnp.mean(cen * cen, axis=-1, keepdims=True)
        o = cen * lax.rsqrt(var + EPS) * gn_ref[:, h * R_DV:(h + 1) * R_DV]
        o_ref[rows, h * R_DV:(h + 1) * R_DV] = (g_ref[rows, h * R_DV:(h + 1) * R_DV] * o).astype(bf16)

    @pl.when(c == nsteps - 1)
    def _():
        snew_ref[0] = s_scr[...]


def retention(proj, gn, batch, length, pos0, *, state_all, prev_out, layer, depth):
    lc = min(length, CHUNK)
    per_step = max(p for p in (4, 2, 1) if length % (p * lc) == 0)
    rows = per_step * lc
    nc = length // rows
    half = R_DK // 2
    inv_freq = jnp.exp(-math.log(ROPE_BASE) * jnp.arange(half, dtype=f32) / half)
    pos = (pos0 + jnp.arange(length, dtype=jnp.int32)).astype(f32)
    ang = pos[:, None] * inv_freq[None]
    cos2 = jnp.concatenate([jnp.cos(ang), jnp.cos(ang)], axis=-1)
    sin2 = jnp.concatenate([-jnp.sin(ang), jnp.sin(ang)], axis=-1)
    log_gamma = jnp.log1p(-jnp.exp2(-5.0 - jnp.arange(R_HEADS, dtype=f32)))
    idx = jnp.arange(lc, dtype=f32)
    lg = log_gamma[:, None]
    intra = jnp.exp(jnp.abs(idx[:, None] - idx[None, :])[None] * lg[:, :, None])
    din = jnp.broadcast_to(jnp.exp((idx + 1.0)[None] * lg)[:, :, None], (R_HEADS, lc, R_DV))
    dup = jnp.broadcast_to(jnp.exp((lc - 1.0 - idx)[None] * lg)[:, :, None], (R_HEADS, lc, R_DK))
    dall = jnp.exp(lc * log_gamma)
    tail = (R_HEADS, R_DK, R_DV)
    st_in, st_specs, st_out_spec = _layered_state_specs(layer * batch, tail, state_all, prev_out)
    n_fixed = 11
    kern = functools.partial(_retention_kernel, lc=lc, per_step=per_step, nsteps=nc,
                             has_init=state_all is not None, n_alias=int(prev_out is not None))
    aliases = {n_fixed + len(st_in) - 1: 1} if prev_out is not None else {}
    return pl.pallas_call(
        kern,
        grid=(batch, nc),
        in_specs=[pl.BlockSpec((rows, R_QK), lambda b, c: (b * nc + c, OFF_Q // R_QK)),
                  pl.BlockSpec((rows, R_QK), lambda b, c: (b * nc + c, OFF_K // R_QK)),
                  pl.BlockSpec((rows, R_VAL), lambda b, c: (b * nc + c, OFF_V // R_VAL)),
                  pl.BlockSpec((rows, R_VAL), lambda b, c: (b * nc + c, OFF_G // R_VAL)),
                  pl.BlockSpec((rows, R_DK), lambda b, c: (c, 0)),
                  pl.BlockSpec((rows, R_DK), lambda b, c: (c, 0)),
                  pl.BlockSpec((R_HEADS, lc, lc), lambda b, c: (0, 0, 0)),
                  pl.BlockSpec((R_HEADS, lc, R_DV), lambda b, c: (0, 0, 0)),
                  pl.BlockSpec((R_HEADS, lc, R_DK), lambda b, c: (0, 0, 0)),
                  pl.BlockSpec(memory_space=pltpu.SMEM),
                  pl.BlockSpec((1, R_VAL), lambda b, c: (0, 0))] + st_specs,
        out_specs=[pl.BlockSpec((rows, R_VAL), lambda b, c: (b * nc + c, 0)), st_out_spec],
        out_shape=[jax.ShapeDtypeStruct((batch * length, R_VAL), bf16),
                   jax.ShapeDtypeStruct((depth * batch,) + tail, f32)],
        scratch_shapes=[pltpu.VMEM(tail, f32)],
        input_output_aliases=aliases,
        compiler_params=_cparams("parallel", "arbitrary"),
        name="retention",
    )(proj, proj, proj, proj, cos2, sin2, intra, din, dup, dall, gn, *st_in)


def _s5_prep_kernel(are_ref, aim_ref, ldt_ref, bre_ref, bim_ref, ar_ref, ai_ref, lr_ref, li_ref, bbr_ref, bbi_ref):
    a_re = are_ref[...]
    a_im = aim_ref[...]
    dt = jnp.exp(ldt_ref[...])
    lr = dt * a_re
    li = dt * a_im
    mag = jnp.exp(lr)
    ar = mag * jnp.cos(li)
    ai = mag * jnp.sin(li)
    nr = ar - 1.0
    den = a_re * a_re + a_im * a_im
    cr = (nr * a_re + ai * a_im) / den
    ci = (ai * a_re - nr * a_im) / den
    ar_ref[...] = ar
    ai_ref[...] = ai
    lr_ref[...] = lr
    li_ref[...] = li
    b_re = bre_ref[...]
    b_im = bim_ref[...]
    bbr_ref[...] = cr[:, None, :] * b_re - ci[:, None, :] * b_im
    bbi_ref[...] = cr[:, None, :] * b_im + ci[:, None, :] * b_re


def s5_prep(a_re, a_im, log_dt, b_re, b_im):
    gp = jax.ShapeDtypeStruct((S5_GROUPS, S5_STATE), f32)
    gb = jax.ShapeDtypeStruct((S5_GROUPS, S5_GROUP, S5_STATE), f32)
    return pl.pallas_call(
        _s5_prep_kernel,
        out_shape=[gp, gp, gp, gp, gb, gb],
        compiler_params=pltpu.CompilerParams(vmem_limit_bytes=VMEM_LIMIT_BYTES),
        name="s5_prep",
    )(a_re, a_im, log_dt.reshape(S5_GROUPS, 1), b_re, b_im)


def _s5_project_in(u_ref, perm_ref, bre_ref, bim_ref, bur_scr, bui_scr):
    ub = u_ref[...].astype(bf16)
    up = _dot(perm_ref[...], ub).astype(bf16)
    bur_scr[...] = _dot(up, bre_ref[0])
    bui_scr[...] = _dot(up, bim_ref[0])


def _s5_scan(bur_scr, bui_scr, ar_row, ai_row, init_r, init_i, nseq, steps):
    lw = min(S5_TL, S5_LANES // nseq)
    fin_r, fin_i = [], []
    for lb in range(S5_TL // lw):
        sl = slice(lb * lw, (lb + 1) * lw)
        ar = jnp.broadcast_to(ar_row[:, sl], (nseq, lw))
        ai = jnp.broadcast_to(ai_row[:, sl], (nseq, lw))
        xr = init_r[:, sl]
        xi = init_i[:, sl]
        for t in range(steps):
            rows = slice(t * nseq, (t + 1) * nseq)
            nxr = ar * xr - ai * xi + bur_scr[rows, sl]
            nxi = ar * xi + ai * xr + bui_scr[rows, sl]
            xr, xi = nxr, nxi
            bur_scr[rows, sl] = xr
            bui_scr[rows, sl] = xi
        fin_r.append(xr)
        fin_i.append(xi)
    return fin_r, fin_i, lw


def _s5_output(u_ref, xr, xi, cre_ref, cim_ref, d_ref, permt_ref, y_ref):
    y = _dot(xr, cre_ref[0]) - _dot(xi, cim_ref[0])
    pt = permt_ref[...]
    y0, y1, y2 = _split3(y)
    y = _dot(pt, y0) + _dot(pt, y1) + _dot(pt, y2)
    y_ref[...] = jax.nn.gelu(y + d_ref[...] * u_ref[...]).astype(bf16)


def _s5_prompt_kernel(u_ref, d_ref, bre_ref, bim_ref, cre_ref, cim_ref, ar_ref, ai_ref, lr_ref, li_ref,
                      perm_ref, permt_ref, y_ref, hr_ref, hi_ref,
                      bur_scr, bui_scr, xbr_scr, xbi_scr, apr_scr, api_scr, xsr_scr, xsi_scr, car_scr, cai_scr,
                      *, nseq, steps, nsc):
    b = pl.program_id(1)
    sc = pl.program_id(2)

    @pl.when((b == 0) & (sc == 0))
    def _():
        tt = (lax.broadcasted_iota(jnp.int32, (steps, S5_TL), 0) + 1).astype(f32)
        mag = jnp.exp(tt * lr_ref[...])
        pwr = mag * jnp.cos(tt * li_ref[...])
        pwi = mag * jnp.sin(tt * li_ref[...])
        for t in range(steps):
            apr_scr[t * nseq:(t + 1) * nseq, :] = jnp.broadcast_to(pwr[t:t + 1, :], (nseq, S5_TL))
            api_scr[t * nseq:(t + 1) * nseq, :] = jnp.broadcast_to(pwi[t:t + 1, :], (nseq, S5_TL))

    @pl.when(sc == 0)
    def _():
        car_scr[...] = jnp.zeros_like(car_scr)
        cai_scr[...] = jnp.zeros_like(cai_scr)

    _s5_project_in(u_ref, perm_ref, bre_ref, bim_ref, bur_scr, bui_scr)
    zero = jnp.zeros((nseq, S5_TL), f32)
    _s5_scan(bur_scr, bui_scr, ar_ref[...], ai_ref[...], zero, zero, nseq, steps)
    pr = apr_scr[(steps - 1) * nseq:(steps - 1) * nseq + 1, :]
    pi = api_scr[(steps - 1) * nseq:(steps - 1) * nseq + 1, :]
    xr = car_scr[...]
    xi = cai_scr[...]
    for r in range(nseq):
        xsr_scr[r:r + 1, :] = xr
        xsi_scr[r:r + 1, :] = xi
        er = bur_scr[(steps - 1) * nseq + r:(steps - 1) * nseq + r + 1, :]
        ei = bui_scr[(steps - 1) * nseq + r:(steps - 1) * nseq + r + 1, :]
        xr, xi = er + pr * xr - pi * xi, ei + pr * xi + pi * xr
    car_scr[...] = xr
    cai_scr[...] = xi
    blk = 2 * nseq
    lq = S5_TL // 4
    for q in range(S5_TL // lq):
        sl = slice(q * lq, (q + 1) * lq)
        xsr = jnp.concatenate([xsr_scr[:, sl], xsr_scr[:, sl]], axis=0)
        xsi = jnp.concatenate([xsi_scr[:, sl], xsi_scr[:, sl]], axis=0)
        for t2 in range(nseq * steps // blk):
            rows = slice(t2 * blk, (t2 + 1) * blk)
            ptr = apr_scr[rows, sl]
            pti = api_scr[rows, sl]
            xbr_scr[rows, sl] = (bur_scr[rows, sl] + ptr * xsr - pti * xsi).astype(bf16)
            xbi_scr[rows, sl] = (bui_scr[rows, sl] + ptr * xsi + pti * xsr).astype(bf16)
    _s5_output(u_ref, xbr_scr[...], xbi_scr[...], cre_ref, cim_ref, d_ref, permt_ref, y_ref)

    @pl.when(sc == nsc - 1)
    def _():
        hr_ref[0] = car_scr[...]
        hi_ref[0] = cai_scr[...]


def _s5_sample_kernel(u_ref, d_ref, bre_ref, bim_ref, cre_ref, cim_ref, ar_ref, ai_ref, perm_ref, permt_ref,
                      h0r_ref, h0i_ref, y_ref, hr_ref, hi_ref, bur_scr, bui_scr, *, nseq, steps):
    _s5_project_in(u_ref, perm_ref, bre_ref, bim_ref, bur_scr, bui_scr)
    fin_r, fin_i, lw = _s5_scan(bur_scr, bui_scr, ar_ref[...], ai_ref[...], h0r_ref[...], h0i_ref[...], nseq, steps)
    for lb in range(len(fin_r)):
        hr_ref[:, lb * lw:(lb + 1) * lw] = fin_r[lb]
        hi_ref[:, lb * lw:(lb + 1) * lw] = fin_i[lb]
    _s5_output(u_ref, bur_scr[...].astype(bf16), bui_scr[...].astype(bf16), cre_ref, cim_ref, d_ref, permt_ref,
               y_ref)


def _s5_blockdiag_in(bb):
    t = bb.reshape(S5_NGT, S5_GT, S5_GROUP, S5_STATE)
    eye = jnp.eye(S5_GT, dtype=bool)
    out = jnp.where(eye[None, :, None, :, None], t[:, :, :, None, :], 0.0)
    return out.reshape(S5_NGT, S5_TU, S5_TL).astype(bf16)


def _s5_blockdiag_out(c):
    t = jnp.transpose(c.reshape(S5_NGT, S5_GT, S5_GROUP, S5_STATE), (0, 1, 3, 2))
    eye = jnp.eye(S5_GT, dtype=bool)
    out = jnp.where(eye[None, :, None, :, None], t[:, :, :, None, :], 0.0)
    return out.reshape(S5_NGT, S5_TL, S5_TU).astype(bf16)


def _s5_perm(nseq, steps):
    n = nseq * steps
    dst = jnp.arange(n)
    src = (dst % nseq) * steps + dst // nseq
    p = src[:, None] == jnp.arange(n)[None, :]
    return jnp.where(p, 1.0, 0.0).astype(bf16), jnp.where(p.T, 1.0, 0.0).astype(bf16)


def _s5_common_specs(rows, row_map):
    lane = lambda *a: (0, a[0])
    mat = lambda *a: (a[0], 0, 0)
    return [pl.BlockSpec((rows, S5_TU), row_map),
            pl.BlockSpec((1, S5_TU), lane),
            pl.BlockSpec((1, S5_TU, S5_TL), mat),
            pl.BlockSpec((1, S5_TU, S5_TL), mat),
            pl.BlockSpec((1, S5_TL, S5_TU), mat),
            pl.BlockSpec((1, S5_TL, S5_TU), mat),
            pl.BlockSpec((1, S5_TL), lane),
            pl.BlockSpec((1, S5_TL), lane)]


def s5_prompt(proj, d, mats, batch, length):
    bre, bim, cre, cim, ar, ai, lr, li = mats
    nseq, steps = 8, CHUNK
    rows = nseq * steps
    nsc = length // rows
    ucol = OFF_U // S5_TU
    kern = functools.partial(_s5_prompt_kernel, nseq=nseq, steps=steps, nsc=nsc)
    lane = lambda g, b, s: (0, g)
    const = lambda g, b, s: (0, 0)
    perm, permt = _s5_perm(nseq, steps)
    st = jax.ShapeDtypeStruct((batch, 1, S5_LANES), f32)
    y, hr, hi = pl.pallas_call(
        kern,
        grid=(S5_NGT, batch, nsc),
        in_specs=_s5_common_specs(rows, lambda g, b, s: (b * nsc + s, ucol + g))
        + [pl.BlockSpec((1, S5_TL), lane), pl.BlockSpec((1, S5_TL), lane),
           pl.BlockSpec((rows, rows), const), pl.BlockSpec((rows, rows), const)],
        out_specs=[pl.BlockSpec((rows, S5_TU), lambda g, b, s: (b * nsc + s, g)),
                   pl.BlockSpec((1, 1, S5_TL), lambda g, b, s: (b, 0, g)),
                   pl.BlockSpec((1, 1, S5_TL), lambda g, b, s: (b, 0, g))],
        out_shape=[jax.ShapeDtypeStruct((batch * length, D_MODEL), bf16), st, st],
        scratch_shapes=[pltpu.VMEM((rows, S5_TL), f32), pltpu.VMEM((rows, S5_TL), f32),
                        pltpu.VMEM((rows, S5_TL), bf16), pltpu.VMEM((rows, S5_TL), bf16),
                        pltpu.VMEM((rows, S5_TL), f32), pltpu.VMEM((rows, S5_TL), f32),
                        pltpu.VMEM((nseq, S5_TL), f32), pltpu.VMEM((nseq, S5_TL), f32),
                        pltpu.VMEM((1, S5_TL), f32), pltpu.VMEM((1, S5_TL), f32)],
        compiler_params=_cparams("arbitrary", "arbitrary", "arbitrary"),
        name="s5_prompt",
    )(proj, d, bre, bim, cre, cim, ar, ai, lr, li, perm, permt)
    return y, hr.reshape(batch, S5_GROUPS, S5_STATE), hi.reshape(batch, S5_GROUPS, S5_STATE)


def s5_sample(proj, d, mats, h0r, h0i, batch, length):
    bre, bim, cre, cim, ar, ai, _, _ = mats
    rows = batch * length
    ucol = OFF_U // S5_TU
    kern = functools.partial(_s5_sample_kernel, nseq=batch, steps=length)
    perm, permt = _s5_perm(batch, length)
    st = jax.ShapeDtypeStruct((batch, S5_LANES), f32)
    state_spec = pl.BlockSpec((batch, S5_TL), lambda g: (0, g))
    perm_spec = pl.BlockSpec((rows, rows), lambda g: (0, 0))
    y, hr, hi = pl.pallas_call(
        kern,
        grid=(S5_NGT,),
        in_specs=_s5_common_specs(rows, lambda g: (0, ucol + g)) + [perm_spec, perm_spec, state_spec, state_spec],
        out_specs=[pl.BlockSpec((rows, S5_TU), lambda g: (0, g)), state_spec, state_spec],
        out_shape=[jax.ShapeDtypeStruct((rows, D_MODEL), bf16), st, st],
        scratch_shapes=[pltpu.VMEM((rows, S5_TL), f32), pltpu.VMEM((rows, S5_TL), f32)],
        compiler_params=_cparams("arbitrary"),
        name="s5_sample",
    )(proj, d, bre, bim, cre, cim, ar, ai, perm, permt, h0r.reshape(batch, S5_LANES), h0i.reshape(batch, S5_LANES))
    return y, hr.reshape(batch, S5_GROUPS, S5_STATE), hi.reshape(batch, S5_GROUPS, S5_STATE)


_SSD_PAD = 8
_SSD_TP = 128


def _split3(v):
    p0 = v.astype(bf16)
    r0 = v - p0.astype(f32)
    p1 = r0.astype(bf16)
    p2 = (r0 - p1.astype(f32)).astype(bf16)
    return p0, p1, p2


def _ssd_conv(c, nchunks, xbc_ref, conv0_ref, cw_ref, cb_ref, convnew_ref, stage, lc):
    keep = SSD_CONV_W - 1
    lo = _SSD_PAD - keep

    @pl.when(c == 0)
    def _():
        if conv0_ref is not None:
            stage[lo:_SSD_PAD, :] = conv0_ref[0]
        else:
            stage[0:_SSD_PAD, :] = jnp.zeros((_SSD_PAD, SSD_CONV_DIM), f32)

    stage[_SSD_PAD:_SSD_PAD + lc, :] = xbc_ref[...]
    xc = cb_ref[...]
    for j in range(SSD_CONV_W):
        xc = xc + cw_ref[j:j + 1, :] * stage[lo + j:lo + j + lc, :]
    xc = xc * _sigmoid(xc)
    tail = stage[lo + lc:_SSD_PAD + lc, :]

    @pl.when(c == nchunks - 1)
    def _():
        convnew_ref[0] = tail

    stage[lo:_SSD_PAD, :] = tail
    return xc


def _ssd_dt_cumsum(dt_raw, dtb_ref, alog_ref, lc):
    dtx = dt_raw + dtb_ref[...]
    dtv = jnp.maximum(dtx, 0.0) + jnp.log1p(jnp.exp(-jnp.abs(dtx)))
    da = dtv * (-jnp.exp(alog_ref[...]))
    ri = lax.broadcasted_iota(jnp.int32, (lc, lc), 0)
    ci = lax.broadcasted_iota(jnp.int32, (lc, lc), 1)
    tri = jnp.where(ri >= ci, 1.0, 0.0).astype(bf16)
    da3 = _split3(da)
    cs = _dot(tri, da3[0]) + _dot(tri, da3[1]) + _dot(tri, da3[2])
    return dtv, cs, da3


def _ssd_parse(refs, has_init, n_alias, n_fixed):
    fixed = refs[:n_fixed]
    conv0_ref, h0_ref = (refs[n_fixed], refs[n_fixed + 1]) if has_init else (None, None)
    rest = refs[n_fixed + 2 * has_init + n_alias:]
    return fixed, conv0_ref, h0_ref, rest


def _ssd_wide_kernel(*refs, nsteps, per_step, has_init, n_alias):
    lc = SSD_HEADDIM
    z_ref, xc_ref, dt_ref, dtb_ref, alog_ref, dexp_ref, ng_ref, rep_ref = refs[:8]
    h0_ref = refs[8] if has_init else None
    y_ref, hnew_ref, ht_scr, cm_scr, dx_scr = refs[8 + has_init + n_alias:]
    c = pl.program_id(1)

    @pl.when(c == 0)
    def _():
        if has_init:
            for h in range(SSD_HEADS):
                ht_scr[:, h * lc:(h + 1) * lc] = h0_ref[0, h].T
        else:
            ht_scr[...] = jnp.zeros_like(ht_scr)

    pw = 2 * SSD_HEADDIM
    gw = SSD_HPG * SSD_HEADDIM
    ti = lax.broadcasted_iota(jnp.int32, (lc, gw), 0)
    si = lax.broadcasted_iota(jnp.int32, (lc, gw), 1) % lc
    causal = ti >= si
    bdmask = (lax.broadcasted_iota(jnp.int32, (pw, pw), 0) // lc) == (lax.broadcasted_iota(jnp.int32, (pw, pw), 1) // lc)
    low_half = lax.broadcasted_iota(jnp.int32, (1, pw), 1) < lc
    for cc in range(per_step):
        _ssd_wide_chunk(slice(cc * lc, (cc + 1) * lc), z_ref, xc_ref, dt_ref, dtb_ref, alog_ref, dexp_ref, ng_ref,
                        rep_ref, y_ref, ht_scr, cm_scr, dx_scr, causal, bdmask, low_half)

    @pl.when(c == nsteps - 1)
    def _():
        for h in range(SSD_HEADS):
            hnew_ref[0, h] = ht_scr[:, h * lc:(h + 1) * lc].T


def _ssd_wide_chunk(rows, z_ref, xc_ref, dt_ref, dtb_ref, alog_ref, dexp_ref, ng_ref, rep_ref, y_ref, ht_scr, cm_scr,
                    dx_scr, causal, bdmask, low_half):
    lc = SSD_HEADDIM
    pw = 2 * SSD_HEADDIM
    gw = SSD_HPG * SSD_HEADDIM
    dtv, cs, da3 = _ssd_dt_cumsum(dt_ref[rows, :], dtb_ref, alog_ref, lc)
    upper2 = jnp.where(lax.broadcasted_iota(jnp.int32, (lc, pw), 0) <= lax.broadcasted_iota(jnp.int32, (lc, pw), 1) % lc,
                       1.0, 0.0).astype(bf16)
    cst2 = _dot_tn(da3[0], upper2) + _dot_tn(da3[1], upper2) + _dot_tn(da3[2], upper2)
    rep = rep_ref[...]
    c0, c1, c2 = _split3(cs)
    cm_scr[rows, :] = _dot(c0, rep) + _dot(c1, rep) + _dot(c2, rep)
    d0, d1, d2 = _split3(dtv)
    dx_scr[rows, :] = _dot(d0, rep) + _dot(d1, rep) + _dot(d2, rep)

    b_off, c_off = SSD_INNER, SSD_INNER + SSD_GROUPS * SSD_STATE
    for g in range(SSD_GROUPS):
        gs = slice(g * gw, (g + 1) * gw)
        cm = cm_scr[rows, gs]
        row = jnp.concatenate(
            [jnp.where(low_half, cst2[h0:h0 + 1, :], cst2[h0 + 1:h0 + 2, :])
             for h0 in range(g * SSD_HPG, (g + 1) * SSD_HPG, 2)], axis=1)
        decay = jnp.exp(jnp.where(causal, cm - row, -jnp.inf))
        last = cm[lc - 1:lc, :]
        xg = xc_ref[rows, gs]
        xdt = xg * dx_scr[rows, gs]
        bg = xc_ref[rows, b_off + g * SSD_STATE:b_off + (g + 1) * SSD_STATE].astype(bf16)
        cg = xc_ref[rows, c_off + g * SSD_STATE:c_off + (g + 1) * SSD_STATE].astype(bf16)
        cb2 = _dot_nt(cg, jnp.concatenate([bg, bg], axis=0))
        parts = []
        for pr in range(gw // pw):
            ps = slice(pr * pw, (pr + 1) * pw)
            m = (cb2 * decay[:, ps]).astype(bf16)
            xp = xdt[:, ps]
            bd = jnp.where(bdmask, jnp.concatenate([xp, xp], axis=0), 0.0).astype(bf16)
            parts.append(_dot(m, bd))
        y = jnp.concatenate(parts, axis=1)
        ht = ht_scr[:, gs]
        y = y + _dot(cg, ht.astype(bf16)) * jnp.exp(cm)
        xw = (xdt * jnp.exp(last - cm)).astype(bf16)
        ht_scr[:, gs] = jnp.exp(last) * ht + _dot_tn(bg, xw)
        y = y + dexp_ref[:, gs] * xg
        yg = y * z_ref[rows, gs]
        yg = yg * lax.rsqrt(jnp.mean(yg * yg, axis=-1, keepdims=True) + EPS) * ng_ref[:, gs]
        y_ref[rows, gs] = yg.astype(bf16)


def _ssd_kernel(*refs, lc, nchunks, has_init, n_alias):
    fixed, conv0_ref, h0_ref, rest = _ssd_parse(refs, has_init, n_alias, 9)
    z_ref, xbc_ref, dt_ref, cw_ref, cb_ref, dtb_ref, alog_ref, dexp_ref, ng_ref = fixed
    y_ref, hnew_ref, convnew_ref, stage, h_scr, pad_scr, y_scr = rest
    c = pl.program_id(1)

    @pl.when(c == 0)
    def _():
        if has_init:
            h_scr[...] = h0_ref[0]
        else:
            h_scr[...] = jnp.zeros_like(h_scr)

    xc = _ssd_conv(c, nchunks, xbc_ref, conv0_ref, cw_ref, cb_ref, convnew_ref, stage, lc)
    dtv, cs, _ = _ssd_dt_cumsum(dt_ref[...], dtb_ref, alog_ref, lc)
    causal = lax.broadcasted_iota(jnp.int32, (lc, lc), 0) >= lax.broadcasted_iota(jnp.int32, (lc, lc), 1)
    pad_scr[...] = jnp.zeros_like(pad_scr)
    pad_scr[0:lc, 0:SSD_HEADS] = cs
    cst = pad_scr[...].T
    ecs = jnp.exp(cs)
    cs_last = cs[lc - 1:lc, :]
    wdec = jnp.exp(cs_last - cs)
    elast = jnp.exp(cs_last)
    xs_off, b_off, c_off = 0, SSD_INNER, SSD_INNER + SSD_GROUPS * SSD_STATE
    for g in range(SSD_GROUPS):
        bg = xc[:, b_off + g * SSD_STATE:b_off + (g + 1) * SSD_STATE].astype(bf16)
        cg = xc[:, c_off + g * SSD_STATE:c_off + (g + 1) * SSD_STATE].astype(bf16)
        cb = _dot_nt(cg, bg)
        for r in range(SSD_HPG):
            h = g * SSD_HPG + r
            hs = slice(h * SSD_HEADDIM, (h + 1) * SSD_HEADDIM)
            col = cs[:, h:h + 1]
            row = cst[h:h + 1, 0:lc]
            decay = jnp.exp(jnp.where(causal, col - row, -jnp.inf))
            xh = xc[:, hs]
            xdt = xh * dtv[:, h:h + 1]
            hst = h_scr[h]
            y = _dot((cb * decay).astype(bf16), xdt.astype(bf16))
            y = y + _dot_nt(cg, hst.astype(bf16)) * ecs[:, h:h + 1]
            xw = (xdt * wdec[:, h:h + 1]).astype(bf16)
            h_scr[h] = elast[:, h:h + 1] * hst + _dot_tn(xw, bg)
            y_scr[:, hs] = y + dexp_ref[:, hs] * xh
        gs = slice(g * SSD_HPG * SSD_HEADDIM, (g + 1) * SSD_HPG * SSD_HEADDIM)
        yg = y_scr[:, gs] * z_ref[:, gs]
        yg = yg * lax.rsqrt(jnp.mean(yg * yg, axis=-1, keepdims=True) + EPS) * ng_ref[:, gs]
        y_ref[:, gs] = yg.astype(bf16)

    @pl.when(c == nchunks - 1)
    def _():
        hnew_ref[0] = h_scr[...]


def ssd(proj, dt_raw, conv_w, conv_b, dt_bias, a_log, d, norm_g, batch, length, *, h_all, conv_all, prev_h,
        prev_conv, layer, depth):
    lc = min(length, CHUNK)
    nc = length // lc
    dexp = jnp.repeat(d, SSD_HEADDIM).reshape(1, SSD_INNER)
    row = lambda b, c: (b * nc + c, 0)
    const2 = lambda b, c: (0, 0)
    h_tail = (SSD_HEADS, SSD_HEADDIM, SSD_STATE)
    c_tail = (SSD_CONV_W - 1, SSD_CONV_DIM)
    has_init = h_all is not None
    has_prev = prev_h is not None
    h_in, h_specs, h_out_spec = _layered_state_specs(layer * batch, h_tail, h_all, None)
    c_in, c_specs, c_out_spec = _layered_state_specs(layer * batch, c_tail, conv_all, None)
    ins = [proj, proj, dt_raw, conv_w, conv_b.reshape(1, -1), dt_bias.reshape(1, -1), a_log.reshape(1, -1), dexp,
           norm_g.reshape(1, -1)]
    specs = [pl.BlockSpec((lc, SSD_INNER), lambda b, c: (b * nc + c, OFF_Z // SSD_INNER)),
             pl.BlockSpec((lc, SSD_CONV_DIM), lambda b, c: (b * nc + c, OFF_XBC // SSD_CONV_DIM)),
             pl.BlockSpec((lc, SSD_HEADS), row),
             pl.BlockSpec((SSD_CONV_W, SSD_CONV_DIM), const2),
             pl.BlockSpec((1, SSD_CONV_DIM), const2),
             pl.BlockSpec((1, SSD_HEADS), const2),
             pl.BlockSpec((1, SSD_HEADS), const2),
             pl.BlockSpec((1, SSD_INNER), const2),
             pl.BlockSpec((1, SSD_INNER), const2)]
    kern = functools.partial(_ssd_kernel, lc=lc, nchunks=nc, has_init=has_init, n_alias=2 * has_prev)
    scratch = [pltpu.VMEM((_SSD_PAD + lc, SSD_CONV_DIM), f32),
               pltpu.VMEM(h_tail, f32),
               pltpu.VMEM((_SSD_TP, _SSD_TP), f32),
               pltpu.VMEM((lc, SSD_INNER), f32)]
    ins += c_in + h_in
    specs += c_specs + h_specs
    aliases = {}
    if has_prev:
        aliases = {len(ins): 1, len(ins) + 1: 2}
        ins += [prev_h, prev_conv]
        specs += [pl.BlockSpec(memory_space=pl.ANY), pl.BlockSpec(memory_space=pl.ANY)]
    return pl.pallas_call(
        kern,
        grid=(batch, nc),
        in_specs=specs,
        out_specs=[pl.BlockSpec((lc, SSD_INNER), row), h_out_spec, c_out_spec],
        out_shape=[jax.ShapeDtypeStruct((batch * length, SSD_INNER), bf16),
                   jax.ShapeDtypeStruct((depth * batch,) + h_tail, f32),
                   jax.ShapeDtypeStruct((depth * batch,) + c_tail, f32)],
        scratch_shapes=scratch,
        input_output_aliases=aliases,
        compiler_params=_cparams("parallel", "arbitrary"),
        name="ssd",
    )(*ins)


def ssd_wide(proj, dt_raw, dt_bias, a_log, d, norm_g, batch, length, *, h_all, prev_h, layer, depth):
    lc = SSD_HEADDIM
    per_step = 2 if length % (2 * lc) == 0 else 1
    rows = per_step * lc
    nc = length // rows
    dexp = jnp.repeat(d, SSD_HEADDIM).reshape(1, SSD_INNER)
    rep = jnp.repeat(jnp.eye(SSD_HEADS, dtype=bf16), SSD_HEADDIM, axis=1)
    row = lambda b, c: (b * nc + c, 0)
    const2 = lambda b, c: (0, 0)
    h_tail = (SSD_HEADS, SSD_HEADDIM, SSD_STATE)
    h_in, h_specs, h_out_spec = _layered_state_specs(layer * batch, h_tail, h_all, prev_h)
    ins = [proj, proj, dt_raw, dt_bias.reshape(1, -1), a_log.reshape(1, -1), dexp, norm_g.reshape(1, -1), rep]
    specs = [pl.BlockSpec((rows, SSD_INNER), lambda b, c: (b * nc + c, OFF_Z // SSD_INNER)),
             pl.BlockSpec((rows, SSD_CONV_DIM), lambda b, c: (b * nc + c, OFF_XBC // SSD_CONV_DIM)),
             pl.BlockSpec((rows, SSD_HEADS), row),
             pl.BlockSpec((1, SSD_HEADS), const2),
             pl.BlockSpec((1, SSD_HEADS), const2),
             pl.BlockSpec((1, SSD_INNER), const2),
             pl.BlockSpec((1, SSD_INNER), const2),
             pl.BlockSpec((SSD_HEADS, SSD_INNER), const2)]
    aliases = {len(ins) + len(h_in) - 1: 1} if prev_h is not None else {}
    kern = functools.partial(_ssd_wide_kernel, nsteps=nc, per_step=per_step, has_init=h_all is not None,
                             n_alias=int(prev_h is not None))
    return pl.pallas_call(
        kern,
        grid=(batch, nc),
        in_specs=specs + h_specs,
        out_specs=[pl.BlockSpec((rows, SSD_INNER), row), h_out_spec],
        out_shape=[jax.ShapeDtypeStruct((batch * length, SSD_INNER), bf16),
                   jax.ShapeDtypeStruct((depth * batch,) + h_tail, f32)],
        scratch_shapes=[pltpu.VMEM((SSD_STATE, SSD_INNER), f32),
                        pltpu.VMEM((rows, SSD_INNER), f32),
                        pltpu.VMEM((rows, SSD_INNER), f32)],
        input_output_aliases=aliases,
        compiler_params=_cparams("parallel", "arbitrary"),
        name="ssd_wide",
    )(*ins, *h_in)


def _merge_kernel(ra_ref, sa_ref, da_ref, gr_ref, gs_ref, gd_ref, wr_ref, wv_ref, wg_ref, wd_ref, o_ref):
    sa = sa_ref[...]
    ret = _dot(ra_ref[...], wr_ref[...])
    s5 = _dot(sa, wv_ref[...]) * _sigmoid(_dot(sa, wg_ref[...]))
    sd = _dot(da_ref[...], wd_ref[...])
    merged = gr_ref[...] * ret + gs_ref[...] * s5 + gd_ref[...] * sd
    o_ref[...] = merged.astype(bf16)


def merge(ret_act, s5_act, ssd_act, gates, w_ret_o, w_glu, w_ssd_out, *, tm, tn):
    n = ret_act.shape[0]
    nj = D_MODEL // tn
    act = lambda i, j: (i, 0)
    return pl.pallas_call(
        _merge_kernel,
        grid=(n // tm, nj),
        in_specs=[pl.BlockSpec((tm, R_VAL), act),
                  pl.BlockSpec((tm, D_MODEL), act),
                  pl.BlockSpec((tm, SSD_INNER), act),
                  pl.BlockSpec((tm, tn), lambda i, j: (i, j)),
                  pl.BlockSpec((tm, tn), lambda i, j: (i, nj + j)),
                  pl.BlockSpec((tm, tn), lambda i, j: (i, 2 * nj + j)),
                  pl.BlockSpec((R_VAL, tn), lambda i, j: (0, j)),
                  pl.BlockSpec((D_MODEL, tn), lambda i, j: (0, j)),
                  pl.BlockSpec((D_MODEL, tn), lambda i, j: (0, nj + j)),
                  pl.BlockSpec((SSD_INNER, tn), lambda i, j: (0, j))],
        out_specs=pl.BlockSpec((tm, tn), lambda i, j: (i, j)),
        out_shape=jax.ShapeDtypeStruct((n, D_MODEL), bf16),
        compiler_params=_cparams("parallel", "arbitrary"),
        name="merge",
    )(ret_act, s5_act, ssd_act, gates, gates, gates, w_ret_o, w_glu, w_glu, w_ssd_out)


def _attn_kernel(q_ref, k_ref, v_ref, o_ref):
    scale = X_HEAD_DIM ** -0.5
    for h in range(X_HEADS):
        hs = slice(h * X_HEAD_DIM, (h + 1) * X_HEAD_DIM)
        s = _dot_nt(q_ref[:, hs], k_ref[0, :, hs].astype(bf16)) * scale
        e = jnp.exp(s - jnp.max(s, axis=-1, keepdims=True))
        p = e / jnp.sum(e, axis=-1, keepdims=True)
        o_ref[:, hs] = _dot(p.astype(bf16), v_ref[0, :, hs].astype(bf16)).astype(bf16)


def attention(q, mem_k, mem_v, batch, length, *, tq, kv_base):
    nt = length // tq
    kv = pl.BlockSpec((1, MEM_LEN, D_MODEL), lambda b, i: (kv_base + b, 0, 0))
    return pl.pallas_call(
        _attn_kernel,
        grid=(batch, nt),
        in_specs=[pl.BlockSpec((tq, D_MODEL), lambda b, i: (b * nt + i, 0)), kv, kv],
        out_specs=pl.BlockSpec((tq, D_MODEL), lambda b, i: (b * nt + i, 0)),
        out_shape=jax.ShapeDtypeStruct((batch * length, D_MODEL), bf16),
        compiler_params=_cparams("parallel", "arbitrary"),
        name="attention",
    )(q, mem_k, mem_v)


def _mlp_kernel(x_ref, g_in_ref, g_out_ref, wu_ref, wd_ref, o_ref, hn_scr, acc_scr, *, nff):
    j = pl.program_id(1)

    @pl.when(j == 0)
    def _():
        hn_scr[...] = _rms(x_ref[...], g_in_ref[...]).astype(bf16)
        acc_scr[...] = jnp.zeros_like(acc_scr)

    h = jnp.maximum(_dot(hn_scr[...], wu_ref[...]), 0.0)
    acc_scr[...] += _dot((h * h).astype(bf16), wd_ref[...])

    @pl.when(j == nff - 1)
    def _():
        o_ref[...] = x_ref[...] + _rms(acc_scr[...], g_out_ref[...])


def mlp(x, g_in, g_out, w_up, w_down, *, tm, tf):
    n, d = x.shape
    nff = D_FF // tf
    kern = functools.partial(_mlp_kernel, nff=nff)
    return pl.pallas_call(
        kern,
        grid=(n // tm, nff),
        in_specs=[pl.BlockSpec((tm, d), lambda i, j: (i, 0)),
                  pl.BlockSpec((1, d), lambda i, j: (0, 0)),
                  pl.BlockSpec((1, d), lambda i, j: (0, 0)),
                  pl.BlockSpec((d, tf), lambda i, j: (0, j)),
                  pl.BlockSpec((tf, d), lambda i, j: (j, 0))],
        out_specs=pl.BlockSpec((tm, d), lambda i, j: (i, 0)),
        out_shape=jax.ShapeDtypeStruct((n, d), f32),
        scratch_shapes=[pltpu.VMEM((tm, d), bf16), pltpu.VMEM((tm, d), f32)],
        compiler_params=_cparams("parallel", "arbitrary"),
        name="mlp",
    )(x, g_in, g_out, w_up, w_down)


def _run_layer(x, batch, length, pos0, mem_k, mem_v, kv_base, carried, prev, layer, depth, lw, *, prompt):
    gains = lw["gains"]
    n = x.shape[0]
    tm = min(512, n)
    tm_in = min(1024, n)
    conv_args = {}
    if prompt:
        conv_args = dict(conv_w=lw["ssd_conv_w"], conv_b=lw["ssd_conv_b"], seq_len=length)
    proj, tails = inproj(x, gains[0:1], lw["w_in_all"], layer, tm=tm_in, **conv_args)
    dt_raw = norm_matmul(x, gains[0:1], lw["w_dt"], tm=tm_in, tn=SSD_HEADS, out_dtype=f32)
    gates = norm_matmul(x, gains[0:1], lw["w_gate"], tm=tm_in, tn=1024, out_dtype=f32, sigmoid=True)
    ret_act, ret_all = retention(proj, lw["ret_gn"], batch, length, pos0, state_all=carried["ret"],
                                 prev_out=prev["ret"], layer=layer, depth=depth)
    if prompt:
        s5_act, s5_r_new, s5_i_new = s5_prompt(proj, lw["s5_d"], lw["s5_mats"], batch, length)
    else:
        s5_act, s5_r_new, s5_i_new = s5_sample(proj, lw["s5_d"], lw["s5_mats"], carried["s5_r"], carried["s5_i"],
                                               batch, length)
    if prompt:
        ssd_act, ssd_all = ssd_wide(proj, dt_raw, lw["ssd_dt_bias"], lw["ssd_a_log"], lw["ssd_d"], lw["ssd_norm"],
                                    batch, length, h_all=carried["ssd"], prev_h=prev["ssd"], layer=layer, depth=depth)
        tiles_per_seq = length // tm_in
        seq_tails = tails.reshape(batch, tiles_per_seq, _CONV_PAD, SSD_CONV_DIM)[:, -1, _CONV_PAD - (SSD_CONV_W - 1):]
        conv_all = seq_tails if prev["conv"] is None else jnp.concatenate([prev["conv"], seq_tails], axis=0)
    else:
        ssd_act, ssd_all, conv_all = ssd(proj, dt_raw, lw["ssd_conv_w"], lw["ssd_conv_b"], lw["ssd_dt_bias"],
                                         lw["ssd_a_log"], lw["ssd_d"], lw["ssd_norm"], batch, length,
                                         h_all=carried["ssd"], conv_all=carried["conv"], prev_h=prev["ssd"],
                                         prev_conv=prev["conv"], layer=layer, depth=depth)
    merged = merge(ret_act, s5_act, ssd_act, gates, lw["w_ret_o"], lw["w_s5_glu"], lw["w_ssd_out"], tm=tm, tn=512)
    x = matmul_norm_res(merged, lw["w_mix_out"], gains[1:2], x, tm=tm)
    q = norm_matmul(x, gains[2:3], lw["w_xq"], tm=tm_in, tn=1024, out_dtype=bf16)
    att = attention(q, mem_k, mem_v, batch, length, tq=min(length, 512), kv_base=kv_base)
    x = matmul_norm_res(att, lw["w_xo"], gains[3:4], x, tm=tm)
    x = mlp(x, gains[4:5], gains[5:6], lw["w_up"], lw["w_down"], tm=tm, tf=1024)
    return x, dict(ret=ret_all, ssd=ssd_all, conv=conv_all), (s5_r_new, s5_i_new)


def kernel(x_prompt, x_sample, mem_prompt, state_ret, state_s5_re, state_s5_im, state_ssd, cache_ssd_conv,
           cache_mem_k, cache_mem_v, norm_gains, w_in, ret_gn, w_ret_o, s5_a_re, s5_a_im, s5_b_re, s5_b_im,
           s5_c_re, s5_c_im, s5_d, s5_log_dt, w_s5_glu, ssd_conv_w, ssd_conv_b, ssd_dt_bias, ssd_a_log, ssd_d,
           ssd_norm, w_ssd_out, w_mix_out, w_xq, w_xkv, w_xo, w_up, w_down):
    b, l, _ = x_prompt.shape
    db, dl, _ = x_sample.shape
    depth = w_in.shape[0]
    yp = x_prompt.reshape(b * l, D_MODEL)
    ys = x_sample.reshape(db * dl, D_MODEL)
    mem2d = mem_prompt.reshape(b * MEM_LEN, D_MODEL)
    cache_k = cache_mem_k.reshape(depth * db, MEM_LEN, D_MODEL)
    cache_v = cache_mem_v.reshape(depth * db, MEM_LEN, D_MODEL)
    s_carried = dict(ret=state_ret.reshape(depth * db, R_HEADS, R_DK, R_DV),
                     ssd=state_ssd.reshape(depth * db, SSD_HEADS, SSD_HEADDIM, SSD_STATE),
                     conv=cache_ssd_conv.reshape(depth * db, SSD_CONV_W - 1, SSD_CONV_DIM))
    p_carried = dict(ret=None, ssd=None, conv=None)
    p_prev = dict(ret=None, ssd=None, conv=None)
    s_prev = dict(ret=None, ssd=None, conv=None)
    p_s5, s_s5, p_mk, p_mv = [], [], [], []
    w_in_bf = w_in.astype(bf16)
    for i in range(depth):
        ar, ai, lr, li, bbr, bbi = s5_prep(s5_a_re[i], s5_a_im[i], s5_log_dt[i],
                                           jnp.swapaxes(s5_b_re[i], 1, 2), jnp.swapaxes(s5_b_im[i], 1, 2))
        lane = lambda t: t.reshape(1, S5_LANES)
        lw = dict(
            gains=norm_gains[i],
            w_in_all=w_in_bf,
            w_dt=w_in_bf[i, :, OFF_DT:OFF_GATE],
            w_gate=w_in_bf[i, :, OFF_GATE:],
            ret_gn=ret_gn[i].reshape(1, R_VAL),
            w_ret_o=w_ret_o[i].astype(bf16),
            s5_d=s5_d[i].reshape(1, D_MODEL),
            s5_mats=(_s5_blockdiag_in(bbr), _s5_blockdiag_in(bbi),
                     _s5_blockdiag_out(s5_c_re[i]), _s5_blockdiag_out(s5_c_im[i]),
                     lane(ar), lane(ai), lane(lr), lane(li)),
            w_s5_glu=w_s5_glu[i].astype(bf16),
            ssd_conv_w=ssd_conv_w[i], ssd_conv_b=ssd_conv_b[i], ssd_dt_bias=ssd_dt_bias[i],
            ssd_a_log=ssd_a_log[i], ssd_d=ssd_d[i], ssd_norm=ssd_norm[i],
            w_ssd_out=w_ssd_out[i].astype(bf16),
            w_mix_out=w_mix_out[i].astype(bf16),
            w_xq=w_xq[i].astype(bf16),
            w_xo=w_xo[i].astype(bf16),
            w_up=w_up[i].astype(bf16),
            w_down=w_down[i].astype(bf16),
        )
        kv = norm_matmul(mem2d, norm_gains[i, 6:7], w_xkv[i].astype(bf16), tm=b * MEM_LEN, tn=512, out_dtype=f32)
        mk = kv[:, :D_MODEL].reshape(b, MEM_LEN, D_MODEL)
        mv = kv[:, D_MODEL:].reshape(b, MEM_LEN, D_MODEL)
        p_mk.append(mk.reshape(b, MEM_LEN, X_HEADS, X_HEAD_DIM))
        p_mv.append(mv.reshape(b, MEM_LEN, X_HEADS, X_HEAD_DIM))
        yp, p_prev, s5p = _run_layer(yp, b, l, 0, mk, mv, 0, p_carried, p_prev, i, depth, lw, prompt=True)
        p_s5.append(s5p)
        s_carried_i = dict(s_carried, s5_r=state_s5_re[i], s5_i=state_s5_im[i])
        ys, s_prev, s5s = _run_layer(ys, db, dl, PAST_LEN, cache_k, cache_v, i * db, s_carried_i, s_prev, i, depth,
                                     lw, prompt=False)
        s_s5.append(s5s)
    p_s5_re, p_s5_im = [jnp.stack(t) for t in zip(*p_s5)]
    s_s5_re, s_s5_im = [jnp.stack(t) for t in zip(*s_s5)]
    unstack = lambda t, bsz: t.reshape((depth, bsz) + t.shape[1:])
    return (yp.reshape(b, l, D_MODEL), ys.reshape(db, dl, D_MODEL),
            unstack(p_prev["ret"], b), p_s5_re, p_s5_im, unstack(p_prev["ssd"], b), unstack(p_prev["conv"], b),
            jnp.stack(p_mk), jnp.stack(p_mv),
            unstack(s_prev["ret"], db), s_s5_re, s_s5_im, unstack(s_prev["ssd"], db), unstack(s_prev["conv"], db))
```

```python
import functools
import math

import jax
import jax.numpy as jnp
from jax import lax
from jax.experimental import pallas as pl
from jax.experimental.pallas import tpu as pltpu

f32 = jnp.float32
bf16 = jnp.bfloat16

D_MODEL = 2048
PAST_LEN = 1024
CHUNK = 64
EPS = 1e-6
R_HEADS = 8
R_DK = 128
R_DV = 256
R_QK = R_HEADS * R_DK
R_VAL = R_HEADS * R_DV
ROPE_BASE = 10000.0
S5_GROUP = 16
S5_GROUPS = D_MODEL // S5_GROUP
S5_STATE = 64
S5_LANES = S5_GROUPS * S5_STATE
S5_GT = 16
S5_NGT = S5_GROUPS // S5_GT
S5_TU = S5_GT * S5_GROUP
S5_TL = S5_GT * S5_STATE
SSD_INNER = 2 * D_MODEL
SSD_HEADDIM = 64
SSD_HEADS = SSD_INNER // SSD_HEADDIM
SSD_GROUPS = 8
SSD_HPG = SSD_HEADS // SSD_GROUPS
SSD_STATE = 128
SSD_CONV_W = 4
SSD_CONV_DIM = SSD_INNER + 2 * SSD_GROUPS * SSD_STATE
MEM_LEN = 256
X_HEADS = 4
X_HEAD_DIM = D_MODEL // X_HEADS
D_FF = 4 * D_MODEL
OFF_Q = 0
OFF_K = OFF_Q + R_QK
OFF_V = OFF_K + R_QK
OFF_G = OFF_V + R_VAL
OFF_U = OFF_G + R_VAL
OFF_Z = OFF_U + D_MODEL
OFF_XBC = OFF_Z + SSD_INNER
OFF_DT = OFF_XBC + SSD_CONV_DIM
OFF_GATE = OFF_DT + SSD_HEADS
MAIN_COLS = OFF_DT

VMEM_LIMIT_BYTES = 52 * 1024 * 1024


def _cparams(*sem):
    return pltpu.CompilerParams(dimension_semantics=sem, vmem_limit_bytes=VMEM_LIMIT_BYTES)


def _rms(x, g):
    return x * lax.rsqrt(jnp.mean(x * x, axis=-1, keepdims=True) + EPS) * g


def _dot(a, b):
    return jnp.dot(a, b, preferred_element_type=f32)


def _dot_nt(a, b):
    return lax.dot_general(a, b, (((1,), (1,)), ((), ())), preferred_element_type=f32)


def _dot_tn(a, b):
    return lax.dot_general(a, b, (((0,), (0,)), ((), ())), preferred_element_type=f32)


def _sigmoid(x):
    return 1.0 / (1.0 + jnp.exp(-x))


def _norm_matmul_kernel(x_ref, g_ref, w_ref, o_ref, hn_ref, *, sigmoid):
    @pl.when(pl.program_id(1) == 0)
    def _():
        hn_ref[...] = _rms(x_ref[...], g_ref[...]).astype(bf16)

    y = _dot(hn_ref[...], w_ref[...])
    if sigmoid:
        y = _sigmoid(y)
    o_ref[...] = y.astype(o_ref.dtype)


def norm_matmul(x, g, w, *, tm, tn, out_dtype, sigmoid=False):
    n, k = x.shape
    m = w.shape[1]
    return pl.pallas_call(
        functools.partial(_norm_matmul_kernel, sigmoid=sigmoid),
        grid=(n // tm, m // tn),
        in_specs=[pl.BlockSpec((tm, k), lambda i, j: (i, 0)),
                  pl.BlockSpec((1, k), lambda i, j: (0, 0)),
                  pl.BlockSpec((k, tn), lambda i, j: (0, j))],
        out_specs=pl.BlockSpec((tm, tn), lambda i, j: (i, j)),
        out_shape=jax.ShapeDtypeStruct((n, m), out_dtype),
        scratch_shapes=[pltpu.VMEM((tm, k), bf16)],
        compiler_params=_cparams("parallel", "arbitrary"),
        name="norm_matmul",
    )(x, g, w)


IN_TN = 1024
_J_G = (OFF_G // IN_TN, OFF_U // IN_TN)
_J_Z = (OFF_Z // IN_TN, OFF_XBC // IN_TN)
_J_XBC = OFF_XBC // IN_TN
_NJ_MAIN = MAIN_COLS // IN_TN
_CONV_PAD = 8
_IN_SUB = 256


def _inproj_kernel(*refs, tm, conv, tiles_per_seq):
    if conv:
        x_ref, g_ref, w_ref, cw_ref, cb_ref, o_ref, tails_ref, hn_ref, tail_scr = refs
    else:
        x_ref, g_ref, w_ref, o_ref, hn_ref = refs
    i = pl.program_id(0)
    j = pl.program_id(1)

    @pl.when(j == 0)
    def _():
        hn_ref[...] = _rms(x_ref[...], g_ref[...]).astype(bf16)

    is_silu = ((j >= _J_G[0]) & (j < _J_G[1])) | ((j >= _J_Z[0]) & (j < _J_Z[1]))
    is_plain = jnp.logical_not(is_silu)
    if conv:
        is_plain = is_plain & (j < _J_XBC)
    subs = [slice(s * _IN_SUB, (s + 1) * _IN_SUB) for s in range(IN_TN // _IN_SUB)]

    @pl.when(is_silu)
    def _():
        for cs in subs:
            acc = _dot(hn_ref[...], w_ref[:, cs])
            o_ref[:, cs] = acc * _sigmoid(acc)

    @pl.when(is_plain)
    def _():
        o_ref[...] = _dot(hn_ref[...], w_ref[...])

    if conv:
        @pl.when(j >= _J_XBC)
        def _():
            jj = j - _J_XBC

            @pl.when(i % tiles_per_seq == 0)
            def _():
                tail_scr[jj] = jnp.zeros((_CONV_PAD, IN_TN), f32)

            for cs in subs:
                acc = _dot(hn_ref[...], w_ref[:, cs])
                ext = jnp.concatenate([tail_scr[jj, :, cs], acc], axis=0)
                h = cw_ref[0:1, cs] * ext
                for t in range(1, SSD_CONV_W):
                    h = pltpu.roll(h, 1, 0) + cw_ref[t:t + 1, cs] * ext
                xc = h[_CONV_PAD:, :] + cb_ref[:, cs]
                o_ref[:, cs] = xc * _sigmoid(xc)
                last = acc[tm - _CONV_PAD:, :]
                tail_scr[jj, :, cs] = last
                tails_ref[0, :, cs] = last


def inproj(x, g, w_all, layer, *, tm, conv_w=None, conv_b=None, seq_len=None):
    n, k = x.shape
    conv = conv_w is not None
    nxbc = SSD_CONV_DIM // IN_TN
    in_specs = [pl.BlockSpec((tm, k), lambda i, j: (i, 0)),
                pl.BlockSpec((1, k), lambda i, j: (0, 0)),
                pl.BlockSpec((None, k, IN_TN), lambda i, j: (layer, 0, j))]
    out_specs = pl.BlockSpec((tm, IN_TN), lambda i, j: (i, j))
    out_shape = jax.ShapeDtypeStruct((n, MAIN_COLS), f32)
    scratch = [pltpu.VMEM((tm, k), bf16)]
    args = [x, g, w_all]
    tiles_per_seq = None
    if conv:
        xcol = lambda i, j: (0, jnp.maximum(j - _J_XBC, 0))
        in_specs += [pl.BlockSpec((SSD_CONV_W, IN_TN), xcol), pl.BlockSpec((1, IN_TN), xcol)]
        out_specs = [out_specs, pl.BlockSpec((1, _CONV_PAD, IN_TN), lambda i, j: (i, 0, jnp.maximum(j - _J_XBC, 0)))]
        out_shape = [out_shape, jax.ShapeDtypeStruct((n // tm, _CONV_PAD, SSD_CONV_DIM), f32)]
        scratch += [pltpu.VMEM((nxbc, _CONV_PAD, IN_TN), f32)]
        args += [conv_w, conv_b.reshape(1, -1)]
        tiles_per_seq = seq_len // tm
    kern = functools.partial(_inproj_kernel, tm=tm, conv=conv, tiles_per_seq=tiles_per_seq)
    out = pl.pallas_call(
        kern,
        grid=(n // tm, _NJ_MAIN),
        in_specs=in_specs,
        out_specs=out_specs,
        out_shape=out_shape,
        scratch_shapes=scratch,
        compiler_params=_cparams("arbitrary", "arbitrary"),
        name="inproj",
    )(*args)
    return out if conv else (out, None)


def _matmul_norm_res_kernel(a_ref, w_ref, g_ref, x_ref, o_ref):
    y = _dot(a_ref[...], w_ref[...])
    o_ref[...] = x_ref[...] + _rms(y, g_ref[...])


def matmul_norm_res(a, w, g, x, *, tm):
    n, k = a.shape
    d = w.shape[1]
    return pl.pallas_call(
        _matmul_norm_res_kernel,
        grid=(n // tm,),
        in_specs=[pl.BlockSpec((tm, k), lambda i: (i, 0)),
                  pl.BlockSpec((k, d), lambda i: (0, 0)),
                  pl.BlockSpec((1, d), lambda i: (0, 0)),
                  pl.BlockSpec((tm, d), lambda i: (i, 0))],
        out_specs=pl.BlockSpec((tm, d), lambda i: (i, 0)),
        out_shape=jax.ShapeDtypeStruct((n, d), f32),
        compiler_params=_cparams("parallel"),
        name="matmul_norm_res",
    )(a, w, g, x)


def _layered_state_specs(layer_rows, tail, state_all, prev_out):
    zeros = (0,) * len(tail)
    spec = pl.BlockSpec((1,) + tail, lambda b, c: (layer_rows + b,) + zeros)
    ins, specs = [], []
    if state_all is not None:
        ins.append(state_all)
        specs.append(spec)
    if prev_out is not None:
        ins.append(prev_out)
        specs.append(pl.BlockSpec(memory_space=pl.ANY))
    return ins, specs, spec


def _retention_kernel(*refs, lc, per_step, nsteps, has_init, n_alias):
    q_ref, k_ref, v_ref, g_ref, cos_ref, sin_ref, intra_ref, din_ref, dup_ref, dall_ref, gn_ref = refs[:11]
    s0_ref = refs[11] if has_init else None
    o_ref, snew_ref, s_scr = refs[11 + has_init + n_alias:]
    c = pl.program_id(1)

    @pl.when(c == 0)
    def _():
        if has_init:
            s_scr[...] = s0_ref[0]
        else:
            s_scr[...] = jnp.zeros_like(s_scr)

    for cc, h in [(cc, h) for cc in range(per_step) for h in range(R_HEADS)]:
        rows = slice(cc * lc, (cc + 1) * lc)
        cos = cos_ref[rows, :]
        sin = sin_ref[rows, :]
        q = q_ref[rows, h * R_DK:(h + 1) * R_DK]
        k = k_ref[rows, h * R_DK:(h + 1) * R_DK]
        v = v_ref[rows, h * R_DV:(h + 1) * R_DV].astype(bf16)
        qr = (q * cos + pltpu.roll(q, R_DK // 2, 1) * sin)
        kr = (k * cos + pltpu.roll(k, R_DK // 2, 1) * sin) * (R_DK ** -0.5)
        qb = qr.astype(bf16)
        s = s_scr[h]
        scores = _dot_nt(qb, kr.astype(bf16)) * intra_ref[h]
        out = _dot(scores.astype(bf16), v) + _dot(qb, s.astype(bf16)) * din_ref[h]
        s_scr[h] = dall_ref[h] * s + _dot_tn((kr * dup_ref[h]).astype(bf16), v)
        mu = jnp.mean(out, axis=-1, keepdims=True)
        cen = out - mu
        var = jnp.mean(cen * cen, axis=-1, keepdims=True)
        o = cen * lax.rsqrt(var + EPS) * gn_ref[:, h * R_DV:(h + 1) * R_DV]
        o_ref[rows, h * R_DV:(h + 1) * R_DV] = (g_ref[rows, h * R_DV:(h + 1) * R_DV] * o).astype(bf16)

    @pl.when(c == nsteps - 1)
    def _():
        snew_ref[0] = s_scr[...]


def retention(proj, gn, batch, length, pos0, *, state_all, prev_out, layer, depth):
    lc = min(length, CHUNK)
    per_step = max(p for p in (4, 2, 1) if length % (p * lc) == 0)
    rows = per_step * lc
    nc = length // rows
    half = R_DK // 2
    inv_freq = jnp.exp(-math.log(ROPE_BASE) * jnp.arange(half, dtype=f32) / half)
    pos = (pos0 + jnp.arange(length, dtype=jnp.int32)).astype(f32)
    ang = pos[:, None] * inv_freq[None]
    cos2 = jnp.concatenate([jnp.cos(ang), jnp.cos(ang)], axis=-1)
    sin2 = jnp.concatenate([-jnp.sin(ang), jnp.sin(ang)], axis=-1)
    log_gamma = jnp.log1p(-jnp.exp2(-5.0 - jnp.arange(R_HEADS, dtype=f32)))
    idx = jnp.arange(lc, dtype=f32)
    lg = log_gamma[:, None]
    intra = jnp.exp(jnp.abs(idx[:, None] - idx[None, :])[None] * lg[:, :, None])
    din = jnp.broadcast_to(jnp.exp((idx + 1.0)[None] * lg)[:, :, None], (R_HEADS, lc, R_DV))
    dup = jnp.broadcast_to(jnp.exp((lc - 1.0 - idx)[None] * lg)[:, :, None], (R_HEADS, lc, R_DK))
    dall = jnp.exp(lc * log_gamma)
    tail = (R_HEADS, R_DK, R_DV)
    st_in, st_specs, st_out_spec = _layered_state_specs(layer * batch, tail, state_all, prev_out)
    n_fixed = 11
    kern = functools.partial(_retention_kernel, lc=lc, per_step=per_step, nsteps=nc,
                             has_init=state_all is not None, n_alias=int(prev_out is not None))
    aliases = {n_fixed + len(st_in) - 1: 1} if prev_out is not None else {}
    return pl.pallas_call(
        kern,
        grid=(batch, nc),
        in_specs=[pl.BlockSpec((rows, R_QK), lambda b, c: (b * nc + c, OFF_Q // R_QK)),
                  pl.BlockSpec((rows, R_QK), lambda b, c: (b * nc + c, OFF_K // R_QK)),
                  pl.BlockSpec((rows, R_VAL), lambda b, c: (b * nc + c, OFF_V // R_VAL)),
                  pl.BlockSpec((rows, R_VAL), lambda b, c: (b * nc + c, OFF_G // R_VAL)),
                  pl.BlockSpec((rows, R_DK), lambda b, c: (c, 0)),
                  pl.BlockSpec((rows, R_DK), lambda b, c: (c, 0)),
                  pl.BlockSpec((R_HEADS, lc, lc), lambda b, c: (0, 0, 0)),
                  pl.BlockSpec((R_HEADS, lc, R_DV), lambda b, c: (0, 0, 0)),
                  pl.BlockSpec((R_HEADS, lc, R_DK), lambda b, c: (0, 0, 0)),
                  pl.BlockSpec(memory_space=pltpu.SMEM),
                  pl.BlockSpec((1, R_VAL), lambda b, c: (0, 0))] + st_specs,
        out_specs=[pl.BlockSpec((rows, R_VAL), lambda b, c: (b * nc + c, 0)), st_out_spec],
        out_shape=[jax.ShapeDtypeStruct((batch * length, R_VAL), bf16),
                   jax.ShapeDtypeStruct((depth * batch,) + tail, f32)],
        scratch_shapes=[pltpu.VMEM(tail, f32)],
        input_output_aliases=aliases,
        compiler_params=_cparams("parallel", "arbitrary"),
        name="retention",
    )(proj, proj, proj, proj, cos2, sin2, intra, din, dup, dall, gn, *st_in)


def _s5_prep_kernel(are_ref, aim_ref, ldt_ref, bre_ref, bim_ref, ar_ref, ai_ref, lr_ref, li_ref, bbr_ref, bbi_ref):
    a_re = are_ref[...]
    a_im = aim_ref[...]
    dt = jnp.exp(ldt_ref[...])
    lr = dt * a_re
    li = dt * a_im
    mag = jnp.exp(lr)
    ar = mag * jnp.cos(li)
    ai = mag * jnp.sin(li)
    nr = ar - 1.0
    den = a_re * a_re + a_im * a_im
    cr = (nr * a_re + ai * a_im) / den
    ci = (ai * a_re - nr * a_im) / den
    ar_ref[...] = ar
    ai_ref[...] = ai
    lr_ref[...] = lr
    li_ref[...] = li
    b_re = bre_ref[...]
    b_im = bim_ref[...]
    bbr_ref[...] = cr[:, None, :] * b_re - ci[:, None, :] * b_im
    bbi_ref[...] = cr[:, None, :] * b_im + ci[:, None, :] * b_re


def s5_prep(a_re, a_im, log_dt, b_re, b_im):
    gp = jax.ShapeDtypeStruct((S5_GROUPS, S5_STATE), f32)
    gb = jax.ShapeDtypeStruct((S5_GROUPS, S5_GROUP, S5_STATE), f32)
    return pl.pallas_call(
        _s5_prep_kernel,
        out_shape=[gp, gp, gp, gp, gb, gb],
        compiler_params=pltpu.CompilerParams(vmem_limit_bytes=VMEM_LIMIT_BYTES),
        name="s5_prep",
    )(a_re, a_im, log_dt.reshape(S5_GROUPS, 1), b_re, b_im)


def _s5_project_in(u_ref, perm_ref, bre_ref, bim_ref, bur_scr, bui_scr):
    ub = u_ref[...].astype(bf16)
    up = _dot(perm_ref[...], ub).astype(bf16)
    bur_scr[...] = _dot(up, bre_ref[0])
    bui_scr[...] = _dot(up, bim_ref[0])


def _s5_scan(bur_scr, bui_scr, ar_row, ai_row, init_r, init_i, nseq, steps):
    lw = min(S5_TL, S5_LANES // nseq)
    fin_r, fin_i = [], []
    for lb in range(S5_TL // lw):
        sl = slice(lb * lw, (lb + 1) * lw)
        ar = jnp.broadcast_to(ar_row[:, sl], (nseq, lw))
        ai = jnp.broadcast_to(ai_row[:, sl], (nseq, lw))
        xr = init_r[:, sl]
        xi = init_i[:, sl]
        for t in range(steps):
            rows = slice(t * nseq, (t + 1) * nseq)
            nxr = ar * xr - ai * xi + bur_scr[rows, sl]
            nxi = ar * xi + ai * xr + bui_scr[rows, sl]
            xr, xi = nxr, nxi
            bur_scr[rows, sl] = xr
            bui_scr[rows, sl] = xi
        fin_r.append(xr)
        fin_i.append(xi)
    return fin_r, fin_i, lw


def _s5_output(u_ref, xr, xi, cre_ref, cim_ref, d_ref, permt_ref, y_ref):
    y = _dot(xr, cre_ref[0]) - _dot(xi, cim_ref[0])
    pt = permt_ref[...]
    y0, y1, y2 = _split3(y)
    y = _dot(pt, y0) + _dot(pt, y1) + _dot(pt, y2)
    y_ref[...] = jax.nn.gelu(y + d_ref[...] * u_ref[...]).astype(bf16)


def _s5_prompt_kernel(u_ref, d_ref, bre_ref, bim_ref, cre_ref, cim_ref, ar_ref, ai_ref, lr_ref, li_ref,
                      perm_ref, permt_ref, y_ref, hr_ref, hi_ref,
                      bur_scr, bui_scr, xbr_scr, xbi_scr, apr_scr, api_scr, xsr_scr, xsi_scr, car_scr, cai_scr,
                      *, nseq, steps, nsc):
    b = pl.program_id(1)
    sc = pl.program_id(2)

    @pl.when((b == 0) & (sc == 0))
    def _():
        tt = (lax.broadcasted_iota(jnp.int32, (steps, S5_TL), 0) + 1).astype(f32)
        mag = jnp.exp(tt * lr_ref[...])
        pwr = mag * jnp.cos(tt * li_ref[...])
        pwi = mag * jnp.sin(tt * li_ref[...])
        for t in range(steps):
            apr_scr[t * nseq:(t + 1) * nseq, :] = jnp.broadcast_to(pwr[t:t + 1, :], (nseq, S5_TL))
            api_scr[t * nseq:(t + 1) * nseq, :] = jnp.broadcast_to(pwi[t:t + 1, :], (nseq, S5_TL))

    @pl.when(sc == 0)
    def _():
        car_scr[...] = jnp.zeros_like(car_scr)
        cai_scr[...] = jnp.zeros_like(cai_scr)

    _s5_project_in(u_ref, perm_ref, bre_ref, bim_ref, bur_scr, bui_scr)
    zero = jnp.zeros((nseq, S5_TL), f32)
    _s5_scan(bur_scr, bui_scr, ar_ref[...], ai_ref[...], zero, zero, nseq, steps)
    pr = apr_scr[(steps - 1) * nseq:(steps - 1) * nseq + 1, :]
    pi = api_scr[(steps - 1) * nseq:(steps - 1) * nseq + 1, :]
    xr = car_scr[...]
    xi = cai_scr[...]
    for r in range(nseq):
        xsr_scr[r:r + 1, :] = xr
        xsi_scr[r:r + 1, :] = xi
        er = bur_scr[(steps - 1) * nseq + r:(steps - 1) * nseq + r + 1, :]
        ei = bui_scr[(steps - 1) * nseq + r:(steps - 1) * nseq + r + 1, :]
        xr, xi = er + pr * xr - pi * xi, ei + pr * xi + pi * xr
    car_scr[...] = xr
    cai_scr[...] = xi
    blk = 2 * nseq
    lq = S5_TL // 4
    for q in range(S5_TL // lq):
        sl = slice(q * lq, (q + 1) * lq)
        xsr = jnp.concatenate([xsr_scr[:, sl], xsr_scr[:, sl]], axis=0)
        xsi = jnp.concatenate([xsi_scr[:, sl], xsi_scr[:, sl]], axis=0)
        for t2 in range(nseq * steps // blk):
            rows = slice(t2 * blk, (t2 + 1) * blk)
            ptr = apr_scr[rows, sl]
            pti = api_scr[rows, sl]
            xbr_scr[rows, sl] = (bur_scr[rows, sl] + ptr * xsr - pti * xsi).astype(bf16)
            xbi_scr[rows, sl] = (bui_scr[rows, sl] + ptr * xsi + pti * xsr).astype(bf16)
    _s5_output(u_ref, xbr_scr[...], xbi_scr[...], cre_ref, cim_ref, d_ref, permt_ref, y_ref)

    @pl.when(sc == nsc - 1)
    def _():
        hr_ref[0] = car_scr[...]
        hi_ref[0] = cai_scr[...]


def _s5_sample_kernel(u_ref, d_ref, bre_ref, bim_ref, cre_ref, cim_ref, ar_ref, ai_ref, perm_ref, permt_ref,
                      h0r_ref, h0i_ref, y_ref, hr_ref, hi_ref, bur_scr, bui_scr, *, nseq, steps):
    _s5_project_in(u_ref, perm_ref, bre_ref, bim_ref, bur_scr, bui_scr)
    fin_r, fin_i, lw = _s5_scan(bur_scr, bui_scr, ar_ref[...], ai_ref[...], h0r_ref[...], h0i_ref[...], nseq, steps)
    for lb in range(len(fin_r)):
        hr_ref[:, lb * lw:(lb + 1) * lw] = fin_r[lb]
        hi_ref[:, lb * lw:(lb + 1) * lw] = fin_i[lb]
    _s5_output(u_ref, bur_scr[...].astype(bf16), bui_scr[...].astype(bf16), cre_ref, cim_ref, d_ref, permt_ref,
               y_ref)


def _s5_blockdiag_in(bb):
    t = bb.reshape(S5_NGT, S5_GT, S5_GROUP, S5_STATE)
    eye = jnp.eye(S5_GT, dtype=bool)
    out = jnp.where(eye[None, :, None, :, None], t[:, :, :, None, :], 0.0)
    return out.reshape(S5_NGT, S5_TU, S5_TL).astype(bf16)


def _s5_blockdiag_out(c):
    t = jnp.transpose(c.reshape(S5_NGT, S5_GT, S5_GROUP, S5_STATE), (0, 1, 3, 2))
    eye = jnp.eye(S5_GT, dtype=bool)
    out = jnp.where(eye[None, :, None, :, None], t[:, :, :, None, :], 0.0)
    return out.reshape(S5_NGT, S5_TL, S5_TU).astype(bf16)


def _s5_perm(nseq, steps):
    n = nseq * steps
    dst = jnp.arange(n)
    src = (dst % nseq) * steps + dst // nseq
    p = src[:, None] == jnp.arange(n)[None, :]
    return jnp.where(p, 1.0, 0.0).astype(bf16), jnp.where(p.T, 1.0, 0.0).astype(bf16)


def _s5_common_specs(rows, row_map):
    lane = lambda *a: (0, a[0])
    mat = lambda *a: (a[0], 0, 0)
    return [pl.BlockSpec((rows, S5_TU), row_map),
            pl.BlockSpec((1, S5_TU), lane),
            pl.BlockSpec((1, S5_TU, S5_TL), mat),
            pl.BlockSpec((1, S5_TU, S5_TL), mat),
            pl.BlockSpec((1, S5_TL, S5_TU), mat),
            pl.BlockSpec((1, S5_TL, S5_TU), mat),
            pl.BlockSpec((1, S5_TL), lane),
            pl.BlockSpec((1, S5_TL), lane)]


def s5_prompt(proj, d, mats, batch, length):
    bre, bim, cre, cim, ar, ai, lr, li = mats
    nseq, steps = 8, CHUNK
    rows = nseq * steps
    nsc = length // rows
    ucol = OFF_U // S5_TU
    kern = functools.partial(_s5_prompt_kernel, nseq=nseq, steps=steps, nsc=nsc)
    lane = lambda g, b, s: (0, g)
    const = lambda g, b, s: (0, 0)
    perm, permt = _s5_perm(nseq, steps)
    st = jax.ShapeDtypeStruct((batch, 1, S5_LANES), f32)
    y, hr, hi = pl.pallas_call(
        kern,
        grid=(S5_NGT, batch, nsc),
        in_specs=_s5_common_specs(rows, lambda g, b, s: (b * nsc + s, ucol + g))
        + [pl.BlockSpec((1, S5_TL), lane), pl.BlockSpec((1, S5_TL), lane),
           pl.BlockSpec((rows, rows), const), pl.BlockSpec((rows, rows), const)],
        out_specs=[pl.BlockSpec((rows, S5_TU), lambda g, b, s: (b * nsc + s, g)),
                   pl.BlockSpec((1, 1, S5_TL), lambda g, b, s: (b, 0, g)),
                   pl.BlockSpec((1, 1, S5_TL), lambda g, b, s: (b, 0, g))],
        out_shape=[jax.ShapeDtypeStruct((batch * length, D_MODEL), bf16), st, st],
        scratch_shapes=[pltpu.VMEM((rows, S5_TL), f32), pltpu.VMEM((rows, S5_TL), f32),
                        pltpu.VMEM((rows, S5_TL), bf16), pltpu.VMEM((rows, S5_TL), bf16),
                        pltpu.VMEM((rows, S5_TL), f32), pltpu.VMEM((rows, S5_TL), f32),
                        pltpu.VMEM((nseq, S5_TL), f32), pltpu.VMEM((nseq, S5_TL), f32),
                        pltpu.VMEM((1, S5_TL), f32), pltpu.VMEM((1, S5_TL), f32)],
        compiler_params=_cparams("arbitrary", "arbitrary", "arbitrary"),
        name="s5_prompt",
    )(proj, d, bre, bim, cre, cim, ar, ai, lr, li, perm, permt)
    return y, hr.reshape(batch, S5_GROUPS, S5_STATE), hi.reshape(batch, S5_GROUPS, S5_STATE)


def s5_sample(proj, d, mats, h0r, h0i, batch, length):
    bre, bim, cre, cim, ar, ai, _, _ = mats
    rows = batch * length
    ucol = OFF_U // S5_TU
    kern = functools.partial(_s5_sample_kernel, nseq=batch, steps=length)
    perm, permt = _s5_perm(batch, length)
    st = jax.ShapeDtypeStruct((batch, S5_LANES), f32)
    state_spec = pl.BlockSpec((batch, S5_TL), lambda g: (0, g))
    perm_spec = pl.BlockSpec((rows, rows), lambda g: (0, 0))
    y, hr, hi = pl.pallas_call(
        kern,
        grid=(S5_NGT,),
        in_specs=_s5_common_specs(rows, lambda g: (0, ucol + g)) + [perm_spec, perm_spec, state_spec, state_spec],
        out_specs=[pl.BlockSpec((rows, S5_TU), lambda g: (0, g)), state_spec, state_spec],
        out_shape=[jax.ShapeDtypeStruct((rows, D_MODEL), bf16), st, st],
        scratch_shapes=[pltpu.VMEM((rows, S5_TL), f32), pltpu.VMEM((rows, S5_TL), f32)],
        compiler_params=_cparams("arbitrary"),
        name="s5_sample",
    )(proj, d, bre, bim, cre, cim, ar, ai, perm, permt, h0r.reshape(batch, S5_LANES), h0i.reshape(batch, S5_LANES))
    return y, hr.reshape(batch, S5_GROUPS, S5_STATE), hi.reshape(batch, S5_GROUPS, S5_STATE)


_SSD_PAD = 8
_SSD_TP = 128


def _split3(v):
    p0 = v.astype(bf16)
    r0 = v - p0.astype(f32)
    p1 = r0.astype(bf16)
    p2 = (r0 - p1.astype(f32)).astype(bf16)
    return p0, p1, p2


def _ssd_conv(c, nchunks, xbc_ref, conv0_ref, cw_ref, cb_ref, convnew_ref, stage, lc):
    keep = SSD_CONV_W - 1
    lo = _SSD_PAD - keep

    @pl.when(c == 0)
    def _():
        if conv0_ref is not None:
            stage[lo:_SSD_PAD, :] = conv0_ref[0]
        else:
            stage[0:_SSD_PAD, :] = jnp.zeros((_SSD_PAD, SSD_CONV_DIM), f32)

    stage[_SSD_PAD:_SSD_PAD + lc, :] = xbc_ref[...]
    xc = cb_ref[...]
    for j in range(SSD_CONV_W):
        xc = xc + cw_ref[j:j + 1, :] * stage[lo + j:lo + j + lc, :]
    xc = xc * _sigmoid(xc)
    tail = stage[lo + lc:_SSD_PAD + lc, :]

    @pl.when(c == nchunks - 1)
    def _():
        convnew_ref[0] = tail

    stage[lo:_SSD_PAD, :] = tail
    return xc


def _ssd_dt_cumsum(dt_raw, dtb_ref, alog_ref, lc):
    dtx = dt_raw + dtb_ref[...]
    dtv = jnp.maximum(dtx, 0.0) + jnp.log1p(jnp.exp(-jnp.abs(dtx)))
    da = dtv * (-jnp.exp(alog_ref[...]))
    ri = lax.broadcasted_iota(jnp.int32, (lc, lc), 0)
    ci = lax.broadcasted_iota(jnp.int32, (lc, lc), 1)
    tri = jnp.where(ri >= ci, 1.0, 0.0).astype(bf16)
    da3 = _split3(da)
    cs = _dot(tri, da3[0]) + _dot(tri, da3[1]) + _dot(tri, da3[2])
    return dtv, cs, da3


def _ssd_parse(refs, has_init, n_alias, n_fixed):
    fixed = refs[:n_fixed]
    conv0_ref, h0_ref = (refs[n_fixed], refs[n_fixed + 1]) if has_init else (None, None)
    rest = refs[n_fixed + 2 * has_init + n_alias:]
    return fixed, conv0_ref, h0_ref, rest


def _ssd_wide_kernel(*refs, nsteps, per_step, has_init, n_alias):
    lc = SSD_HEADDIM
    z_ref, xc_ref, dt_ref, dtb_ref, alog_ref, dexp_ref, ng_ref, rep_ref = refs[:8]
    h0_ref = refs[8] if has_init else None
    y_ref, hnew_ref, ht_scr, cm_scr, dx_scr = refs[8 + has_init + n_alias:]
    c = pl.program_id(1)

    @pl.when(c == 0)
    def _():
        if has_init:
            for h in range(SSD_HEADS):
                ht_scr[:, h * lc:(h + 1) * lc] = h0_ref[0, h].T
        else:
            ht_scr[...] = jnp.zeros_like(ht_scr)

    pw = 2 * SSD_HEADDIM
    gw = SSD_HPG * SSD_HEADDIM
    ti = lax.broadcasted_iota(jnp.int32, (lc, gw), 0)
    si = lax.broadcasted_iota(jnp.int32, (lc, gw), 1) % lc
    causal = ti >= si
    bdmask = (lax.broadcasted_iota(jnp.int32, (pw, pw), 0) // lc) == (lax.broadcasted_iota(jnp.int32, (pw, pw), 1) // lc)
    low_half = lax.broadcasted_iota(jnp.int32, (1, pw), 1) < lc
    for cc in range(per_step):
        _ssd_wide_chunk(slice(cc * lc, (cc + 1) * lc), z_ref, xc_ref, dt_ref, dtb_ref, alog_ref, dexp_ref, ng_ref,
                        rep_ref, y_ref, ht_scr, cm_scr, dx_scr, causal, bdmask, low_half)

    @pl.when(c == nsteps - 1)
    def _():
        for h in range(SSD_HEADS):
            hnew_ref[0, h] = ht_scr[:, h * lc:(h + 1) * lc].T


def _ssd_wide_chunk(rows, z_ref, xc_ref, dt_ref, dtb_ref, alog_ref, dexp_ref, ng_ref, rep_ref, y_ref, ht_scr, cm_scr,
                    dx_scr, causal, bdmask, low_half):
    lc = SSD_HEADDIM
    pw = 2 * SSD_HEADDIM
    gw = SSD_HPG * SSD_HEADDIM
    dtv, cs, da3 = _ssd_dt_cumsum(dt_ref[rows, :], dtb_ref, alog_ref, lc)
    upper2 = jnp.where(lax.broadcasted_iota(jnp.int32, (lc, pw), 0) <= lax.broadcasted_iota(jnp.int32, (lc, pw), 1) % lc,
                       1.0, 0.0).astype(bf16)
    cst2 = _dot_tn(da3[0], upper2) + _dot_tn(da3[1], upper2) + _dot_tn(da3[2], upper2)
    rep = rep_ref[...]
    c0, c1, c2 = _split3(cs)
    cm_scr[rows, :] = _dot(c0, rep) + _dot(c1, rep) + _dot(c2, rep)
    d0, d1, d2 = _split3(dtv)
    dx_scr[rows, :] = _dot(d0, rep) + _dot(d1, rep) + _dot(d2, rep)

    b_off, c_off = SSD_INNER, SSD_INNER + SSD_GROUPS * SSD_STATE
    for g in range(SSD_GROUPS):
        gs = slice(g * gw, (g + 1) * gw)
        cm = cm_scr[rows, gs]
        row = jnp.concatenate(
            [jnp.where(low_half, cst2[h0:h0 + 1, :], cst2[h0 + 1:h0 + 2, :])
             for h0 in range(g * SSD_HPG, (g + 1) * SSD_HPG, 2)], axis=1)
        decay = jnp.exp(jnp.where(causal, cm - row, -jnp.inf))
        last = cm[lc - 1:lc, :]
        xg = xc_ref[rows, gs]
        xdt = xg * dx_scr[rows, gs]
        bg = xc_ref[rows, b_off + g * SSD_STATE:b_off + (g + 1) * SSD_STATE].astype(bf16)
        cg = xc_ref[rows, c_off + g * SSD_STATE:c_off + (g + 1) * SSD_STATE].astype(bf16)
        cb2 = _dot_nt(cg, jnp.concatenate([bg, bg], axis=0))
        parts = []
        for pr in range(gw // pw):
            ps = slice(pr * pw, (pr + 1) * pw)
            m = (cb2 * decay[:, ps]).astype(bf16)
            xp = xdt[:, ps]
            bd = jnp.where(bdmask, jnp.concatenate([xp, xp], axis=0), 0.0).astype(bf16)
            parts.append(_dot(m, bd))
        y = jnp.concatenate(parts, axis=1)
        ht = ht_scr[:, gs]
        y = y + _dot(cg, ht.astype(bf16)) * jnp.exp(cm)
        xw = (xdt * jnp.exp(last - cm)).astype(bf16)
        ht_scr[:, gs] = jnp.exp(last) * ht + _dot_tn(bg, xw)
        y = y + dexp_ref[:, gs] * xg
        yg = y * z_ref[rows, gs]
        yg = yg * lax.rsqrt(jnp.mean(yg * yg, axis=-1, keepdims=True) + EPS) * ng_ref[:, gs]
        y_ref[rows, gs] = yg.astype(bf16)


def _ssd_kernel(*refs, lc, nchunks, has_init, n_alias):
    fixed, conv0_ref, h0_ref, rest = _ssd_parse(refs, has_init, n_alias, 9)
    z_ref, xbc_ref, dt_ref, cw_ref, cb_ref, dtb_ref, alog_ref, dexp_ref, ng_ref = fixed
    y_ref, hnew_ref, convnew_ref, stage, h_scr, pad_scr, y_scr = rest
    c = pl.program_id(1)

    @pl.when(c == 0)
    def _():
        if has_init:
            h_scr[...] = h0_ref[0]
        else:
            h_scr[...] = jnp.zeros_like(h_scr)

    xc = _ssd_conv(c, nchunks, xbc_ref, conv0_ref, cw_ref, cb_ref, convnew_ref, stage, lc)
    dtv, cs, _ = _ssd_dt_cumsum(dt_ref[...], dtb_ref, alog_ref, lc)
    causal = lax.broadcasted_iota(jnp.int32, (lc, lc), 0) >= lax.broadcasted_iota(jnp.int32, (lc, lc), 1)
    pad_scr[...] = jnp.zeros_like(pad_scr)
    pad_scr[0:lc, 0:SSD_HEADS] = cs
    cst = pad_scr[...].T
    ecs = jnp.exp(cs)
    cs_last = cs[lc - 1:lc, :]
    wdec = jnp.exp(cs_last - cs)
    elast = jnp.exp(cs_last)
    xs_off, b_off, c_off = 0, SSD_INNER, SSD_INNER + SSD_GROUPS * SSD_STATE
    for g in range(SSD_GROUPS):
        bg = xc[:, b_off + g * SSD_STATE:b_off + (g + 1) * SSD_STATE].astype(bf16)
        cg = xc[:, c_off + g * SSD_STATE:c_off + (g + 1) * SSD_STATE].astype(bf16)
        cb = _dot_nt(cg, bg)
        for r in range(SSD_HPG):
            h = g * SSD_HPG + r
            hs = slice(h * SSD_HEADDIM, (h + 1) * SSD_HEADDIM)
            col = cs[:, h:h + 1]
            row = cst[h:h + 1, 0:lc]
            decay = jnp.exp(jnp.where(causal, col - row, -jnp.inf))
            xh = xc[:, hs]
            xdt = xh * dtv[:, h:h + 1]
            hst = h_scr[h]
            y = _dot((cb * decay).astype(bf16), xdt.astype(bf16))
            y = y + _dot_nt(cg, hst.astype(bf16)) * ecs[:, h:h + 1]
            xw = (xdt * wdec[:, h:h + 1]).astype(bf16)
            h_scr[h] = elast[:, h:h + 1] * hst + _dot_tn(xw, bg)
            y_scr[:, hs] = y + dexp_ref[:, hs] * xh
        gs = slice(g * SSD_HPG * SSD_HEADDIM, (g + 1) * SSD_HPG * SSD_HEADDIM)
        yg = y_scr[:, gs] * z_ref[:, gs]
        yg = yg * lax.rsqrt(jnp.mean(yg * yg, axis=-1, keepdims=True) + EPS) * ng_ref[:, gs]
        y_ref[:, gs] = yg.astype(bf16)

    @pl.when(c == nchunks - 1)
    def _():
        hnew_ref[0] = h_scr[...]


def ssd(proj, dt_raw, conv_w, conv_b, dt_bias, a_log, d, norm_g, batch, length, *, h_all, conv_all, prev_h,
        prev_conv, layer, depth):
    lc = min(length, CHUNK)
    nc = length // lc
    dexp = jnp.repeat(d, SSD_HEADDIM).reshape(1, SSD_INNER)
    row = lambda b, c: (b * nc + c, 0)
    const2 = lambda b, c: (0, 0)
    h_tail = (SSD_HEADS, SSD_HEADDIM, SSD_STATE)
    c_tail = (SSD_CONV_W - 1, SSD_CONV_DIM)
    has_init = h_all is not None
    has_prev = prev_h is not None
    h_in, h_specs, h_out_spec = _layered_state_specs(layer * batch, h_tail, h_all, None)
    c_in, c_specs, c_out_spec = _layered_state_specs(layer * batch, c_tail, conv_all, None)
    ins = [proj, proj, dt_raw, conv_w, conv_b.reshape(1, -1), dt_bias.reshape(1, -1), a_log.reshape(1, -1), dexp,
           norm_g.reshape(1, -1)]
    specs = [pl.BlockSpec((lc, SSD_INNER), lambda b, c: (b * nc + c, OFF_Z // SSD_INNER)),
             pl.BlockSpec((lc, SSD_CONV_DIM), lambda b, c: (b * nc + c, OFF_XBC // SSD_CONV_DIM)),
             pl.BlockSpec((lc, SSD_HEADS), row),
             pl.BlockSpec((SSD_CONV_W, SSD_CONV_DIM), const2),
             pl.BlockSpec((1, SSD_CONV_DIM), const2),
             pl.BlockSpec((1, SSD_HEADS), const2),
             pl.BlockSpec((1, SSD_HEADS), const2),
             pl.BlockSpec((1, SSD_INNER), const2),
             pl.BlockSpec((1, SSD_INNER), const2)]
    kern = functools.partial(_ssd_kernel, lc=lc, nchunks=nc, has_init=has_init, n_alias=2 * has_prev)
    scratch = [pltpu.VMEM((_SSD_PAD + lc, SSD_CONV_DIM), f32),
               pltpu.VMEM(h_tail, f32),
               pltpu.VMEM((_SSD_TP, _SSD_TP), f32),
               pltpu.VMEM((lc, SSD_INNER), f32)]
    ins += c_in + h_in
    specs += c_specs + h_specs
    aliases = {}
    if has_prev:
        aliases = {len(ins): 1, len(ins) + 1: 2}
        ins += [prev_h, prev_conv]
        specs += [pl.BlockSpec(memory_space=pl.ANY), pl.BlockSpec(memory_space=pl.ANY)]
    return pl.pallas_call(
        kern,
        grid=(batch, nc),
        in_specs=specs,
        out_specs=[pl.BlockSpec((lc, SSD_INNER), row), h_out_spec, c_out_spec],
        out_shape=[jax.ShapeDtypeStruct((batch * length, SSD_INNER), bf16),
                   jax.ShapeDtypeStruct((depth * batch,) + h_tail, f32),
                   jax.ShapeDtypeStruct((depth * batch,) + c_tail, f32)],
        scratch_shapes=scratch,
        input_output_aliases=aliases,
        compiler_params=_cparams("parallel", "arbitrary"),
        name="ssd",
    )(*ins)


def ssd_wide(proj, dt_raw, dt_bias, a_log, d, norm_g, batch, length, *, h_all, prev_h, layer, depth):
    lc = SSD_HEADDIM
    per_step = max(p for p in (4, 2, 1) if length % (p * lc) == 0)
    rows = per_step * lc
    nc = length // rows
    dexp = jnp.repeat(d, SSD_HEADDIM).reshape(1, SSD_INNER)
    rep = jnp.repeat(jnp.eye(SSD_HEADS, dtype=bf16), SSD_HEADDIM, axis=1)
    row = lambda b, c: (b * nc + c, 0)
    const2 = lambda b, c: (0, 0)
    h_tail = (SSD_HEADS, SSD_HEADDIM, SSD_STATE)
    h_in, h_specs, h_out_spec = _layered_state_specs(layer * batch, h_tail, h_all, prev_h)
    ins = [proj, proj, dt_raw, dt_bias.reshape(1, -1), a_log.reshape(1, -1), dexp, norm_g.reshape(1, -1), rep]
    specs = [pl.BlockSpec((rows, SSD_INNER), lambda b, c: (b * nc + c, OFF_Z // SSD_INNER)),
             pl.BlockSpec((rows, SSD_CONV_DIM), lambda b, c: (b * nc + c, OFF_XBC // SSD_CONV_DIM)),
             pl.BlockSpec((rows, SSD_HEADS), row),
             pl.BlockSpec((1, SSD_HEADS), const2),
             pl.BlockSpec((1, SSD_HEADS), const2),
             pl.BlockSpec((1, SSD_INNER), const2),
             pl.BlockSpec((1, SSD_INNER), const2),
             pl.BlockSpec((SSD_HEADS, SSD_INNER), const2)]
    aliases = {len(ins) + len(h_in) - 1: 1} if prev_h is not None else {}
    kern = functools.partial(_ssd_wide_kernel, nsteps=nc, per_step=per_step, has_init=h_all is not None,
                             n_alias=int(prev_h is not None))
    return pl.pallas_call(
        kern,
        grid=(batch, nc),
        in_specs=specs + h_specs,
        out_specs=[pl.BlockSpec((rows, SSD_INNER), row), h_out_spec],
        out_shape=[jax.ShapeDtypeStruct((batch * length, SSD_INNER), bf16),
                   jax.ShapeDtypeStruct((depth * batch,) + h_tail, f32)],
        scratch_shapes=[pltpu.VMEM((SSD_STATE, SSD_INNER), f32),
                        pltpu.VMEM((rows, SSD_INNER), f32),
                        pltpu.VMEM((rows, SSD_INNER), f32)],
        input_output_aliases=aliases,
        compiler_params=_cparams("parallel", "arbitrary"),
        name="ssd_wide",
    )(*ins, *h_in)


def _merge_kernel(ra_ref, sa_ref, da_ref, gr_ref, gs_ref, gd_ref, wr_ref, wv_ref, wg_ref, wd_ref, o_ref):
    sa = sa_ref[...]
    ret = _dot(ra_ref[...], wr_ref[...])
    s5 = _dot(sa, wv_ref[...]) * _sigmoid(_dot(sa, wg_ref[...]))
    sd = _dot(da_ref[...], wd_ref[...])
    merged = gr_ref[...] * ret + gs_ref[...] * s5 + gd_ref[...] * sd
    o_ref[...] = merged.astype(bf16)


def merge(ret_act, s5_act, ssd_act, gates, w_ret_o, w_glu, w_ssd_out, *, tm, tn):
    n = ret_act.shape[0]
    nj = D_MODEL // tn
    act = lambda i, j: (i, 0)
    return pl.pallas_call(
        _merge_kernel,
        grid=(n // tm, nj),
        in_specs=[pl.BlockSpec((tm, R_VAL), act),
                  pl.BlockSpec((tm, D_MODEL), act),
                  pl.BlockSpec((tm, SSD_INNER), act),
                  pl.BlockSpec((tm, tn), lambda i, j: (i, j)),
                  pl.BlockSpec((tm, tn), lambda i, j: (i, nj + j)),
                  pl.BlockSpec((tm, tn), lambda i, j: (i, 2 * nj + j)),
                  pl.BlockSpec((R_VAL, tn), lambda i, j: (0, j)),
                  pl.BlockSpec((D_MODEL, tn), lambda i, j: (0, j)),
                  pl.BlockSpec((D_MODEL, tn), lambda i, j: (0, nj + j)),
                  pl.BlockSpec((SSD_INNER, tn), lambda i, j: (0, j))],
        out_specs=pl.BlockSpec((tm, tn), lambda i, j: (i, j)),
        out_shape=jax.ShapeDtypeStruct((n, D_MODEL), bf16),
        compiler_params=_cparams("parallel", "arbitrary"),
        name="merge",
    )(ret_act, s5_act, ssd_act, gates, gates, gates, w_ret_o, w_glu, w_glu, w_ssd_out)


def _attn_kernel(q_ref, k_ref, v_ref, o_ref):
    scale = X_HEAD_DIM ** -0.5
    for h in range(X_HEADS):
        hs = slice(h * X_HEAD_DIM, (h + 1) * X_HEAD_DIM)
        s = _dot_nt(q_ref[:, hs], k_ref[0, :, hs].astype(bf16)) * scale
        e = jnp.exp(s - jnp.max(s, axis=-1, keepdims=True))
        p = e / jnp.sum(e, axis=-1, keepdims=True)
        o_ref[:, hs] = _dot(p.astype(bf16), v_ref[0, :, hs].astype(bf16)).astype(bf16)


def attention(q, mem_k, mem_v, batch, length, *, tq, kv_base):
    nt = length // tq
    kv = pl.BlockSpec((1, MEM_LEN, D_MODEL), lambda b, i: (kv_base + b, 0, 0))
    return pl.pallas_call(
        _attn_kernel,
        grid=(batch, nt),
        in_specs=[pl.BlockSpec((tq, D_MODEL), lambda b, i: (b * nt + i, 0)), kv, kv],
        out_specs=pl.BlockSpec((tq, D_MODEL), lambda b, i: (b * nt + i, 0)),
        out_shape=jax.ShapeDtypeStruct((batch * length, D_MODEL), bf16),
        compiler_params=_cparams("parallel", "arbitrary"),
        name="attention",
    )(q, mem_k, mem_v)


def _mlp_kernel(x_ref, g_in_ref, g_out_ref, wu_ref, wd_ref, o_ref, hn_scr, acc_scr, *, nff):
    j = pl.program_id(1)

    @pl.when(j == 0)
    def _():
        hn_scr[...] = _rms(x_ref[...], g_in_ref[...]).astype(bf16)
        acc_scr[...] = jnp.zeros_like(acc_scr)

    h = jnp.maximum(_dot(hn_scr[...], wu_ref[...]), 0.0)
    acc_scr[...] += _dot((h * h).astype(bf16), wd_ref[...])

    @pl.when(j == nff - 1)
    def _():
        o_ref[...] = x_ref[...] + _rms(acc_scr[...], g_out_ref[...])


def mlp(x, g_in, g_out, w_up, w_down, *, tm, tf):
    n, d = x.shape
    nff = D_FF // tf
    kern = functools.partial(_mlp_kernel, nff=nff)
    return pl.pallas_call(
        kern,
        grid=(n // tm, nff),
        in_specs=[pl.BlockSpec((tm, d), lambda i, j: (i, 0)),
                  pl.BlockSpec((1, d), lambda i, j: (0, 0)),
                  pl.BlockSpec((1, d), lambda i, j: (0, 0)),
                  pl.BlockSpec((d, tf), lambda i, j: (0, j)),
                  pl.BlockSpec((tf, d), lambda i, j: (j, 0))],
        out_specs=pl.BlockSpec((tm, d), lambda i, j: (i, 0)),
        out_shape=jax.ShapeDtypeStruct((n, d), f32),
        scratch_shapes=[pltpu.VMEM((tm, d), bf16), pltpu.VMEM((tm, d), f32)],
        compiler_params=_cparams("parallel", "arbitrary"),
        name="mlp",
    )(x, g_in, g_out, w_up, w_down)


def _run_layer(x, batch, length, pos0, mem_k, mem_v, kv_base, carried, prev, layer, depth, lw, *, prompt):
    gains = lw["gains"]
    n = x.shape[0]
    tm = min(512, n)
    tm_in = min(1024, n)
    conv_args = {}
    if prompt:
        conv_args = dict(conv_w=lw["ssd_conv_w"], conv_b=lw["ssd_conv_b"], seq_len=length)
    proj, tails = inproj(x, gains[0:1], lw["w_in_all"], layer, tm=tm_in, **conv_args)
    dt_raw = norm_matmul(x, gains[0:1], lw["w_dt"], tm=tm_in, tn=SSD_HEADS, out_dtype=f32)
    gates = norm_matmul(x, gains[0:1], lw["w_gate"], tm=tm_in, tn=1024, out_dtype=f32, sigmoid=True)
    ret_act, ret_all = retention(proj, lw["ret_gn"], batch, length, pos0, state_all=carried["ret"],
                                 prev_out=prev["ret"], layer=layer, depth=depth)
    if prompt:
        s5_act, s5_r_new, s5_i_new = s5_prompt(proj, lw["s5_d"], lw["s5_mats"], batch, length)
    else:
        s5_act, s5_r_new, s5_i_new = s5_sample(proj, lw["s5_d"], lw["s5_mats"], carried["s5_r"], carried["s5_i"],
                                               batch, length)
    if prompt:
        ssd_act, ssd_all = ssd_wide(proj, dt_raw, lw["ssd_dt_bias"], lw["ssd_a_log"], lw["ssd_d"], lw["ssd_norm"],
                                    batch, length, h_all=carried["ssd"], prev_h=prev["ssd"], layer=layer, depth=depth)
        tiles_per_seq = length // tm_in
        seq_tails = tails.reshape(batch, tiles_per_seq, _CONV_PAD, SSD_CONV_DIM)[:, -1, _CONV_PAD - (SSD_CONV_W - 1):]
        conv_all = seq_tails if prev["conv"] is None else jnp.concatenate([prev["conv"], seq_tails], axis=0)
    else:
        ssd_act, ssd_all, conv_all = ssd(proj, dt_raw, lw["ssd_conv_w"], lw["ssd_conv_b"], lw["ssd_dt_bias"],
                                         lw["ssd_a_log"], lw["ssd_d"], lw["ssd_norm"], batch, length,
                                         h_all=carried["ssd"], conv_all=carried["conv"], prev_h=prev["ssd"],
                                         prev_conv=prev["conv"], layer=layer, depth=depth)
    merged = merge(ret_act, s5_act, ssd_act, gates, lw["w_ret_o"], lw["w_s5_glu"], lw["w_ssd_out"], tm=tm, tn=512)
    x = matmul_norm_res(merged, lw["w_mix_out"], gains[1:2], x, tm=tm)
    q = norm_matmul(x, gains[2:3], lw["w_xq"], tm=tm_in, tn=1024, out_dtype=bf16)
    att = attention(q, mem_k, mem_v, batch, length, tq=min(length, 512), kv_base=kv_base)
    x = matmul_norm_res(att, lw["w_xo"], gains[3:4], x, tm=tm)
    x = mlp(x, gains[4:5], gains[5:6], lw["w_up"], lw["w_down"], tm=tm, tf=1024)
    return x, dict(ret=ret_all, ssd=ssd_all, conv=conv_all), (s5_r_new, s5_i_new)


def kernel(x_prompt, x_sample, mem_prompt, state_ret, state_s5_re, state_s5_im, state_ssd, cache_ssd_conv,
           cache_mem_k, cache_mem_v, norm_gains, w_in, ret_gn, w_ret_o, s5_a_re, s5_a_im, s5_b_re, s5_b_im,
           s5_c_re, s5_c_im, s5_d, s5_log_dt, w_s5_glu, ssd_conv_w, ssd_conv_b, ssd_dt_bias, ssd_a_log, ssd_d,
           ssd_norm, w_ssd_out, w_mix_out, w_xq, w_xkv, w_xo, w_up, w_down):
    b, l, _ = x_prompt.shape
    db, dl, _ = x_sample.shape
    depth = w_in.shape[0]
    yp = x_prompt.reshape(b * l, D_MODEL)
    ys = x_sample.reshape(db * dl, D_MODEL)
    mem2d = mem_prompt.reshape(b * MEM_LEN, D_MODEL)
    cache_k = cache_mem_k.reshape(depth * db, MEM_LEN, D_MODEL)
    cache_v = cache_mem_v.reshape(depth * db, MEM_LEN, D_MODEL)
    s_carried = dict(ret=state_ret.reshape(depth * db, R_HEADS, R_DK, R_DV),
                     ssd=state_ssd.reshape(depth * db, SSD_HEADS, SSD_HEADDIM, SSD_STATE),
                     conv=cache_ssd_conv.reshape(depth * db, SSD_CONV_W - 1, SSD_CONV_DIM))
    p_carried = dict(ret=None, ssd=None, conv=None)
    p_prev = dict(ret=None, ssd=None, conv=None)
    s_prev = dict(ret=None, ssd=None, conv=None)
    p_s5, s_s5, p_mk, p_mv = [], [], [], []
    w_in_bf = w_in.astype(bf16)
    for i in range(depth):
        ar, ai, lr, li, bbr, bbi = s5_prep(s5_a_re[i], s5_a_im[i], s5_log_dt[i],
                                           jnp.swapaxes(s5_b_re[i], 1, 2), jnp.swapaxes(s5_b_im[i], 1, 2))
        lane = lambda t: t.reshape(1, S5_LANES)
        lw = dict(
            gains=norm_gains[i],
            w_in_all=w_in_bf,
            w_dt=w_in_bf[i, :, OFF_DT:OFF_GATE],
            w_gate=w_in_bf[i, :, OFF_GATE:],
            ret_gn=ret_gn[i].reshape(1, R_VAL),
            w_ret_o=w_ret_o[i].astype(bf16),
            s5_d=s5_d[i].reshape(1, D_MODEL),
            s5_mats=(_s5_blockdiag_in(bbr), _s5_blockdiag_in(bbi),
                     _s5_blockdiag_out(s5_c_re[i]), _s5_blockdiag_out(s5_c_im[i]),
                     lane(ar), lane(ai), lane(lr), lane(li)),
            w_s5_glu=w_s5_glu[i].astype(bf16),
            ssd_conv_w=ssd_conv_w[i], ssd_conv_b=ssd_conv_b[i], ssd_dt_bias=ssd_dt_bias[i],
            ssd_a_log=ssd_a_log[i], ssd_d=ssd_d[i], ssd_norm=ssd_norm[i],
            w_ssd_out=w_ssd_out[i].astype(bf16),
            w_mix_out=w_mix_out[i].astype(bf16),
            w_xq=w_xq[i].astype(bf16),
            w_xo=w_xo[i].astype(bf16),
            w_up=w_up[i].astype(bf16),
            w_down=w_down[i].astype(bf16),
        )
        kv = norm_matmul(mem2d, norm_gains[i, 6:7], w_xkv[i].astype(bf16), tm=b * MEM_LEN, tn=512, out_dtype=f32)
        mk = kv[:, :D_MODEL].reshape(b, MEM_LEN, D_MODEL)
        mv = kv[:, D_MODEL:].reshape(b, MEM_LEN, D_MODEL)
        p_mk.append(mk.reshape(b, MEM_LEN, X_HEADS, X_HEAD_DIM))
        p_mv.append(mv.reshape(b, MEM_LEN, X_HEADS, X_HEAD_DIM))
        yp, p_prev, s5p = _run_layer(yp, b, l, 0, mk, mv, 0, p_carried, p_prev, i, depth, lw, prompt=True)
        p_s5.append(s5p)
        s_carried_i = dict(s_carried, s5_r=state_s5_re[i], s5_i=state_s5_im[i])
        ys, s_prev, s5s = _run_layer(ys, db, dl, PAST_LEN, cache_k, cache_v, i * db, s_carried_i, s_prev, i, depth,
                                     lw, prompt=False)
        s_s5.append(s5s)
    p_s5_re, p_s5_im = [jnp.stack(t) for t in zip(*p_s5)]
    s_s5_re, s_s5_im = [jnp.stack(t) for t in zip(*s_s5)]
    unstack = lambda t, bsz: t.reshape((depth, bsz) + t.shape[1:])
    return (yp.reshape(b, l, D_MODEL), ys.reshape(db, dl, D_MODEL),
            unstack(p_prev["ret"], b), p_s5_re, p_s5_im, unstack(p_prev["ssd"], b), unstack(p_prev["conv"], b),
            jnp.stack(p_mk), jnp.stack(p_mv),
            unstack(s_prev["ret"], db), s_s5_re, s_s5_im, unstack(s_prev["ssd"], db), unstack(s_prev["conv"], db))
```
